```python
import math
import jax
import jax.numpy as jnp
from jax import lax
import numpy as np

D_MODEL = 1024
BATCH = 8
SEQ = 4096
DEPTH = 4

CTX_LEN = 256
GRID_W = 64
N_MOD = 6
EPS = 1e-6

N_Q_HEADS = 8
N_KV_HEADS = 2
HEAD_DIM = 64
KV_REP = N_Q_HEADS // N_KV_HEADS
ATTN_WIDTH = N_Q_HEADS * HEAD_DIM
KV_WIDTH = N_KV_HEADS * HEAD_DIM
ATTN_SCALE = HEAD_DIM ** -0.5
Q_BLOCK = 128
ROPE_THETA = 10000.0

HYENA_WIDTH = D_MODEL - ATTN_WIDTH
SHORT_CONV = 3
FILTER_EMB_DIM = 33
FILTER_HIDDEN = 64
DECAY_TARGET = 1e-2
FAST_DECAY_PCT = 0.3
SLOW_DECAY_PCT = 1.5

PROJ_WIDTH = ATTN_WIDTH + 2 * KV_WIDTH + 3 * HYENA_WIDTH

N_EXPERTS = 16
N_EXPERT_GROUPS = 4
EXPERTS_PER_GROUP = N_EXPERTS // N_EXPERT_GROUPS
TOP_K = 2
D_FF_EXPERT = 512

kernel_name = 'hybrid_hyena_gqa_grouped_moe_prefix_dit'


def rms_norm(x, g):
    xf = x.astype(jnp.float32)
    y = xf * lax.rsqrt(jnp.mean(xf * xf, axis=-1, keepdims=True) + EPS)
    return (y * g.astype(jnp.float32)).astype(x.dtype)


def modulate(h, shift, scale):
    return h * (1 + scale) + shift


def axial_rope(x):
    L = x.shape[1]
    rows = L // GRID_W
    row = jnp.repeat(jnp.arange(rows), GRID_W).astype(jnp.float32)
    col = jnp.tile(jnp.arange(GRID_W), rows).astype(jnp.float32)
    half = HEAD_DIM // 2
    quarter = half // 2
    inv_freq = ROPE_THETA ** (-jnp.arange(quarter, dtype=jnp.float32) / quarter)

    def rotate(xh, pos):
        ang = pos[:, None] * inv_freq[None, :]
        cos = jnp.cos(ang)[None, :, None, :]
        sin = jnp.sin(ang)[None, :, None, :]
        a, b = xh[..., :quarter], xh[..., quarter:]
        return jnp.concatenate([a * cos - b * sin, b * cos + a * sin], axis=-1)

    xf = x.astype(jnp.float32)
    out = jnp.concatenate([rotate(xf[..., :half], row), rotate(xf[..., half:], col)], axis=-1)
    return out.astype(x.dtype)


def split_projection(p):
    B, L, _ = p.shape
    q = p[..., :ATTN_WIDTH].reshape(B, L, N_Q_HEADS, HEAD_DIM)
    k = p[..., ATTN_WIDTH:ATTN_WIDTH + KV_WIDTH].reshape(B, L, N_KV_HEADS, HEAD_DIM)
    v = p[..., ATTN_WIDTH + KV_WIDTH:ATTN_WIDTH + 2 * KV_WIDTH].reshape(B, L, N_KV_HEADS, HEAD_DIM)
    u = p[..., ATTN_WIDTH + 2 * KV_WIDTH:]
    return q, k, v, u


def gqa_attend(q, keys, vals):
    s = jnp.einsum('bqgrd,bkgd->bgrqk', q, keys, preferred_element_type=jnp.float32) * ATTN_SCALE
    p = jax.nn.softmax(s, axis=-1).astype(vals.dtype)
    return jnp.einsum('bgrqk,bkgd->bqgrd', p, vals)


def latent_attention(q, k, v, k_ctx, v_ctx):
    B, L = q.shape[:2]
    keys = jnp.concatenate([k_ctx, k], axis=1)
    vals = jnp.concatenate([v_ctx, v], axis=1)
    qb = q.reshape(B, L // Q_BLOCK, Q_BLOCK, N_KV_HEADS, KV_REP, HEAD_DIM).transpose(1, 0, 2, 3, 4, 5)
    o = lax.map(lambda qi: gqa_attend(qi, keys, vals), qb)
    return o.transpose(1, 0, 2, 3, 4, 5).reshape(B, L, ATTN_WIDTH)


def context_attention(q, k, v):
    B, C = q.shape[:2]
    o = gqa_attend(q.reshape(B, C, N_KV_HEADS, KV_REP, HEAD_DIM), k, v)
    return o.reshape(B, C, ATTN_WIDTH)


def short_conv(u, w, b):
    L = u.shape[1]
    pad = (SHORT_CONV - 1) // 2
    up = jnp.pad(u, ((0, 0), (pad, SHORT_CONV - 1 - pad), (0, 0)))
    out = b
    for j in range(SHORT_CONV):
        out = out + up[:, j:j + L] * w[j]
    return out


def implicit_filters(L, fw1, fb1, fw2, fb2, fw3, ffreq):
    f32 = jnp.float32
    t = jnp.linspace(0.0, 1.0, L, dtype=f32)[:, None]
    bands = (FILTER_EMB_DIM - 1) // 2
    w = 2.0 * math.pi * jnp.arange(L, dtype=f32)[:, None] / L
    f = jnp.linspace(1e-4, bands - 1, bands, dtype=f32)[None, :]
    z = jnp.concatenate([t, jnp.cos(f * w), -jnp.sin(f * w)], axis=-1)
    freq = ffreq.astype(f32)
    h = jnp.sin(freq * (z @ fw1.astype(f32) + fb1.astype(f32)))
    h = jnp.sin(freq * (h @ fw2.astype(f32) + fb2.astype(f32)))
    h = (h @ fw3.astype(f32)).reshape(L, 2, HYENA_WIDTH)
    min_decay = math.log(DECAY_TARGET) / SLOW_DECAY_PCT
    max_decay = math.log(DECAY_TARGET) / FAST_DECAY_PCT
    deltas = jnp.linspace(min_decay, max_decay, HYENA_WIDTH, dtype=f32)
    h = h * jnp.exp(-t * jnp.abs(deltas))[:, None, :]
    h = h / jnp.sum(jnp.abs(h), axis=(0, 1), keepdims=True)
    return h[:, 0], h[:, 1]


def bidirectional_long_conv(u, h_fwd, h_bwd, bias):
    L = u.shape[1]
    kern = jnp.concatenate([h_fwd, jnp.zeros((1, HYENA_WIDTH), jnp.float32), h_bwd[1:][::-1]], axis=0)
    K = jnp.fft.rfft(kern, n=2 * L, axis=0)
    uf = u.astype(jnp.float32)
    U = jnp.fft.rfft(uf, n=2 * L, axis=1)
    y = jnp.fft.irfft(U * K[None], n=2 * L, axis=1)[:, :L]
    return (y + uf * bias.astype(jnp.float32)).astype(u.dtype)


def hyena_mixer(u, conv_w, conv_b, fw1, fb1, fw2, fb2, fw3, ffreq, fbias):
    uc = short_conv(u, conv_w, conv_b)
    x0, x1, v = jnp.split(uc, 3, axis=-1)
    h_fwd, h_bwd = implicit_filters(u.shape[1], fw1, fb1, fw2, fb2, fw3, ffreq)
    return x0 * bidirectional_long_conv(v * x1, h_fwd, h_bwd, fbias)


def route(h, router_w, router_bias):
    T = h.shape[0]
    scores = jax.nn.sigmoid((h @ router_w).astype(jnp.float32))
    sel = (scores + router_bias.astype(jnp.float32)).reshape(T, N_EXPERT_GROUPS, EXPERTS_PER_GROUP)
    group_score = jnp.sum(lax.top_k(sel, 2)[0], axis=-1)
    best_group = jnp.argmax(group_score, axis=-1)
    in_group = jax.nn.one_hot(best_group, N_EXPERT_GROUPS, dtype=jnp.bool_)[:, :, None]
    masked = jnp.where(in_group, sel, -jnp.inf).reshape(T, N_EXPERTS)
    _, idx = lax.top_k(masked, TOP_K)
    w = jnp.take_along_axis(scores, idx, axis=-1)
    w = w / jnp.sum(w, axis=-1, keepdims=True)
    return jnp.einsum('tk,tke->te', w, jax.nn.one_hot(idx, N_EXPERTS, dtype=jnp.float32))


def grouped_moe(h, router_w, router_bias, w_gate, w_up, w_down):
    B, L, D = h.shape
    ht = h.reshape(B * L, D)
    combine = route(ht, router_w, router_bias).astype(h.dtype)
    out = jnp.zeros_like(ht)
    for e in range(N_EXPERTS):
        a = jax.nn.silu(ht @ w_gate[e]) * (ht @ w_up[e])
        out = out + combine[:, e:e + 1] * (a @ w_down[e])
    return out.reshape(B, L, D)


def setup_inputs(seed: int = 0) -> dict:
    key = jax.random.key(seed)
    ks = jax.random.split(key, 32)
    D, W, E, F = D_MODEL, HYENA_WIDTH, N_EXPERTS, D_FF_EXPERT

    def nrm(k, shape, scale):
        return jax.random.normal(k, shape, jnp.float32) * scale

    return {
        'x': nrm(ks[0], (BATCH, SEQ, D), 1.0),
        'c': nrm(ks[1], (BATCH, D), 1.0),
        'ctx': nrm(ks[2], (BATCH, CTX_LEN, D), 1.0),
        'c_ctx': nrm(ks[3], (D,), 1.0),
        'w_mod': nrm(ks[4], (DEPTH, D, N_MOD * D), 0.5 * D ** -0.5),
        'b_mod': nrm(ks[5], (DEPTH, N_MOD * D), 0.02),
        'norm1_g': 1.0 + nrm(ks[6], (DEPTH, D), 0.02),
        'w_in': nrm(ks[7], (DEPTH, D, PROJ_WIDTH), D ** -0.5),
        'q_norm_g': 1.0 + nrm(ks[8], (DEPTH, HEAD_DIM), 0.02),
        'k_norm_g': 1.0 + nrm(ks[9], (DEPTH, HEAD_DIM), 0.02),
        'conv_w': nrm(ks[10], (DEPTH, SHORT_CONV, 3 * W), SHORT_CONV ** -0.5),
        'conv_b': nrm(ks[11], (DEPTH, 3 * W), 0.02),
        'filt_w1': nrm(ks[12], (DEPTH, FILTER_EMB_DIM, FILTER_HIDDEN), FILTER_EMB_DIM ** -0.5),
        'filt_b1': nrm(ks[13], (DEPTH, FILTER_HIDDEN), 0.1),
        'filt_w2': nrm(ks[14], (DEPTH, FILTER_HIDDEN, FILTER_HIDDEN), FILTER_HIDDEN ** -0.5),
        'filt_b2': nrm(ks[15], (DEPTH, FILTER_HIDDEN), 0.1),
        'filt_w3': nrm(ks[16], (DEPTH, FILTER_HIDDEN, 2 * W), FILTER_HIDDEN ** -0.5),
        'filt_freq': 1.0 + nrm(ks[17], (DEPTH, FILTER_HIDDEN), 0.02),
        'filt_bias': nrm(ks[18], (DEPTH, W), 0.5),
        'attn_out_g': 1.0 + nrm(ks[19], (DEPTH, ATTN_WIDTH), 0.02),
        'hyena_out_g': 1.0 + nrm(ks[20], (DEPTH, W), 0.02),
        'w_out': nrm(ks[21], (DEPTH, D, D), D ** -0.5),
        'norm2_g': 1.0 + nrm(ks[22], (DEPTH, D), 0.02),
        'router_w': nrm(ks[23], (D, E), D ** -0.5),
        'router_bias': nrm(ks[24], (E,), 0.01),
        'expert_w_gate': nrm(ks[25], (DEPTH, E, D, F), D ** -0.5),
        'expert_w_up': nrm(ks[26], (DEPTH, E, D, F), D ** -0.5),
        'expert_w_down': nrm(ks[27], (DEPTH, E, F, D), F ** -0.5),
    }


def reference(x, c, ctx, c_ctx, w_mod, b_mod, norm1_g, w_in, q_norm_g, k_norm_g,
              conv_w, conv_b, filt_w1, filt_b1, filt_w2, filt_b2, filt_w3, filt_freq,
              filt_bias, attn_out_g, hyena_out_g, w_out, norm2_g, router_w, router_bias,
              expert_w_gate, expert_w_up, expert_w_down):
    silu_c = jax.nn.silu(c)
    silu_cc = jax.nn.silu(c_ctx)
    for l in range(DEPTH):
        last = l == DEPTH - 1
        sh1, sc1, g1, sh2, sc2, g2 = jnp.split((silu_c @ w_mod[l] + b_mod[l])[:, None, :], N_MOD, axis=-1)
        csh1, csc1, cg1, csh2, csc2, cg2 = jnp.split(silu_cc @ w_mod[l] + b_mod[l], N_MOD, axis=-1)

        h = modulate(rms_norm(x, norm1_g[l]), sh1, sc1)
        hc = modulate(rms_norm(ctx, norm1_g[l]), csh1, csc1)
        q, k, v, u = split_projection(h @ w_in[l])
        qc, kc, vc, uc = split_projection(hc @ w_in[l])
        q = axial_rope(rms_norm(q, q_norm_g[l]))
        k = axial_rope(rms_norm(k, k_norm_g[l]))
        kc = rms_norm(kc, k_norm_g[l])
        a = latent_attention(q, k, v, kc, vc)
        y = hyena_mixer(u, conv_w[l], conv_b[l], filt_w1[l], filt_b1[l], filt_w2[l], filt_b2[l],
                        filt_w3[l], filt_freq[l], filt_bias[l])
        mix = jnp.concatenate([rms_norm(a, attn_out_g[l]), rms_norm(y, hyena_out_g[l])], axis=-1) @ w_out[l]
        x = x + g1 * mix

        h2 = modulate(rms_norm(x, norm2_g[l]), sh2, sc2)
        x = x + g2 * grouped_moe(h2, router_w, router_bias, expert_w_gate[l], expert_w_up[l], expert_w_down[l])

        if not last:
            ac = context_attention(rms_norm(qc, q_norm_g[l]), kc, vc)
            yc = hyena_mixer(uc, conv_w[l], conv_b[l], filt_w1[l], filt_b1[l], filt_w2[l], filt_b2[l],
                             filt_w3[l], filt_freq[l], filt_bias[l])
            mixc = jnp.concatenate([rms_norm(ac, attn_out_g[l]), rms_norm(yc, hyena_out_g[l])], axis=-1) @ w_out[l]
            ctx = ctx + cg1 * mixc
            hc2 = modulate(rms_norm(ctx, norm2_g[l]), csh2, csc2)
            ctx = ctx + cg2 * grouped_moe(hc2, router_w, router_bias, expert_w_gate[l], expert_w_up[l], expert_w_down[l])
    return x
```

```python
import functools
import math

import jax
import jax.numpy as jnp
from jax import lax
from jax.experimental import pallas as pl
from jax.experimental.pallas import tpu as pltpu

F32 = jnp.float32
BF16 = jnp.bfloat16
I32 = jnp.int32
HIGHEST = lax.Precision.HIGHEST

D_MODEL = 1024
N_MOD = 6
EPS = 1e-6
N_Q_HEADS = 8
N_KV_HEADS = 2
HEAD_DIM = 64
KV_REP = N_Q_HEADS // N_KV_HEADS
ATTN_WIDTH = N_Q_HEADS * HEAD_DIM
KV_WIDTH = N_KV_HEADS * HEAD_DIM
ATTN_SCALE = HEAD_DIM ** -0.5
GRID_W = 64
ROPE_THETA = 10000.0
ROPE_QUARTER = HEAD_DIM // 4
HYENA_WIDTH = D_MODEL - ATTN_WIDTH
SHORT_CONV = 3
FILTER_EMB_DIM = 33
FILTER_HIDDEN = 64
DECAY_TARGET = 1e-2
FAST_DECAY_PCT = 0.3
SLOW_DECAY_PCT = 1.5
N_EXPERTS = 16
N_EXPERT_GROUPS = 4
EXPERTS_PER_GROUP = N_EXPERTS // N_EXPERT_GROUPS
D_FF_EXPERT = 512

TM = 256
FFN_TILE = 256
CONV_CB = 128
FFT_CB = 32
FILT_CB = 128
VMEM_LIMIT = 56 * 1024 * 1024


def _params(*sem):
    return pltpu.CompilerParams(dimension_semantics=tuple(sem), vmem_limit_bytes=VMEM_LIMIT)


def _norm_mod(x, g, shift, scale):
    y = x * lax.rsqrt(jnp.mean(x * x, axis=-1, keepdims=True) + EPS)
    return (y * g) * (1 + scale) + shift


def _mod_kernel(c_ref, w_ref, b_ref, o_ref):
    c = c_ref[...]
    s = c * jax.nn.sigmoid(c)
    o_ref[0] = jnp.dot(s, w_ref[0], preferred_element_type=F32, precision=HIGHEST) + b_ref[0]


def _modulations(c, c_ctx, w_mod, b_mod):
    depth, d, nmd = w_mod.shape
    b = c.shape[0]
    rows = -(-(b + 1) // 8) * 8
    c_all = jnp.zeros((rows, d), F32).at[:b].set(c).at[b].set(c_ctx)
    tn = nmd // 4
    out = pl.pallas_call(
        _mod_kernel,
        grid=(depth, nmd // tn),
        in_specs=[pl.BlockSpec((rows, d), lambda l, j: (0, 0)),
                  pl.BlockSpec((1, d, tn), lambda l, j: (l, 0, j)),
                  pl.BlockSpec((1, 1, tn), lambda l, j: (l, 0, j))],
        out_specs=pl.BlockSpec((1, rows, tn), lambda l, j: (l, 0, j)),
        out_shape=jax.ShapeDtypeStruct((depth, rows, nmd), F32),
        compiler_params=_params("arbitrary", "arbitrary"),
        name="modulation",
    )(c_all, w_mod, b_mod.reshape(depth, 1, nmd))
    return out.reshape(depth, rows, N_MOD, d)


def _inproj_kernel(x_ref, mod_ref, g_ref, wt_ref, qg_ref, kg_ref, rope_ref,
                   q_ref, k_ref, v_ref, u_ref):
    x = x_ref[0]
    tm = x.shape[0]
    h = _norm_mod(x, g_ref[...], mod_ref[0, 0:1, :], mod_ref[0, 1:2, :])
    pt = lax.dot_general(wt_ref[...], h.astype(BF16), (((1,), (1,)), ((), ())),
                         preferred_element_type=F32)
    rope = rope_ref[...]
    qd = ROPE_QUARTER
    cr, sr = rope[0:qd][None], rope[qd:2 * qd][None]
    cc, sc = rope[2 * qd:3 * qd][None], rope[3 * qd:4 * qd][None]

    def norm_rope(t, gain, nh):
        t = t.reshape(nh, HEAD_DIM, tm)
        t = t * lax.rsqrt(jnp.mean(t * t, axis=1, keepdims=True) + EPS) * gain[None]
        a, b = t[:, 0:qd], t[:, qd:2 * qd]
        c, d = t[:, 2 * qd:3 * qd], t[:, 3 * qd:4 * qd]
        return jnp.concatenate([a * cr - b * sr, b * cr + a * sr,
                                c * cc - d * sc, d * cc + c * sc], axis=1)

    q = norm_rope(pt[0:ATTN_WIDTH], qg_ref[...], N_Q_HEADS) * ATTN_SCALE
    q_ref[0] = q.astype(BF16)
    k = norm_rope(pt[ATTN_WIDTH:ATTN_WIDTH + KV_WIDTH], kg_ref[...], N_KV_HEADS)
    kt = k.reshape(KV_WIDTH, tm).T
    for g in range(N_KV_HEADS):
        k_ref[0, g, 0] = kt[:, g * HEAD_DIM:(g + 1) * HEAD_DIM].astype(BF16)
    v = pt[ATTN_WIDTH + KV_WIDTH:ATTN_WIDTH + 2 * KV_WIDTH]
    v_ref[0, :, 0] = v.reshape(N_KV_HEADS, HEAD_DIM, tm).astype(BF16)
    u_ref[0, 0] = pt[ATTN_WIDTH + 2 * KV_WIDTH:].astype(BF16)


def _inproj(x, mods, mod_row, g, wt, qg, kg, rope):
    b, l, d = x.shape
    nt = l // TM
    p = wt.shape[0]
    uw = p - ATTN_WIDTH - 2 * KV_WIDTH
    if mod_row is None:
        mod_map = lambda bi, i: (bi, 0, 0)
    else:
        mod_map = lambda bi, i: (mod_row, 0, 0)
    return pl.pallas_call(
        _inproj_kernel,
        grid=(b, nt),
        in_specs=[pl.BlockSpec((1, TM, d), lambda bi, i: (bi, i, 0)),
                  pl.BlockSpec((1, N_MOD, d), mod_map),
                  pl.BlockSpec((1, d), lambda bi, i: (0, 0)),
                  pl.BlockSpec((p, d), lambda bi, i: (0, 0)),
                  pl.BlockSpec((HEAD_DIM, 1), lambda bi, i: (0, 0)),
                  pl.BlockSpec((HEAD_DIM, 1), lambda bi, i: (0, 0)),
                  pl.BlockSpec((HEAD_DIM, TM), lambda bi, i: (0, i))],
        out_specs=[pl.BlockSpec((1, N_Q_HEADS, HEAD_DIM, TM), lambda bi, i: (bi, 0, 0, i)),
                   pl.BlockSpec((1, N_KV_HEADS, 1, TM, HEAD_DIM), lambda bi, i: (bi, 0, i, 0, 0)),
                   pl.BlockSpec((1, N_KV_HEADS, 1, HEAD_DIM, TM), lambda bi, i: (bi, 0, i, 0, 0)),
                   pl.BlockSpec((1, 1, uw, TM), lambda bi, i: (bi, i, 0, 0))],
        out_shape=[jax.ShapeDtypeStruct((b, N_Q_HEADS, HEAD_DIM, l), BF16),
                   jax.ShapeDtypeStruct((b, N_KV_HEADS, nt, TM, HEAD_DIM), BF16),
                   jax.ShapeDtypeStruct((b, N_KV_HEADS, nt, HEAD_DIM, TM), BF16),
                   jax.ShapeDtypeStruct((b, nt, uw, TM), BF16)],
        compiler_params=_params("arbitrary", "arbitrary"),
        name="inproj",
    )(x, mods, g, wt, qg, kg, rope)


def _attn_kernel(q_ref, k_ref, v_ref, o_ref):
    nk = k_ref.shape[2]
    tq = q_ref.shape[-1]
    for r in range(KV_REP):
        q = q_ref[0, r]

        def body(j, carry):
            m, l, acc = carry
            s = jnp.dot(k_ref[0, 0, j], q, preferred_element_type=F32)
            m_new = jnp.maximum(m, jnp.max(s, axis=0, keepdims=True))
            alpha = jnp.exp(m - m_new)
            p = jnp.exp(s - m_new)
            l = alpha * l + jnp.sum(p, axis=0, keepdims=True)
            acc = alpha * acc + jnp.dot(v_ref[0, 0, j], p.astype(BF16), preferred_element_type=F32)
            return m_new, l, acc

        init = (jnp.full((1, tq), -jnp.inf, F32), jnp.zeros((1, tq), F32),
                jnp.zeros((HEAD_DIM, tq), F32))
        m, l, acc = lax.fori_loop(0, nk, body, init)
        o_ref[0, 0, r * HEAD_DIM:(r + 1) * HEAD_DIM, :] = (acc / l).astype(o_ref.dtype)


def _attention(q, k, v):
    b, _, _, l = q.shape
    nk = k.shape[2]
    nt = l // TM
    return pl.pallas_call(
        _attn_kernel,
        grid=(b, N_KV_HEADS, nt),
        in_specs=[pl.BlockSpec((1, KV_REP, HEAD_DIM, TM), lambda bi, g, i: (bi, g, 0, i)),
                  pl.BlockSpec((1, 1, nk, TM, HEAD_DIM), lambda bi, g, i: (bi, g, 0, 0, 0)),
                  pl.BlockSpec((1, 1, nk, HEAD_DIM, TM), lambda bi, g, i: (bi, g, 0, 0, 0))],
        out_specs=pl.BlockSpec((1, 1, KV_REP * HEAD_DIM, TM), lambda bi, g, i: (bi, i, g, 0)),
        out_shape=jax.ShapeDtypeStruct((b, nt, ATTN_WIDTH, TM), BF16),
        compiler_params=_params("arbitrary", "arbitrary", "arbitrary"),
        name="attention",
    )(q, k, v)


def _short_conv_tiles(u_ref, w, bias):
    nt, c, tm = u_ref.shape[1:]
    lane = lax.broadcasted_iota(I32, (c, tm), 1)
    tiles = [u_ref[0, i].astype(F32) for i in range(nt)]
    prev = [pltpu.roll(t, 1, 1) for t in tiles]
    nxt = [pltpu.roll(t, tm - 1, 1) for t in tiles]
    zero = jnp.zeros((c, tm), F32)
    out = []
    for i in range(nt):
        up = jnp.where(lane == 0, prev[i - 1] if i > 0 else zero, prev[i])
        un = jnp.where(lane == tm - 1, nxt[i + 1] if i < nt - 1 else zero, nxt[i])
        out.append(bias + up * w[:, 0:1] + tiles[i] * w[:, 1:2] + un * w[:, 2:3])
    return out


def _sconv_kernel(u0_ref, u1_ref, u2_ref, w_ref, b_ref, g_ref, x0_ref):
    x0 = _short_conv_tiles(u0_ref, w_ref[0], b_ref[0])
    x1 = _short_conv_tiles(u1_ref, w_ref[1], b_ref[1])
    v = _short_conv_tiles(u2_ref, w_ref[2], b_ref[2])
    for i in range(len(x0)):
        g_ref[0, i] = (v[i] * x1[i]).astype(g_ref.dtype)
        x0_ref[0, i] = x0[i].astype(x0_ref.dtype)


def _short_conv_gate(u, cw, cb):
    b, nt, _, _ = u.shape
    w = HYENA_WIDTH
    nc = w // CONV_CB
    uspec = lambda grp: pl.BlockSpec((1, nt, CONV_CB, TM), lambda bi, c: (bi, 0, grp * nc + c, 0))
    ospec = pl.BlockSpec((1, nt, CONV_CB, TM), lambda bi, c: (bi, 0, c, 0))
    return pl.pallas_call(
        _sconv_kernel,
        grid=(b, nc),
        in_specs=[uspec(0), uspec(1), uspec(2),
                  pl.BlockSpec((3, CONV_CB, SHORT_CONV), lambda bi, c: (0, c, 0)),
                  pl.BlockSpec((3, CONV_CB, 1), lambda bi, c: (0, c, 0))],
        out_specs=[ospec, ospec],
        out_shape=[jax.ShapeDtypeStruct((b, nt, w, TM), BF16)] * 2,
        compiler_params=_params("arbitrary", "arbitrary"),
        name="short_conv",
    )(u, u, u, cw, cb)


def _dft_consts(nt):
    n1 = 2 * nt
    n = n1 * TM
    two_pi = 2.0 * math.pi

    def cs(prod, mod):
        ang = (prod % mod).astype(F32) * (two_pi / mod)
        return jnp.cos(ang), jnp.sin(ang)

    f1 = jnp.arange(n1, dtype=I32)
    t1 = jnp.arange(nt, dtype=I32)
    t2 = jnp.arange(TM, dtype=I32)
    c, s = cs(f1[:, None] * t1[None, :], n1)
    fwd1 = jnp.concatenate([c, -s], axis=0).astype(BF16)
    c, s = cs(f1[:, None] * t2[None, :], n)
    tw = jnp.stack([c, -s])
    c, s = cs(t2[:, None] * t2[None, :], TM)
    fwd2 = jnp.stack([jnp.concatenate([c, -s], axis=1),
                      jnp.concatenate([s, c], axis=1)]).astype(BF16)
    inv2 = jnp.stack([jnp.concatenate([c, s], axis=1),
                      jnp.concatenate([-s, c], axis=1)]).astype(BF16)
    c, s = cs(t1[:, None] * f1[None, :], n1)
    inv1 = (jnp.stack([c, -s]) * (1.0 / n)).astype(BF16)
    return fwd1, tw, fwd2, inv2, inv1


def _fft_fwd(g, fwd1, tw, fwd2, cb):
    n1 = fwd1.shape[0] // 2
    a = jnp.dot(fwd1, g, preferred_element_type=F32)
    are = jnp.concatenate([a[:n1, c * TM:(c + 1) * TM] for c in range(cb)], axis=0)
    aim = jnp.concatenate([a[n1:, c * TM:(c + 1) * TM] for c in range(cb)], axis=0)
    are = are.reshape(cb, n1, TM)
    aim = aim.reshape(cb, n1, TM)
    twr, twi = tw[0][None], tw[1][None]
    pr = (are * twr - aim * twi).reshape(cb * n1, TM).astype(BF16)
    pi = (are * twi + aim * twr).reshape(cb * n1, TM).astype(BF16)
    return (jnp.dot(pr, fwd2[0], preferred_element_type=F32)
            + jnp.dot(pi, fwd2[1], preferred_element_type=F32))


def _fftconv_kernel(g_ref, x0_ref, k_ref, fb_ref, fwd1_ref, tw_ref, fwd2_ref, inv2_ref, inv1_ref, o_ref):
    cb = g_ref.shape[-1] // TM
    n1 = fwd1_ref.shape[0] // 2
    g = g_ref[0]
    tw = tw_ref[...]
    x = _fft_fwd(g, fwd1_ref[...], tw, fwd2_ref[...], cb)
    kf = k_ref[...]
    xr, xi = x[:, :TM], x[:, TM:]
    kr, ki = kf[:, :TM], kf[:, TM:]
    yr = (xr * kr - xi * ki).astype(BF16)
    yi = (xr * ki + xi * kr).astype(BF16)
    bc = (jnp.dot(yr, inv2_ref[0], preferred_element_type=F32)
          + jnp.dot(yi, inv2_ref[1], preferred_element_type=F32))
    br = bc[:, :TM].reshape(cb, n1, TM)
    bi = bc[:, TM:].reshape(cb, n1, TM)
    twr, twi = tw[0][None], tw[1][None]
    pr = br * twr + bi * twi
    pi = bi * twr - br * twi
    prl = jnp.concatenate([pr[c] for c in range(cb)], axis=1).astype(BF16)
    pil = jnp.concatenate([pi[c] for c in range(cb)], axis=1).astype(BF16)
    y = (jnp.dot(inv1_ref[0], prl, preferred_element_type=F32)
         + jnp.dot(inv1_ref[1], pil, preferred_element_type=F32))
    gf = g.astype(F32)
    o_ref[0] = (x0_ref[0].astype(F32) * (y + fb_ref[...] * gf)).astype(o_ref.dtype)


def _fft_conv(g, x0, kf, fb, consts):
    b, nt, w, _ = g.shape
    n1 = 2 * nt
    cb = FFT_CB
    g2 = g.reshape(b, nt, w * TM)
    x2 = x0.reshape(b, nt, w * TM)
    fwd1, tw, fwd2, inv2, inv1 = consts
    dspec = pl.BlockSpec((1, nt, cb * TM), lambda c, bi: (bi, 0, c))
    full = lambda a: pl.BlockSpec(a.shape, lambda c, bi: (0,) * a.ndim)
    out = pl.pallas_call(
        _fftconv_kernel,
        grid=(w // cb, b),
        in_specs=[dspec, dspec,
                  pl.BlockSpec((cb * n1, 2 * TM), lambda c, bi: (c, 0)),
                  pl.BlockSpec((1, cb * TM), lambda c, bi: (0, c)),
                  full(fwd1), full(tw), full(fwd2), full(inv2), full(inv1)],
        out_specs=dspec,
        out_shape=jax.ShapeDtypeStruct((b, nt, w * TM), BF16),
        compiler_params=_params("arbitrary", "arbitrary"),
        name="long_conv",
    )(g2, x2, kf, fb, fwd1, tw, fwd2, inv2, inv1)
    return out.reshape(b, nt, w, TM)


def _spectrum_kernel(hf_ref, hb_ref, fwd1_ref, tw_ref, fwd2_ref, k_ref):
    cb = hf_ref.shape[-1] // TM
    tw = tw_ref[...]
    xf = _fft_fwd(hf_ref[...].astype(BF16), fwd1_ref[...], tw, fwd2_ref[...], cb)
    xb = _fft_fwd(hb_ref[...].astype(BF16), fwd1_ref[...], tw, fwd2_ref[...], cb)
    k_ref[...] = jnp.concatenate([xf[:, :TM] + xb[:, :TM], xf[:, TM:] - xb[:, TM:]], axis=1)


def _filter_spectrum(hf, hb, consts):
    nt, w, _ = hf.shape
    n1 = 2 * nt
    cb = FFT_CB
    fwd1, tw, fwd2, _, _ = consts
    hspec = pl.BlockSpec((nt, cb * TM), lambda c: (0, c))
    full = lambda a: pl.BlockSpec(a.shape, lambda c: (0,) * a.ndim)
    return pl.pallas_call(
        _spectrum_kernel,
        grid=(w // cb,),
        in_specs=[hspec, hspec, full(fwd1), full(tw), full(fwd2)],
        out_specs=pl.BlockSpec((cb * n1, 2 * TM), lambda c: (c, 0)),
        out_shape=jax.ShapeDtypeStruct((w * n1, 2 * TM), F32),
        compiler_params=_params("arbitrary"),
        name="filter_spectrum",
    )(hf.reshape(nt, w * TM), hb.reshape(nt, w * TM), fwd1, tw, fwd2)


def _filter_kernel(z_ref, t_ref, w1_ref, b1_ref, w2_ref, b2_ref, fr_ref, w3f_ref, w3b_ref, dl_ref,
                   hf_ref, hb_ref):
    nt = hf_ref.shape[0]
    dot = functools.partial(jnp.dot, preferred_element_type=F32, precision=HIGHEST)
    fr = fr_ref[...]
    h = jnp.sin(fr * (dot(w1_ref[...], z_ref[...]) + b1_ref[...]))
    h = jnp.sin(fr * (dot(w2_ref[...], h) + b2_ref[...]))
    decay = jnp.exp(-t_ref[...] * jnp.abs(dl_ref[...]))
    hf = dot(w3f_ref[...], h) * decay
    hb = dot(w3b_ref[...], h) * decay
    norm = (jnp.sum(jnp.abs(hf), axis=1, keepdims=True)
            + jnp.sum(jnp.abs(hb), axis=1, keepdims=True))
    hf = hf / norm
    hb = hb / norm
    lane = lax.broadcasted_iota(I32, hb.shape, 1)
    hb = jnp.where(lane == 0, 0.0, hb)
    for i in range(nt):
        hf_ref[i] = hf[:, i * TM:(i + 1) * TM]
        hb_ref[i] = hb[:, i * TM:(i + 1) * TM]


def _implicit_filters(l, fw1, fb1, fw2, fb2, fw3, ffreq):
    nt = l // TM
    w = HYENA_WIDTH
    t = jnp.linspace(0.0, 1.0, l, dtype=F32)[None, :]
    bands = (FILTER_EMB_DIM - 1) // 2
    wv = 2.0 * math.pi * jnp.arange(l, dtype=F32)[None, :] / l
    f = jnp.linspace(1e-4, bands - 1, bands, dtype=F32)[:, None]
    z = jnp.concatenate([t, jnp.cos(f * wv), -jnp.sin(f * wv)], axis=0)
    min_decay = math.log(DECAY_TARGET) / SLOW_DECAY_PCT
    max_decay = math.log(DECAY_TARGET) / FAST_DECAY_PCT
    deltas = jnp.linspace(min_decay, max_decay, w, dtype=F32)[:, None]
    w3t = fw3.T
    nc = w // FILT_CB
    col = lambda a: a.reshape(-1, 1)
    full = lambda a: pl.BlockSpec(a.shape, lambda c: (0,) * a.ndim)
    args = (z, t, fw1.T, col(fb1), fw2.T, col(fb2), col(ffreq))
    ospec = pl.BlockSpec((nt, FILT_CB, TM), lambda c: (0, c, 0))
    return pl.pallas_call(
        _filter_kernel,
        grid=(nc,),
        in_specs=[full(a) for a in args] + [
            pl.BlockSpec((FILT_CB, FILTER_HIDDEN), lambda c: (c, 0)),
            pl.BlockSpec((FILT_CB, FILTER_HIDDEN), lambda c: (nc + c, 0)),
            pl.BlockSpec((FILT_CB, 1), lambda c: (c, 0))],
        out_specs=[ospec, ospec],
        out_shape=[jax.ShapeDtypeStruct((nt, w, TM), F32)] * 2,
        compiler_params=_params("arbitrary"),
        name="implicit_filter",
    )(*args, w3t, w3t, deltas)


def _dense_dft_consts():
    n = 2 * TM
    two_pi = 2.0 * math.pi
    t = jnp.arange(TM, dtype=I32)
    f = jnp.arange(n, dtype=I32)
    ang = ((t[:, None] * f[None, :]) % n).astype(F32) * (two_pi / n)
    fwd = jnp.concatenate([jnp.cos(ang), -jnp.sin(ang)], axis=1).astype(BF16)
    inv = (jnp.stack([jnp.cos(ang).T, -jnp.sin(ang).T]) * (1.0 / n)).astype(BF16)
    return fwd, inv


def _hyena_tile_kernel(u_ref, w_ref, b_ref, hf_ref, hb_ref, fb_ref, fwd_ref, inv_ref, o_ref):
    w = HYENA_WIDTH
    n = 2 * TM
    x0 = _short_conv_tiles(u_ref.at[:, :, 0:w], w_ref[0], b_ref[0])[0]
    x1 = _short_conv_tiles(u_ref.at[:, :, w:2 * w], w_ref[1], b_ref[1])[0]
    v = _short_conv_tiles(u_ref.at[:, :, 2 * w:3 * w], w_ref[2], b_ref[2])[0]
    g = (v * x1).astype(BF16)
    fwd = fwd_ref[...]
    dot = functools.partial(jnp.dot, preferred_element_type=F32)
    kf = dot(hf_ref[0].astype(BF16), fwd)
    kb = dot(hb_ref[0].astype(BF16), fwd)
    kr = kf[:, :n] + kb[:, :n]
    ki = kf[:, n:] - kb[:, n:]
    x = dot(g, fwd)
    xr, xi = x[:, :n], x[:, n:]
    yr = (xr * kr - xi * ki).astype(BF16)
    yi = (xr * ki + xi * kr).astype(BF16)
    y = dot(yr, inv_ref[0]) + dot(yi, inv_ref[1])
    o_ref[0, 0] = (x0 * (y + fb_ref[...] * g.astype(F32))).astype(o_ref.dtype)


def _hyena_one_tile(u, cw, cb, hf, hb, fbias, consts):
    b = u.shape[0]
    w = HYENA_WIDTH
    fwd, inv = consts
    full = lambda a: pl.BlockSpec(a.shape, lambda bi: (0,) * a.ndim)
    return pl.pallas_call(
        _hyena_tile_kernel,
        grid=(b,),
        in_specs=[pl.BlockSpec((1, 1, 3 * w, TM), lambda bi: (bi, 0, 0, 0)),
                  full(cw), full(cb), full(hf), full(hb), full(fbias), full(fwd), full(inv)],
        out_specs=pl.BlockSpec((1, 1, w, TM), lambda bi: (bi, 0, 0, 0)),
        out_shape=jax.ShapeDtypeStruct((b, 1, w, TM), BF16),
        compiler_params=_params("arbitrary"),
        name="hyena_context",
    )(u, cw, cb, hf, hb, fbias, fwd, inv)


def _top2_sum(a, b, c, d):
    hi1, lo1 = jnp.maximum(a, b), jnp.minimum(a, b)
    hi2, lo2 = jnp.maximum(c, d), jnp.minimum(c, d)
    return jnp.maximum(hi1, hi2) + jnp.maximum(jnp.minimum(hi1, hi2), jnp.maximum(lo1, lo2))


def _outproj_kernel(a_ref, y_ref, x_ref, mod_ref, ga_ref, gy_ref, wo_ref, n2_ref, rw_ref, rb_ref,
                    cnt_ref, tri_ref, xo_ref, h2_ref, ids_ref, rank_ref, wts_ref, cnto_ref, cnt_sc):
    first = jnp.logical_and(pl.program_id(0) == 0, pl.program_id(1) == 0)

    @pl.when(first)
    def _():
        cnt_sc[...] = cnt_ref[...]

    def group_norm(t, gain):
        t = t.astype(F32)
        return t * lax.rsqrt(jnp.mean(t * t, axis=0, keepdims=True) + EPS) * gain

    mix = jnp.concatenate([group_norm(a_ref[0, 0], ga_ref[...]),
                           group_norm(y_ref[0, 0], gy_ref[...])], axis=0).astype(BF16)
    o = lax.dot_general(mix, wo_ref[...], (((0,), (0,)), ((), ())), preferred_element_type=F32)
    xn = x_ref[0] + mod_ref[0, 2:3, :] * o
    xo_ref[0] = xn
    h2 = _norm_mod(xn, n2_ref[...], mod_ref[0, 3:4, :], mod_ref[0, 4:5, :])
    h2_ref[0] = h2

    tm = xn.shape[0]
    logits = lax.dot_general(rw_ref[...], h2, (((1,), (1,)), ((), ())),
                             preferred_element_type=F32, precision=HIGHEST)
    score = jax.nn.sigmoid(logits)
    sel = score + rb_ref[...]
    srow = [sel[e:e + 1] for e in range(N_EXPERTS)]
    prow = [score[e:e + 1] for e in range(N_EXPERTS)]
    epg = EXPERTS_PER_GROUP
    gs = [_top2_sum(*srow[g * epg:(g + 1) * epg]) for g in range(N_EXPERT_GROUPS)]
    bg = jnp.zeros((1, tm), I32)
    best = gs[0]
    for g in range(1, N_EXPERT_GROUPS):
        upd = gs[g] > best
        bg = jnp.where(upd, g, bg)
        best = jnp.where(upd, gs[g], best)

    def pick_group(rows, j):
        out = rows[j]
        for g in range(1, N_EXPERT_GROUPS):
            out = jnp.where(bg == g, rows[g * epg + j], out)
        return out

    cand = [pick_group(srow, j) for j in range(epg)]
    cprob = [pick_group(prow, j) for j in range(epg)]
    i1 = jnp.zeros((1, tm), I32)
    v1, w1 = cand[0], cprob[0]
    for j in range(1, epg):
        upd = cand[j] > v1
        i1 = jnp.where(upd, j, i1)
        v1 = jnp.where(upd, cand[j], v1)
        w1 = jnp.where(upd, cprob[j], w1)
    i2 = jnp.zeros((1, tm), I32)
    v2 = jnp.full((1, tm), -jnp.inf, F32)
    w2 = jnp.zeros((1, tm), F32)
    for j in range(epg):
        upd = jnp.logical_and(i1 != j, cand[j] > v2)
        i2 = jnp.where(upd, j, i2)
        v2 = jnp.where(upd, cand[j], v2)
        w2 = jnp.where(upd, cprob[j], w2)
    den = w1 + w2
    e1 = bg * epg + i1
    e2 = bg * epg + i2
    ids_ref[0, 0] = jnp.concatenate([e1, e2], axis=0)
    wts_ref[0, 0] = jnp.concatenate([w1 / den, w2 / den], axis=0)

    eio = lax.broadcasted_iota(I32, (N_EXPERTS, tm), 0)
    oh1 = jnp.where(eio == e1, 1.0, 0.0)
    oh2 = jnp.where(eio == e2, 1.0, 0.0)
    tri = tri_ref[...]
    cum1 = jnp.dot(oh1.astype(BF16), tri, preferred_element_type=F32)
    cum2 = jnp.dot(oh2.astype(BF16), tri, preferred_element_type=F32)
    tot1 = jnp.sum(oh1, axis=1, keepdims=True)
    tot2 = jnp.sum(oh2, axis=1, keepdims=True)
    base = cnt_sc[...]
    r1 = jnp.sum(oh1 * (base + cum1), axis=0, keepdims=True)
    r2 = jnp.sum(oh2 * (base + tot1 + cum2), axis=0, keepdims=True)
    rank_ref[0, 0] = jnp.concatenate([r1, r2], axis=0).astype(I32)
    new = base + tot1 + tot2
    cnt_sc[...] = new
    cnto_ref[...] = new


def _outproj_route(a, y, x, mods, mod_row, ga, gy, wo, n2g, rwt, rb, cnt, tri):
    b, l, d = x.shape
    nt = l // TM
    if mod_row is None:
        mod_map = lambda bi, i: (bi, 0, 0)
    else:
        mod_map = lambda bi, i: (mod_row, 0, 0)
    full = lambda arr: pl.BlockSpec(arr.shape, lambda bi, i: (0,) * arr.ndim)
    tile = pl.BlockSpec((1, 1, ATTN_WIDTH, TM), lambda bi, i: (bi, i, 0, 0))
    xspec = pl.BlockSpec((1, TM, d), lambda bi, i: (bi, i, 0))
    rspec = pl.BlockSpec((1, 1, 2, TM), lambda bi, i: (bi, i, 0, 0))
    return pl.pallas_call(
        _outproj_kernel,
        grid=(b, nt),
        in_specs=[tile, tile, xspec, pl.BlockSpec((1, N_MOD, d), mod_map),
                  full(ga), full(gy), full(wo), full(n2g), full(rwt), full(rb), full(cnt), full(tri)],
        out_specs=[xspec, xspec, rspec, rspec, rspec, full(cnt)],
        out_shape=[jax.ShapeDtypeStruct((b, l, d), F32),
                   jax.ShapeDtypeStruct((b, l, d), F32),
                   jax.ShapeDtypeStruct((b, nt, 2, TM), I32),
                   jax.ShapeDtypeStruct((b, nt, 2, TM), I32),
                   jax.ShapeDtypeStruct((b, nt, 2, TM), F32),
                   jax.ShapeDtypeStruct(cnt.shape, F32)],
        scratch_shapes=[pltpu.VMEM(cnt.shape, F32)],
        compiler_params=_params("arbitrary", "arbitrary"),
        name="outproj_route",
    )(a, y, x, mods, ga, gy, wo, n2g, rwt, rb, cnt, tri)


def _dispatch_kernel(slot_ref, h_ref, xs_in_ref, xs_ref, sem):
    del xs_in_ref
    tm = h_ref.shape[0]

    def copy(r, s):
        return pltpu.make_async_copy(h_ref.at[pl.ds(r, 1)], xs_ref.at[pl.ds(s, 1)], sem)

    def issue(r, carry):
        copy(r, slot_ref[0, 0, r]).start()
        copy(r, slot_ref[0, 0, tm + r]).start()
        return carry

    lax.fori_loop(0, tm, issue, 0)

    def drain(r, carry):
        copy(0, 0).wait()
        copy(0, 0).wait()
        return carry

    lax.fori_loop(0, tm, drain, 0)


def _dispatch(slots, h2, xs):
    t, d = h2.shape
    return pl.pallas_call(
        _dispatch_kernel,
        grid=(t // TM,),
        in_specs=[pl.BlockSpec((1, 1, 2 * TM), lambda i: (i, 0, 0), memory_space=pltpu.SMEM),
                  pl.BlockSpec((TM, d), lambda i: (i, 0)),
                  pl.BlockSpec(memory_space=pl.ANY)],
        out_specs=pl.BlockSpec(memory_space=pl.ANY),
        out_shape=jax.ShapeDtypeStruct(xs.shape, xs.dtype),
        scratch_shapes=[pltpu.SemaphoreType.DMA(())],
        input_output_aliases={2: 0},
        compiler_params=_params("arbitrary"),
        name="moe_dispatch",
    )(slots, h2, xs)


def _ffn_kernel(te_ref, nt_ref, xs_ref, wg_ref, wu_ref, wd_ref, ys_ref):
    del te_ref
    live = pl.program_id(0) < nt_ref[0]

    @pl.when(live)
    def _():
        x = xs_ref[...].astype(BF16)
        g = jnp.dot(x, wg_ref[0], preferred_element_type=F32)
        u = jnp.dot(x, wu_ref[0], preferred_element_type=F32)
        a = (g * jax.nn.sigmoid(g)) * u
        ys_ref[...] = jnp.dot(a.astype(BF16), wd_ref[0], preferred_element_type=F32)

    @pl.when(jnp.logical_not(live))
    def _():
        ys_ref[...] = jnp.zeros_like(ys_ref)


def _expert_ffn(tile_expert, n_tiles, xs, wg, wu, wd):
    nslot, d = xs.shape
    f = wg.shape[-1]
    ntile = nslot // FFN_TILE
    row = lambda i, te, nt: (jnp.minimum(i, nt[0] - 1), 0)
    wmap = lambda i, te, nt: (te[jnp.minimum(i, nt[0] - 1)], 0, 0)
    return pl.pallas_call(
        _ffn_kernel,
        grid_spec=pltpu.PrefetchScalarGridSpec(
            num_scalar_prefetch=2,
            grid=(ntile,),
            in_specs=[pl.BlockSpec((FFN_TILE, d), row),
                      pl.BlockSpec((1, d, f), wmap),
                      pl.BlockSpec((1, d, f), wmap),
                      pl.BlockSpec((1, f, d), wmap)],
            out_specs=pl.BlockSpec((FFN_TILE, d), lambda i, te, nt: (i, 0))),
        out_shape=jax.ShapeDtypeStruct((nslot, d), F32),
        compiler_params=_params("arbitrary"),
        name="moe_experts",
    )(tile_expert, n_tiles, xs, wg, wu, wd)


def _combine_kernel(slot_ref, w_ref, x_ref, mod_ref, ys_ref, o_ref, buf, sem):
    tm = x_ref.shape[1]

    def copy(k, r, s):
        return pltpu.make_async_copy(ys_ref.at[pl.ds(s, 1)], buf.at[k, pl.ds(r, 1)], sem)

    def issue(r, carry):
        copy(0, r, slot_ref[0, 0, r]).start()
        copy(1, r, slot_ref[0, 0, tm + r]).start()
        return carry

    lax.fori_loop(0, tm, issue, 0)

    def drain(r, carry):
        copy(0, 0, 0).wait()
        copy(1, 0, 0).wait()
        return carry

    lax.fori_loop(0, tm, drain, 0)
    w = w_ref[...]
    y = w[:, 0:1] * buf[0] + w[:, 1:2] * buf[1]
    o_ref[0] = x_ref[0] + mod_ref[0, 5:6, :] * y


def _combine(slots, wts, x, mods, mod_row, ys):
    b, l, d = x.shape
    nt = l // TM
    if mod_row is None:
        mod_map = lambda bi, i: (bi, 0, 0)
    else:
        mod_map = lambda bi, i: (mod_row, 0, 0)
    xspec = pl.BlockSpec((1, TM, d), lambda bi, i: (bi, i, 0))
    return pl.pallas_call(
        _combine_kernel,
        grid=(b, nt),
        in_specs=[pl.BlockSpec((1, 1, 2 * TM), lambda bi, i: (bi * nt + i, 0, 0), memory_space=pltpu.SMEM),
                  pl.BlockSpec((TM, 2), lambda bi, i: (bi * nt + i, 0)),
                  xspec,
                  pl.BlockSpec((1, N_MOD, d), mod_map),
                  pl.BlockSpec(memory_space=pl.ANY)],
        out_specs=xspec,
        out_shape=jax.ShapeDtypeStruct((b, l, d), F32),
        scratch_shapes=[pltpu.VMEM((2, TM, d), F32), pltpu.SemaphoreType.DMA(())],
        compiler_params=_params("arbitrary", "arbitrary"),
        name="moe_combine",
    )(slots, wts, x, mods, ys)


def _moe(streams, wg, wu, wd):
    counts = streams[-1]["counts"].reshape(-1).astype(I32)
    padded = ((counts + FFN_TILE - 1) // FFN_TILE) * FFN_TILE
    ends = jnp.cumsum(padded)
    offs = ends - padded
    total = sum(s["x"].shape[0] * s["x"].shape[1] for s in streams)
    ntile = (2 * total) // FFN_TILE + N_EXPERTS
    nslot = ntile * FFN_TILE
    d = streams[0]["x"].shape[-1]
    tile_start = jnp.arange(ntile, dtype=I32) * FFN_TILE
    tile_expert = jnp.minimum(jnp.sum((tile_start[:, None] >= ends[None, :]).astype(I32), axis=1),
                              N_EXPERTS - 1).astype(I32)
    n_tiles = (ends[-1] // FFN_TILE).astype(I32).reshape(1)

    xs = jnp.zeros((nslot, d), F32)
    for s in streams:
        b, l, _ = s["x"].shape
        slot = offs[s["ids"]] + s["rank"]
        s["slots"] = slot.reshape(b * (l // TM), 1, 2 * TM)
        xs = _dispatch(s["slots"], s["h2"].reshape(b * l, d), xs)
    ys = _expert_ffn(tile_expert, n_tiles, xs, wg, wu, wd)
    outs = []
    for s in streams:
        b, l, _ = s["x"].shape
        wts = s["wts"].transpose(0, 1, 3, 2).reshape(b * l, 2)
        outs.append(_combine(s["slots"], wts, s["x"], s["mods"], s["mod_row"], ys))
    return outs


def _rope_table(l):
    t = jnp.arange(l)
    row = (t // GRID_W).astype(F32)
    col = (t % GRID_W).astype(F32)
    inv_freq = ROPE_THETA ** (-jnp.arange(ROPE_QUARTER, dtype=F32) / ROPE_QUARTER)
    ar = inv_freq[:, None] * row[None, :]
    ac = inv_freq[:, None] * col[None, :]
    return jnp.concatenate([jnp.cos(ar), jnp.sin(ar), jnp.cos(ac), jnp.sin(ac)], axis=0)


def _identity_rope_table(l):
    one = jnp.ones((ROPE_QUARTER, l), F32)
    zero = jnp.zeros((ROPE_QUARTER, l), F32)
    return jnp.concatenate([one, zero, one, zero], axis=0)


def kernel(x, c, ctx, c_ctx, w_mod, b_mod, norm1_g, w_in, q_norm_g, k_norm_g, conv_w, conv_b, filt_w1, filt_b1, filt_w2, filt_b2, filt_w3, filt_freq, filt_bias, attn_out_g, hyena_out_g, w_out, norm2_g, router_w, router_bias, expert_w_gate, expert_w_up, expert_w_down):
    depth = w_mod.shape[0]
    b, l, d = x.shape
    lc = ctx.shape[1]
    assert l % TM == 0 and lc == TM and d == D_MODEL
    nt = l // TM
    w = HYENA_WIDTH

    mods = _modulations(c, c_ctx, w_mod, b_mod)
    rope_lat = _rope_table(l)
    rope_ctx = _identity_rope_table(lc)
    dft = _dft_consts(nt)
    dft_ctx = _dense_dft_consts()
    tri = (jnp.arange(TM)[:, None] < jnp.arange(TM)[None, :]).astype(BF16)
    rwt = router_w.T
    rb = router_bias.reshape(-1, 1)
    zero_cnt = jnp.zeros((N_EXPERTS, 1), F32)
    col = lambda a: a.reshape(-1, 1)

    for li in range(depth):
        last = li == depth - 1
        m = mods[li]
        wt = w_in[li].T.astype(BF16)
        wo = w_out[li].astype(BF16)
        g1n = norm1_g[li].reshape(1, d)
        g2n = norm2_g[li].reshape(1, d)
        qg, kg = col(q_norm_g[li]), col(k_norm_g[li])
        cw = conv_w[li].T.reshape(3, w, SHORT_CONV)
        cb = conv_b[li].reshape(3, w, 1)
        ga, gy = col(attn_out_g[li]), col(hyena_out_g[li])
        fargs = (filt_w1[li], filt_b1[li], filt_w2[li], filt_b2[li], filt_w3[li], filt_freq[li])

        q, k, v, u = _inproj(x, m, None, g1n, wt, qg, kg, rope_lat)
        qc, kc, vc, uc = _inproj(ctx, m, b, g1n, wt, qg, kg, rope_ctx)
        a = _attention(q, jnp.concatenate([kc, k], axis=2), jnp.concatenate([vc, v], axis=2))
        hf, hb = _implicit_filters(l, *fargs)
        kf = _filter_spectrum(hf, hb, dft)
        g, x0 = _short_conv_gate(u, cw, cb)
        fb_lanes = jnp.repeat(filt_bias[li], TM).reshape(1, w * TM)
        y = _fft_conv(g, x0, kf, fb_lanes, dft)
        lat = dict(zip(("x", "h2", "ids", "rank", "wts", "counts"),
                       _outproj_route(a, y, x, m, None, ga, gy, wo, g2n, rwt, rb, zero_cnt, tri)))
        lat.update(mods=m, mod_row=None)
        streams = [lat]

        if not last:
            ac = _attention(qc, kc, vc)
            hfc, hbc = _implicit_filters(lc, *fargs)
            yc = _hyena_one_tile(uc, cw, cb, hfc, hbc, col(filt_bias[li]), dft_ctx)
            cs = dict(zip(("x", "h2", "ids", "rank", "wts", "counts"),
                          _outproj_route(ac, yc, ctx, m, b, ga, gy, wo, g2n, rwt, rb, lat["counts"], tri)))
            cs.update(mods=m, mod_row=b)
            streams.append(cs)

        outs = _moe(streams, expert_w_gate[li].astype(BF16), expert_w_up[li].astype(BF16),
                    expert_w_down[li].astype(BF16))
        x = outs[0]
        if not last:
            ctx = outs[1]
    return x
```

```python
import functools
import math

import jax
import jax.numpy as jnp
from jax import lax
from jax.experimental import pallas as pl
from jax.experimental.pallas import tpu as pltpu

F32 = jnp.float32
BF16 = jnp.bfloat16
I32 = jnp.int32
HIGHEST = lax.Precision.HIGHEST

D_MODEL = 1024
N_MOD = 6
EPS = 1e-6
N_Q_HEADS = 8
N_KV_HEADS = 2
HEAD_DIM = 64
KV_REP = N_Q_HEADS // N_KV_HEADS
ATTN_WIDTH = N_Q_HEADS * HEAD_DIM
KV_WIDTH = N_KV_HEADS * HEAD_DIM
ATTN_SCALE = HEAD_DIM ** -0.5
LOG2_E = math.log2(math.e)
V_ROWS = HEAD_DIM + 16
GRID_W = 64
ROPE_THETA = 10000.0
ROPE_QUARTER = HEAD_DIM // 4
HYENA_WIDTH = D_MODEL - ATTN_WIDTH
SHORT_CONV = 3
FILTER_EMB_DIM = 33
FILTER_HIDDEN = 64
DECAY_TARGET = 1e-2
FAST_DECAY_PCT = 0.3
SLOW_DECAY_PCT = 1.5
N_EXPERTS = 16
N_EXPERT_GROUPS = 4
EXPERTS_PER_GROUP = N_EXPERTS // N_EXPERT_GROUPS
D_FF_EXPERT = 512

TM = 256
ATTN_UNROLL = 4
FFN_TILE = 256
CONV_CB = 128
FFT_CB = 32
FILT_CB = 128
VMEM_LIMIT = 56 * 1024 * 1024


def _params(*sem):
    return pltpu.CompilerParams(dimension_semantics=tuple(sem), vmem_limit_bytes=VMEM_LIMIT)


def _norm_mod(x, g, shift, scale):
    y = x * lax.rsqrt(jnp.mean(x * x, axis=-1, keepdims=True) + EPS)
    return (y * g) * (1 + scale) + shift


def _mod_kernel(c_ref, w_ref, b_ref, o_ref):
    c = c_ref[...]
    s = c * jax.nn.sigmoid(c)
    o_ref[0] = jnp.dot(s, w_ref[0], preferred_element_type=F32, precision=HIGHEST) + b_ref[0]


def _modulations(c, c_ctx, w_mod, b_mod):
    depth, d, nmd = w_mod.shape
    b = c.shape[0]
    rows = -(-(b + 1) // 8) * 8
    c_all = jnp.zeros((rows, d), F32).at[:b].set(c).at[b].set(c_ctx)
    tn = nmd // 4
    out = pl.pallas_call(
        _mod_kernel,
        grid=(depth, nmd // tn),
        in_specs=[pl.BlockSpec((rows, d), lambda l, j: (0, 0)),
                  pl.BlockSpec((1, d, tn), lambda l, j: (l, 0, j)),
                  pl.BlockSpec((1, 1, tn), lambda l, j: (l, 0, j))],
        out_specs=pl.BlockSpec((1, rows, tn), lambda l, j: (l, 0, j)),
        out_shape=jax.ShapeDtypeStruct((depth, rows, nmd), F32),
        compiler_params=_params("arbitrary", "arbitrary"),
        name="modulation",
    )(c_all, w_mod, b_mod.reshape(depth, 1, nmd))
    return out.reshape(depth, rows, N_MOD, d)


def _inproj_kernel(x_ref, mod_ref, g_ref, wt_ref, qg_ref, kg_ref, rope_ref,
                   q_ref, k_ref, v_ref, u_ref):
    x = x_ref[0]
    tm = x.shape[0]
    h = _norm_mod(x, g_ref[...], mod_ref[0, 0:1, :], mod_ref[0, 1:2, :])
    pt = lax.dot_general(wt_ref[...], h.astype(BF16), (((1,), (1,)), ((), ())),
                         preferred_element_type=F32)
    rope = rope_ref[...]
    qd = ROPE_QUARTER
    cr, sr = rope[0:qd][None], rope[qd:2 * qd][None]
    cc, sc = rope[2 * qd:3 * qd][None], rope[3 * qd:4 * qd][None]

    def norm_rope(t, gain, nh):
        t = t.reshape(nh, HEAD_DIM, tm)
        t = t * lax.rsqrt(jnp.mean(t * t, axis=1, keepdims=True) + EPS) * gain[None]
        a, b = t[:, 0:qd], t[:, qd:2 * qd]
        c, d = t[:, 2 * qd:3 * qd], t[:, 3 * qd:4 * qd]
        return jnp.concatenate([a * cr - b * sr, b * cr + a * sr,
                                c * cc - d * sc, d * cc + c * sc], axis=1)

    q = norm_rope(pt[0:ATTN_WIDTH], qg_ref[...], N_Q_HEADS) * (ATTN_SCALE * LOG2_E)
    q_ref[0] = q.astype(BF16)
    k = norm_rope(pt[ATTN_WIDTH:ATTN_WIDTH + KV_WIDTH], kg_ref[...], N_KV_HEADS)
    kt = k.reshape(KV_WIDTH, tm).T
    for g in range(N_KV_HEADS):
        k_ref[0, g, 0] = kt[:, g * HEAD_DIM:(g + 1) * HEAD_DIM].astype(BF16)
    v = pt[ATTN_WIDTH + KV_WIDTH:ATTN_WIDTH + 2 * KV_WIDTH]
    v_ref[0, :, 0, 0:HEAD_DIM] = v.reshape(N_KV_HEADS, HEAD_DIM, tm).astype(BF16)
    pad_row = lax.broadcasted_iota(I32, (N_KV_HEADS, V_ROWS - HEAD_DIM, tm), 1)
    v_ref[0, :, 0, HEAD_DIM:V_ROWS] = jnp.where(pad_row == 0, 1.0, 0.0).astype(BF16)
    u_ref[0, 0] = pt[ATTN_WIDTH + 2 * KV_WIDTH:].astype(BF16)


def _inproj(x, mods, mod_row, g, wt, qg, kg, rope):
    b, l, d = x.shape
    nt = l // TM
    p = wt.shape[0]
    uw = p - ATTN_WIDTH - 2 * KV_WIDTH
    if mod_row is None:
        mod_map = lambda bi, i: (bi, 0, 0)
    else:
        mod_map = lambda bi, i: (mod_row, 0, 0)
    return pl.pallas_call(
        _inproj_kernel,
        grid=(b, nt),
        in_specs=[pl.BlockSpec((1, TM, d), lambda bi, i: (bi, i, 0)),
                  pl.BlockSpec((1, N_MOD, d), mod_map),
                  pl.BlockSpec((1, d), lambda bi, i: (0, 0)),
                  pl.BlockSpec((p, d), lambda bi, i: (0, 0)),
                  pl.BlockSpec((HEAD_DIM, 1), lambda bi, i: (0, 0)),
                  pl.BlockSpec((HEAD_DIM, 1), lambda bi, i: (0, 0)),
                  pl.BlockSpec((HEAD_DIM, TM), lambda bi, i: (0, i))],
        out_specs=[pl.BlockSpec((1, N_Q_HEADS, HEAD_DIM, TM), lambda bi, i: (bi, 0, 0, i)),
                   pl.BlockSpec((1, N_KV_HEADS, 1, TM, HEAD_DIM), lambda bi, i: (bi, 0, i, 0, 0)),
                   pl.BlockSpec((1, N_KV_HEADS, 1, V_ROWS, TM), lambda bi, i: (bi, 0, i, 0, 0)),
                   pl.BlockSpec((1, 1, uw, TM), lambda bi, i: (bi, i, 0, 0))],
        out_shape=[jax.ShapeDtypeStruct((b, N_Q_HEADS, HEAD_DIM, l), BF16),
                   jax.ShapeDtypeStruct((b, N_KV_HEADS, nt, TM, HEAD_DIM), BF16),
                   jax.ShapeDtypeStruct((b, N_KV_HEADS, nt, V_ROWS, TM), BF16),
                   jax.ShapeDtypeStruct((b, nt, uw, TM), BF16)],
        compiler_params=_params("arbitrary", "arbitrary"),
        name="inproj",
    )(x, mods, g, wt, qg, kg, rope)


def _attn_kernel(q_ref, k_ref, v_ref, o_ref, acc_ref, m_ref):
    nk = k_ref.shape[2]
    tq = q_ref.shape[-1]
    q = jnp.concatenate([q_ref[0, r] for r in range(KV_REP)], axis=1)
    acc_ref[...] = jnp.zeros_like(acc_ref)
    m_ref[...] = jnp.full_like(m_ref, -jnp.inf)

    def chunk(j):
        s = jnp.dot(k_ref[0, 0, j], q, preferred_element_type=F32)
        m = m_ref[...]
        m_new = jnp.maximum(m, jnp.max(s, axis=0, keepdims=True))
        alpha = jnp.exp2(m - m_new)
        p = jnp.exp2(s - m_new).astype(BF16)
        acc_ref[...] = alpha * acc_ref[...] + jnp.dot(v_ref[0, 0, j], p, preferred_element_type=F32)
        m_ref[...] = m_new

    def group(i, carry):
        for t in range(ATTN_UNROLL):
            chunk(ATTN_UNROLL * i + t)
        return carry

    lax.fori_loop(0, nk // ATTN_UNROLL, group, 0)
    for j in range(nk - nk % ATTN_UNROLL, nk):
        chunk(j)
    acc = acc_ref[...]
    out = acc[0:HEAD_DIM] / acc[HEAD_DIM:HEAD_DIM + 1]
    for r in range(KV_REP):
        o_ref[0, 0, r * HEAD_DIM:(r + 1) * HEAD_DIM, :] = out[:, r * tq:(r + 1) * tq].astype(o_ref.dtype)


def _attention(q, k, v):
    b, _, _, l = q.shape
    nk = k.shape[2]
    nt = l // TM
    return pl.pallas_call(
        _attn_kernel,
        grid=(b, N_KV_HEADS, nt),
        in_specs=[pl.BlockSpec((1, KV_REP, HEAD_DIM, TM), lambda bi, g, i: (bi, g, 0, i)),
                  pl.BlockSpec((1, 1, nk, TM, HEAD_DIM), lambda bi, g, i: (bi, g, 0, 0, 0)),
                  pl.BlockSpec((1, 1, nk, V_ROWS, TM), lambda bi, g, i: (bi, g, 0, 0, 0))],
        out_specs=pl.BlockSpec((1, 1, KV_REP * HEAD_DIM, TM), lambda bi, g, i: (bi, i, g, 0)),
        out_shape=jax.ShapeDtypeStruct((b, nt, ATTN_WIDTH, TM), BF16),
        scratch_shapes=[pltpu.VMEM((V_ROWS, KV_REP * TM), F32), pltpu.VMEM((1, KV_REP * TM), F32)],
        compiler_params=_params("arbitrary", "arbitrary", "arbitrary"),
        name="attention",
    )(q, k, v)


def _short_conv_tiles(u_ref, w, bias):
    nt, c, tm = u_ref.shape[1:]
    lane = lax.broadcasted_iota(I32, (c, tm), 1)
    tiles = [u_ref[0, i].astype(F32) for i in range(nt)]
    prev = [pltpu.roll(t, 1, 1) for t in tiles]
    nxt = [pltpu.roll(t, tm - 1, 1) for t in tiles]
    zero = jnp.zeros((c, tm), F32)
    out = []
    for i in range(nt):
        up = jnp.where(lane == 0, prev[i - 1] if i > 0 else zero, prev[i])
        un = jnp.where(lane == tm - 1, nxt[i + 1] if i < nt - 1 else zero, nxt[i])
        out.append(bias + up * w[:, 0:1] + tiles[i] * w[:, 1:2] + un * w[:, 2:3])
    return out


def _sconv_kernel(u0_ref, u1_ref, u2_ref, w_ref, b_ref, g_ref, x0_ref):
    x0 = _short_conv_tiles(u0_ref, w_ref[0], b_ref[0])
    x1 = _short_conv_tiles(u1_ref, w_ref[1], b_ref[1])
    v = _short_conv_tiles(u2_ref, w_ref[2], b_ref[2])
    for i in range(len(x0)):
        g_ref[0, i] = (v[i] * x1[i]).astype(g_ref.dtype)
        x0_ref[0, i] = x0[i].astype(x0_ref.dtype)


def _short_conv_gate(u, cw, cb):
    b, nt, _, _ = u.shape
    w = HYENA_WIDTH
    nc = w // CONV_CB
    uspec = lambda grp: pl.BlockSpec((1, nt, CONV_CB, TM), lambda bi, c: (bi, 0, grp * nc + c, 0))
    ospec = pl.BlockSpec((1, nt, CONV_CB, TM), lambda bi, c: (bi, 0, c, 0))
    return pl.pallas_call(
        _sconv_kernel,
        grid=(b, nc),
        in_specs=[uspec(0), uspec(1), uspec(2),
                  pl.BlockSpec((3, CONV_CB, SHORT_CONV), lambda bi, c: (0, c, 0)),
                  pl.BlockSpec((3, CONV_CB, 1), lambda bi, c: (0, c, 0))],
        out_specs=[ospec, ospec],
        out_shape=[jax.ShapeDtypeStruct((b, nt, w, TM), BF16)] * 2,
        compiler_params=_params("arbitrary", "arbitrary"),
        name="short_conv",
    )(u, u, u, cw, cb)


def _dft_consts(nt):
    n1 = 2 * nt
    n = n1 * TM
    two_pi = 2.0 * math.pi

    def cs(prod, mod):
        ang = (prod % mod).astype(F32) * (two_pi / mod)
        return jnp.cos(ang), jnp.sin(ang)

    f1 = jnp.arange(n1, dtype=I32)
    t1 = jnp.arange(nt, dtype=I32)
    t2 = jnp.arange(TM, dtype=I32)
    c, s = cs(f1[:, None] * t1[None, :], n1)
    fwd1 = jnp.concatenate([c, -s], axis=0).astype(BF16)
    c, s = cs(f1[:, None] * t2[None, :], n)
    tw = jnp.stack([c, -s])
    c, s = cs(t2[:, None] * t2[None, :], TM)
    fwd2 = jnp.stack([jnp.concatenate([c, -s], axis=1),
                      jnp.concatenate([s, c], axis=1)]).astype(BF16)
    inv2 = jnp.stack([jnp.concatenate([c, s], axis=1),
                      jnp.concatenate([-s, c], axis=1)]).astype(BF16)
    c, s = cs(t1[:, None] * f1[None, :], n1)
    inv1 = (jnp.stack([c, -s]) * (1.0 / n)).astype(BF16)
    return fwd1, tw, fwd2, inv2, inv1


def _fft_fwd(g, fwd1, tw, fwd2, cb):
    n1 = fwd1.shape[0] // 2
    a = jnp.dot(fwd1, g, preferred_element_type=F32)
    are = jnp.concatenate([a[:n1, c * TM:(c + 1) * TM] for c in range(cb)], axis=0)
    aim = jnp.concatenate([a[n1:, c * TM:(c + 1) * TM] for c in range(cb)], axis=0)
    are = are.reshape(cb, n1, TM)
    aim = aim.reshape(cb, n1, TM)
    twr, twi = tw[0][None], tw[1][None]
    pr = (are * twr - aim * twi).reshape(cb * n1, TM).astype(BF16)
    pi = (are * twi + aim * twr).reshape(cb * n1, TM).astype(BF16)
    return (jnp.dot(pr, fwd2[0], preferred_element_type=F32)
            + jnp.dot(pi, fwd2[1], preferred_element_type=F32))


def _fftconv_kernel(g_ref, x0_ref, k_ref, fb_ref, fwd1_ref, tw_ref, fwd2_ref, inv2_ref, inv1_ref, o_ref):
    cb = g_ref.shape[-1] // TM
    n1 = fwd1_ref.shape[0] // 2
    g = g_ref[0]
    tw = tw_ref[...]
    x = _fft_fwd(g, fwd1_ref[...], tw, fwd2_ref[...], cb)
    kf = k_ref[...]
    xr, xi = x[:, :TM], x[:, TM:]
    kr, ki = kf[:, :TM], kf[:, TM:]
    yr = (xr * kr - xi * ki).astype(BF16)
    yi = (xr * ki + xi * kr).astype(BF16)
    bc = (jnp.dot(yr, inv2_ref[0], preferred_element_type=F32)
          + jnp.dot(yi, inv2_ref[1], preferred_element_type=F32))
    br = bc[:, :TM].reshape(cb, n1, TM)
    bi = bc[:, TM:].reshape(cb, n1, TM)
    twr, twi = tw[0][None], tw[1][None]
    pr = br * twr + bi * twi
    pi = bi * twr - br * twi
    prl = jnp.concatenate([pr[c] for c in range(cb)], axis=1).astype(BF16)
    pil = jnp.concatenate([pi[c] for c in range(cb)], axis=1).astype(BF16)
    y = (jnp.dot(inv1_ref[0], prl, preferred_element_type=F32)
         + jnp.dot(inv1_ref[1], pil, preferred_element_type=F32))
    gf = g.astype(F32)
    o_ref[0] = (x0_ref[0].astype(F32) * (y + fb_ref[...] * gf)).astype(o_ref.dtype)


def _fft_conv(g, x0, kf, fb, consts):
    b, nt, w, _ = g.shape
    n1 = 2 * nt
    cb = FFT_CB
    g2 = g.reshape(b, nt, w * TM)
    x2 = x0.reshape(b, nt, w * TM)
    fwd1, tw, fwd2, inv2, inv1 = consts
    dspec = pl.BlockSpec((1, nt, cb * TM), lambda c, bi: (bi, 0, c))
    full = lambda a: pl.BlockSpec(a.shape, lambda c, bi: (0,) * a.ndim)
    out = pl.pallas_call(
        _fftconv_kernel,
        grid=(w // cb, b),
        in_specs=[dspec, dspec,
                  pl.BlockSpec((cb * n1, 2 * TM), lambda c, bi: (c, 0)),
                  pl.BlockSpec((1, cb * TM), lambda c, bi: (0, c)),
                  full(fwd1), full(tw), full(fwd2), full(inv2), full(inv1)],
        out_specs=dspec,
        out_shape=jax.ShapeDtypeStruct((b, nt, w * TM), BF16),
        compiler_params=_params("arbitrary", "arbitrary"),
        name="long_conv",
    )(g2, x2, kf, fb, fwd1, tw, fwd2, inv2, inv1)
    return out.reshape(b, nt, w, TM)


def _spectrum_kernel(hf_ref, hb_ref, fwd1_ref, tw_ref, fwd2_ref, k_ref):
    cb = hf_ref.shape[-1] // TM
    tw = tw_ref[...]
    xf = _fft_fwd(hf_ref[...].astype(BF16), fwd1_ref[...], tw, fwd2_ref[...], cb)
    xb = _fft_fwd(hb_ref[...].astype(BF16), fwd1_ref[...], tw, fwd2_ref[...], cb)
    k_ref[...] = jnp.concatenate([xf[:, :TM] + xb[:, :TM], xf[:, TM:] - xb[:, TM:]], axis=1)


def _filter_spectrum(hf, hb, consts):
    nt, w, _ = hf.shape
    n1 = 2 * nt
    cb = FFT_CB
    fwd1, tw, fwd2, _, _ = consts
    hspec = pl.BlockSpec((nt, cb * TM), lambda c: (0, c))
    full = lambda a: pl.BlockSpec(a.shape, lambda c: (0,) * a.ndim)
    return pl.pallas_call(
        _spectrum_kernel,
        grid=(w // cb,),
        in_specs=[hspec, hspec, full(fwd1), full(tw), full(fwd2)],
        out_specs=pl.BlockSpec((cb * n1, 2 * TM), lambda c: (c, 0)),
        out_shape=jax.ShapeDtypeStruct((w * n1, 2 * TM), F32),
        compiler_params=_params("arbitrary"),
        name="filter_spectrum",
    )(hf.reshape(nt, w * TM), hb.reshape(nt, w * TM), fwd1, tw, fwd2)


def _filter_kernel(z_ref, t_ref, w1_ref, b1_ref, w2_ref, b2_ref, fr_ref, w3f_ref, w3b_ref, dl_ref,
                   hf_ref, hb_ref):
    nt = hf_ref.shape[0]
    dot = functools.partial(jnp.dot, preferred_element_type=F32, precision=HIGHEST)
    fr = fr_ref[...]
    h = jnp.sin(fr * (dot(w1_ref[...], z_ref[...]) + b1_ref[...]))
    h = jnp.sin(fr * (dot(w2_ref[...], h) + b2_ref[...]))
    decay = jnp.exp(-t_ref[...] * jnp.abs(dl_ref[...]))
    hf = dot(w3f_ref[...], h) * decay
    hb = dot(w3b_ref[...], h) * decay
    norm = (jnp.sum(jnp.abs(hf), axis=1, keepdims=True)
            + jnp.sum(jnp.abs(hb), axis=1, keepdims=True))
    hf = hf / norm
    hb = hb / norm
    lane = lax.broadcasted_iota(I32, hb.shape, 1)
    hb = jnp.where(lane == 0, 0.0, hb)
    for i in range(nt):
        hf_ref[i] = hf[:, i * TM:(i + 1) * TM]
        hb_ref[i] = hb[:, i * TM:(i + 1) * TM]


def _implicit_filters(l, fw1, fb1, fw2, fb2, fw3, ffreq):
    nt = l // TM
    w = HYENA_WIDTH
    t = jnp.linspace(0.0, 1.0, l, dtype=F32)[None, :]
    bands = (FILTER_EMB_DIM - 1) // 2
    wv = 2.0 * math.pi * jnp.arange(l, dtype=F32)[None, :] / l
    f = jnp.linspace(1e-4, bands - 1, bands, dtype=F32)[:, None]
    z = jnp.concatenate([t, jnp.cos(f * wv), -jnp.sin(f * wv)], axis=0)
    min_decay = math.log(DECAY_TARGET) / SLOW_DECAY_PCT
    max_decay = math.log(DECAY_TARGET) / FAST_DECAY_PCT
    deltas = jnp.linspace(min_decay, max_decay, w, dtype=F32)[:, None]
    w3t = fw3.T
    nc = w // FILT_CB
    col = lambda a: a.reshape(-1, 1)
    full = lambda a: pl.BlockSpec(a.shape, lambda c: (0,) * a.ndim)
    args = (z, t, fw1.T, col(fb1), fw2.T, col(fb2), col(ffreq))
    ospec = pl.BlockSpec((nt, FILT_CB, TM), lambda c: (0, c, 0))
    return pl.pallas_call(
        _filter_kernel,
        grid=(nc,),
        in_specs=[full(a) for a in args] + [
            pl.BlockSpec((FILT_CB, FILTER_HIDDEN), lambda c: (c, 0)),
            pl.BlockSpec((FILT_CB, FILTER_HIDDEN), lambda c: (nc + c, 0)),
            pl.BlockSpec((FILT_CB, 1), lambda c: (c, 0))],
        out_specs=[ospec, ospec],
        out_shape=[jax.ShapeDtypeStruct((nt, w, TM), F32)] * 2,
        compiler_params=_params("arbitrary"),
        name="implicit_filter",
    )(*args, w3t, w3t, deltas)


def _dense_dft_consts():
    n = 2 * TM
    two_pi = 2.0 * math.pi
    t = jnp.arange(TM, dtype=I32)
    f = jnp.arange(n, dtype=I32)
    ang = ((t[:, None] * f[None, :]) % n).astype(F32) * (two_pi / n)
    fwd = jnp.concatenate([jnp.cos(ang), -jnp.sin(ang)], axis=1).astype(BF16)
    inv = (jnp.stack([jnp.cos(ang).T, -jnp.sin(ang).T]) * (1.0 / n)).astype(BF16)
    return fwd, inv


def _hyena_tile_kernel(u_ref, w_ref, b_ref, hf_ref, hb_ref, fb_ref, fwd_ref, inv_ref, o_ref):
    w = HYENA_WIDTH
    n = 2 * TM
    x0 = _short_conv_tiles(u_ref.at[:, :, 0:w], w_ref[0], b_ref[0])[0]
    x1 = _short_conv_tiles(u_ref.at[:, :, w:2 * w], w_ref[1], b_ref[1])[0]
    v = _short_conv_tiles(u_ref.at[:, :, 2 * w:3 * w], w_ref[2], b_ref[2])[0]
    g = (v * x1).astype(BF16)
    fwd = fwd_ref[...]
    dot = functools.partial(jnp.dot, preferred_element_type=F32)
    kf = dot(hf_ref[0].astype(BF16), fwd)
    kb = dot(hb_ref[0].astype(BF16), fwd)
    kr = kf[:, :n] + kb[:, :n]
    ki = kf[:, n:] - kb[:, n:]
    x = dot(g, fwd)
    xr, xi = x[:, :n], x[:, n:]
    yr = (xr * kr - xi * ki).astype(BF16)
    yi = (xr * ki + xi * kr).astype(BF16)
    y = dot(yr, inv_ref[0]) + dot(yi, inv_ref[1])
    o_ref[0, 0] = (x0 * (y + fb_ref[...] * g.astype(F32))).astype(o_ref.dtype)


def _hyena_one_tile(u, cw, cb, hf, hb, fbias, consts):
    b = u.shape[0]
    w = HYENA_WIDTH
    fwd, inv = consts
    full = lambda a: pl.BlockSpec(a.shape, lambda bi: (0,) * a.ndim)
    return pl.pallas_call(
        _hyena_tile_kernel,
        grid=(b,),
        in_specs=[pl.BlockSpec((1, 1, 3 * w, TM), lambda bi: (bi, 0, 0, 0)),
                  full(cw), full(cb), full(hf), full(hb), full(fbias), full(fwd), full(inv)],
        out_specs=pl.BlockSpec((1, 1, w, TM), lambda bi: (bi, 0, 0, 0)),
        out_shape=jax.ShapeDtypeStruct((b, 1, w, TM), BF16),
        compiler_params=_params("arbitrary"),
        name="hyena_context",
    )(u, cw, cb, hf, hb, fbias, fwd, inv)


def _top2_sum(a, b, c, d):
    hi1, lo1 = jnp.maximum(a, b), jnp.minimum(a, b)
    hi2, lo2 = jnp.maximum(c, d), jnp.minimum(c, d)
    return jnp.maximum(hi1, hi2) + jnp.maximum(jnp.minimum(hi1, hi2), jnp.maximum(lo1, lo2))


def _outproj_kernel(a_ref, y_ref, x_ref, mod_ref, ga_ref, gy_ref, wo_ref, n2_ref, rw_ref, rb_ref,
                    cnt_ref, tri_ref, xo_ref, h2_ref, ids_ref, rank_ref, wts_ref, cnto_ref, cnt_sc):
    first = jnp.logical_and(pl.program_id(0) == 0, pl.program_id(1) == 0)

    @pl.when(first)
    def _():
        cnt_sc[...] = cnt_ref[...]

    def group_norm(t, gain):
        t = t.astype(F32)
        return t * lax.rsqrt(jnp.mean(t * t, axis=0, keepdims=True) + EPS) * gain

    mix = jnp.concatenate([group_norm(a_ref[0, 0], ga_ref[...]),
                           group_norm(y_ref[0, 0], gy_ref[...])], axis=0).astype(BF16)
    o = lax.dot_general(mix, wo_ref[...], (((0,), (0,)), ((), ())), preferred_element_type=F32)
    xn = x_ref[0] + mod_ref[0, 2:3, :] * o
    xo_ref[0] = xn
    h2 = _norm_mod(xn, n2_ref[...], mod_ref[0, 3:4, :], mod_ref[0, 4:5, :])
    h2_ref[0] = h2

    tm = xn.shape[0]
    logits = lax.dot_general(rw_ref[...], h2, (((1,), (1,)), ((), ())),
                             preferred_element_type=F32, precision=HIGHEST)
    score = jax.nn.sigmoid(logits)
    sel = score + rb_ref[...]
    srow = [sel[e:e + 1] for e in range(N_EXPERTS)]
    prow = [score[e:e + 1] for e in range(N_EXPERTS)]
    epg = EXPERTS_PER_GROUP
    gs = [_top2_sum(*srow[g * epg:(g + 1) * epg]) for g in range(N_EXPERT_GROUPS)]
    bg = jnp.zeros((1, tm), I32)
    best = gs[0]
    for g in range(1, N_EXPERT_GROUPS):
        upd = gs[g] > best
        bg = jnp.where(upd, g, bg)
        best = jnp.where(upd, gs[g], best)

    def pick_group(rows, j):
        out = rows[j]
        for g in range(1, N_EXPERT_GROUPS):
            out = jnp.where(bg == g, rows[g * epg + j], out)
        return out

    cand = [pick_group(srow, j) for j in range(epg)]
    cprob = [pick_group(prow, j) for j in range(epg)]
    i1 = jnp.zeros((1, tm), I32)
    v1, w1 = cand[0], cprob[0]
    for j in range(1, epg):
        upd = cand[j] > v1
        i1 = jnp.where(upd, j, i1)
        v1 = jnp.where(upd, cand[j], v1)
        w1 = jnp.where(upd, cprob[j], w1)
    i2 = jnp.zeros((1, tm), I32)
    v2 = jnp.full((1, tm), -jnp.inf, F32)
    w2 = jnp.zeros((1, tm), F32)
    for j in range(epg):
        upd = jnp.logical_and(i1 != j, cand[j] > v2)
        i2 = jnp.where(upd, j, i2)
        v2 = jnp.where(upd, cand[j], v2)
        w2 = jnp.where(upd, cprob[j], w2)
    den = w1 + w2
    e1 = bg * epg + i1
    e2 = bg * epg + i2
    ids_ref[0, 0] = jnp.concatenate([e1, e2], axis=0)
    wts_ref[0, 0] = jnp.concatenate([w1 / den, w2 / den], axis=0)

    eio = lax.broadcasted_iota(I32, (N_EXPERTS, tm), 0)
    oh1 = jnp.where(eio == e1, 1.0, 0.0)
    oh2 = jnp.where(eio == e2, 1.0, 0.0)
    tri = tri_ref[...]
    cum1 = jnp.dot(oh1.astype(BF16), tri, preferred_element_type=F32)
    cum2 = jnp.dot(oh2.astype(BF16), tri, preferred_element_type=F32)
    tot1 = jnp.sum(oh1, axis=1, keepdims=True)
    tot2 = jnp.sum(oh2, axis=1, keepdims=True)
    base = cnt_sc[...]
    r1 = jnp.sum(oh1 * (base + cum1), axis=0, keepdims=True)
    r2 = jnp.sum(oh2 * (base + tot1 + cum2), axis=0, keepdims=True)
    rank_ref[0, 0] = jnp.concatenate([r1, r2], axis=0).astype(I32)
    new = base + tot1 + tot2
    cnt_sc[...] = new
    cnto_ref[...] = new


def _outproj_route(a, y, x, mods, mod_row, ga, gy, wo, n2g, rwt, rb, cnt, tri):
    b, l, d = x.shape
    nt = l // TM
    if mod_row is None:
        mod_map = lambda bi, i: (bi, 0, 0)
    else:
        mod_map = lambda bi, i: (mod_row, 0, 0)
    full = lambda arr: pl.BlockSpec(arr.shape, lambda bi, i: (0,) * arr.ndim)
    tile = pl.BlockSpec((1, 1, ATTN_WIDTH, TM), lambda bi, i: (bi, i, 0, 0))
    xspec = pl.BlockSpec((1, TM, d), lambda bi, i: (bi, i, 0))
    rspec = pl.BlockSpec((1, 1, 2, TM), lambda bi, i: (bi, i, 0, 0))
    return pl.pallas_call(
        _outproj_kernel,
        grid=(b, nt),
        in_specs=[tile, tile, xspec, pl.BlockSpec((1, N_MOD, d), mod_map),
                  full(ga), full(gy), full(wo), full(n2g), full(rwt), full(rb), full(cnt), full(tri)],
        out_specs=[xspec, xspec, rspec, rspec, rspec, full(cnt)],
        out_shape=[jax.ShapeDtypeStruct((b, l, d), F32),
                   jax.ShapeDtypeStruct((b, l, d), F32),
                   jax.ShapeDtypeStruct((b, nt, 2, TM), I32),
                   jax.ShapeDtypeStruct((b, nt, 2, TM), I32),
                   jax.ShapeDtypeStruct((b, nt, 2, TM), F32),
                   jax.ShapeDtypeStruct(cnt.shape, F32)],
        scratch_shapes=[pltpu.VMEM(cnt.shape, F32)],
        compiler_params=_params("arbitrary", "arbitrary"),
        name="outproj_route",
    )(a, y, x, mods, ga, gy, wo, n2g, rwt, rb, cnt, tri)


def _dispatch_kernel(slot_ref, h_ref, xs_in_ref, xs_ref, sem):
    del xs_in_ref
    tm = h_ref.shape[0]

    def copy(r, s):
        return pltpu.make_async_copy(h_ref.at[pl.ds(r, 1)], xs_ref.at[pl.ds(s, 1)], sem)

    def issue(r, carry):
        copy(r, slot_ref[0, 0, r]).start(priority=0)
        copy(r, slot_ref[0, 0, tm + r]).start(priority=1)
        return carry

    lax.fori_loop(0, tm, issue, 0)

    def drain(r, carry):
        copy(0, 0).wait()
        copy(0, 0).wait()
        return carry

    lax.fori_loop(0, tm, drain, 0)


def _dispatch(slots, h2, xs):
    t, d = h2.shape
    return pl.pallas_call(
        _dispatch_kernel,
        grid=(t // TM,),
        in_specs=[pl.BlockSpec((1, 1, 2 * TM), lambda i: (i, 0, 0), memory_space=pltpu.SMEM),
                  pl.BlockSpec((TM, d), lambda i: (i, 0)),
                  pl.BlockSpec(memory_space=pl.ANY)],
        out_specs=pl.BlockSpec(memory_space=pl.ANY),
        out_shape=jax.ShapeDtypeStruct(xs.shape, xs.dtype),
        scratch_shapes=[pltpu.SemaphoreType.DMA(())],
        input_output_aliases={2: 0},
        compiler_params=_params("arbitrary"),
        name="moe_dispatch",
    )(slots, h2, xs)


def _ffn_kernel(te_ref, nt_ref, xs_ref, wg_ref, wu_ref, wd_ref, ys_ref):
    del te_ref
    live = pl.program_id(0) < nt_ref[0]

    @pl.when(live)
    def _():
        x = xs_ref[...].astype(BF16)
        g = jnp.dot(x, wg_ref[0], preferred_element_type=F32)
        u = jnp.dot(x, wu_ref[0], preferred_element_type=F32)
        a = (g * jax.nn.sigmoid(g)) * u
        ys_ref[...] = jnp.dot(a.astype(BF16), wd_ref[0], preferred_element_type=F32)

    @pl.when(jnp.logical_not(live))
    def _():
        ys_ref[...] = jnp.zeros_like(ys_ref)


def _expert_ffn(tile_expert, n_tiles, xs, wg, wu, wd):
    nslot, d = xs.shape
    f = wg.shape[-1]
    ntile = nslot // FFN_TILE
    row = lambda i, te, nt: (jnp.minimum(i, nt[0] - 1), 0)
    wmap = lambda i, te, nt: (te[jnp.minimum(i, nt[0] - 1)], 0, 0)
    return pl.pallas_call(
        _ffn_kernel,
        grid_spec=pltpu.PrefetchScalarGridSpec(
            num_scalar_prefetch=2,
            grid=(ntile,),
            in_specs=[pl.BlockSpec((FFN_TILE, d), row),
                      pl.BlockSpec((1, d, f), wmap),
                      pl.BlockSpec((1, d, f), wmap),
                      pl.BlockSpec((1, f, d), wmap)],
            out_specs=pl.BlockSpec((FFN_TILE, d), lambda i, te, nt: (i, 0))),
        out_shape=jax.ShapeDtypeStruct((nslot, d), F32),
        compiler_params=_params("arbitrary"),
        name="moe_experts",
    )(tile_expert, n_tiles, xs, wg, wu, wd)


def _combine_kernel(slot_ref, w_ref, x_ref, mod_ref, ys_ref, o_ref, buf, sem):
    tm = x_ref.shape[1]

    def copy(k, r, s):
        return pltpu.make_async_copy(ys_ref.at[pl.ds(s, 1)], buf.at[k, pl.ds(r, 1)], sem)

    def issue(r, carry):
        copy(0, r, slot_ref[0, 0, r]).start(priority=0)
        copy(1, r, slot_ref[0, 0, tm + r]).start(priority=1)
        return carry

    lax.fori_loop(0, tm, issue, 0)

    def drain(r, carry):
        copy(0, 0, 0).wait()
        copy(1, 0, 0).wait()
        return carry

    lax.fori_loop(0, tm, drain, 0)
    w = w_ref[...]
    y = w[:, 0:1] * buf[0] + w[:, 1:2] * buf[1]
    o_ref[0] = x_ref[0] + mod_ref[0, 5:6, :] * y


def _combine(slots, wts, x, mods, mod_row, ys):
    b, l, d = x.shape
    nt = l // TM
    if mod_row is None:
        mod_map = lambda bi, i: (bi, 0, 0)
    else:
        mod_map = lambda bi, i: (mod_row, 0, 0)
    xspec = pl.BlockSpec((1, TM, d), lambda bi, i: (bi, i, 0))
    return pl.pallas_call(
        _combine_kernel,
        grid=(b, nt),
        in_specs=[pl.BlockSpec((1, 1, 2 * TM), lambda bi, i: (bi * nt + i, 0, 0), memory_space=pltpu.SMEM),
                  pl.BlockSpec((TM, 2), lambda bi, i: (bi * nt + i, 0)),
                  xspec,
                  pl.BlockSpec((1, N_MOD, d), mod_map),
                  pl.BlockSpec(memory_space=pl.ANY)],
        out_specs=xspec,
        out_shape=jax.ShapeDtypeStruct((b, l, d), F32),
        scratch_shapes=[pltpu.VMEM((2, TM, d), F32), pltpu.SemaphoreType.DMA(())],
        compiler_params=_params("arbitrary", "arbitrary"),
        name="moe_combine",
    )(slots, wts, x, mods, ys)


def _moe(streams, wg, wu, wd):
    counts = streams[-1]["counts"].reshape(-1).astype(I32)
    padded = ((counts + FFN_TILE - 1) // FFN_TILE) * FFN_TILE
    ends = jnp.cumsum(padded)
    offs = ends - padded
    total = sum(s["x"].shape[0] * s["x"].shape[1] for s in streams)
    ntile = (2 * total) // FFN_TILE + N_EXPERTS
    nslot = ntile * FFN_TILE
    d = streams[0]["x"].shape[-1]
    tile_start = jnp.arange(ntile, dtype=I32) * FFN_TILE
    tile_expert = jnp.minimum(jnp.sum((tile_start[:, None] >= ends[None, :]).astype(I32), axis=1),
                              N_EXPERTS - 1).astype(I32)
    n_tiles = (ends[-1] // FFN_TILE).astype(I32).reshape(1)

    xs = jnp.zeros((nslot, d), F32)
    for s in streams:
        b, l, _ = s["x"].shape
        slot = s["rank"]
        for e in range(N_EXPERTS):
            slot = slot + jnp.where(s["ids"] == e, offs[e], 0)
        s["slots"] = slot.reshape(b * (l // TM), 1, 2 * TM)
        xs = _dispatch(s["slots"], s["h2"].reshape(b * l, d), xs)
    ys = _expert_ffn(tile_expert, n_tiles, xs, wg, wu, wd)
    outs = []
    for s in streams:
        b, l, _ = s["x"].shape
        wts = s["wts"].transpose(0, 1, 3, 2).reshape(b * l, 2)
        outs.append(_combine(s["slots"], wts, s["x"], s["mods"], s["mod_row"], ys))
    return outs


def _rope_table(l):
    t = jnp.arange(l)
    row = (t // GRID_W).astype(F32)
    col = (t % GRID_W).astype(F32)
    inv_freq = ROPE_THETA ** (-jnp.arange(ROPE_QUARTER, dtype=F32) / ROPE_QUARTER)
    ar = inv_freq[:, None] * row[None, :]
    ac = inv_freq[:, None] * col[None, :]
    return jnp.concatenate([jnp.cos(ar), jnp.sin(ar), jnp.cos(ac), jnp.sin(ac)], axis=0)


def _identity_rope_table(l):
    one = jnp.ones((ROPE_QUARTER, l), F32)
    zero = jnp.zeros((ROPE_QUARTER, l), F32)
    return jnp.concatenate([one, zero, one, zero], axis=0)


def kernel(x, c, ctx, c_ctx, w_mod, b_mod, norm1_g, w_in, q_norm_g, k_norm_g, conv_w, conv_b, filt_w1, filt_b1, filt_w2, filt_b2, filt_w3, filt_freq, filt_bias, attn_out_g, hyena_out_g, w_out, norm2_g, router_w, router_bias, expert_w_gate, expert_w_up, expert_w_down):
    depth = w_mod.shape[0]
    b, l, d = x.shape
    lc = ctx.shape[1]
    assert l % TM == 0 and lc == TM and d == D_MODEL
    nt = l // TM
    w = HYENA_WIDTH

    mods = _modulations(c, c_ctx, w_mod, b_mod)
    rope_lat = _rope_table(l)
    rope_ctx = _identity_rope_table(lc)
    dft = _dft_consts(nt)
    dft_ctx = _dense_dft_consts()
    tri = (jnp.arange(TM)[:, None] < jnp.arange(TM)[None, :]).astype(BF16)
    rwt = router_w.T
    rb = router_bias.reshape(-1, 1)
    zero_cnt = jnp.zeros((N_EXPERTS, 1), F32)
    col = lambda a: a.reshape(-1, 1)

    for li in range(depth):
        last = li == depth - 1
        m = mods[li]
        wt = w_in[li].T.astype(BF16)
        wo = w_out[li].astype(BF16)
        g1n = norm1_g[li].reshape(1, d)
        g2n = norm2_g[li].reshape(1, d)
        qg, kg = col(q_norm_g[li]), col(k_norm_g[li])
        cw = conv_w[li].T.reshape(3, w, SHORT_CONV)
        cb = conv_b[li].reshape(3, w, 1)
        ga, gy = col(attn_out_g[li]), col(hyena_out_g[li])
        fargs = (filt_w1[li], filt_b1[li], filt_w2[li], filt_b2[li], filt_w3[li], filt_freq[li])

        q, k, v, u = _inproj(x, m, None, g1n, wt, qg, kg, rope_lat)
        qc, kc, vc, uc = _inproj(ctx, m, b, g1n, wt, qg, kg, rope_ctx)
        a = _attention(q, jnp.concatenate([kc, k], axis=2), jnp.concatenate([vc, v], axis=2))
        hf, hb = _implicit_filters(l, *fargs)
        kf = _filter_spectrum(hf, hb, dft)
        g, x0 = _short_conv_gate(u, cw, cb)
        fb_lanes = jnp.repeat(filt_bias[li], TM).reshape(1, w * TM)
        y = _fft_conv(g, x0, kf, fb_lanes, dft)
        lat = dict(zip(("x", "h2", "ids", "rank", "wts", "counts"),
                       _outproj_route(a, y, x, m, None, ga, gy, wo, g2n, rwt, rb, zero_cnt, tri)))
        lat.update(mods=m, mod_row=None)
        streams = [lat]

        if not last:
            ac = _attention(qc, kc, vc)
            hfc, hbc = _implicit_filters(lc, *fargs)
            yc = _hyena_one_tile(uc, cw, cb, hfc, hbc, col(filt_bias[li]), dft_ctx)
            cs = dict(zip(("x", "h2", "ids", "rank", "wts", "counts"),
                          _outproj_route(ac, yc, ctx, m, b, ga, gy, wo, g2n, rwt, rb, lat["counts"], tri)))
            cs.update(mods=m, mod_row=b)
            streams.append(cs)

        outs = _moe(streams, expert_w_gate[li].astype(BF16), expert_w_up[li].astype(BF16),
                    expert_w_down[li].astype(BF16))
        x = outs[0]
        if not last:
            ctx = outs[1]
    return x
```

```python
import functools
import math

import jax
import jax.numpy as jnp
from jax import lax
from jax.experimental import pallas as pl
from jax.experimental.pallas import tpu as pltpu

F32 = jnp.float32
BF16 = jnp.bfloat16
I32 = jnp.int32
HIGHEST = lax.Precision.HIGHEST

D_MODEL = 1024
N_MOD = 6
EPS = 1e-6
N_Q_HEADS = 8
N_KV_HEADS = 2
HEAD_DIM = 64
KV_REP = N_Q_HEADS // N_KV_HEADS
ATTN_WIDTH = N_Q_HEADS * HEAD_DIM
KV_WIDTH = N_KV_HEADS * HEAD_DIM
ATTN_SCALE = HEAD_DIM ** -0.5
LOG2_E = math.log2(math.e)
V_ROWS = HEAD_DIM + 16
K_COLS = HEAD_DIM + 16
SCORE_BOUND_SLACK = 1.0 + 2.0 ** -6
UNDERFLOW_GUARD = 2.0 ** -80
GRID_W = 64
ROPE_THETA = 10000.0
ROPE_QUARTER = HEAD_DIM // 4
HYENA_WIDTH = D_MODEL - ATTN_WIDTH
SHORT_CONV = 3
FILTER_EMB_DIM = 33
FILTER_HIDDEN = 64
DECAY_TARGET = 1e-2
FAST_DECAY_PCT = 0.3
SLOW_DECAY_PCT = 1.5
N_EXPERTS = 16
N_EXPERT_GROUPS = 4
EXPERTS_PER_GROUP = N_EXPERTS // N_EXPERT_GROUPS
D_FF_EXPERT = 512

TM = 256
ATTN_UNROLL = 4
FFN_TILE = 256
CONV_CB = 128
FFT_CB = 32
FILT_CB = 128
VMEM_LIMIT = 56 * 1024 * 1024


U32 = jnp.uint32
PAIRS_PER_GROUP = EXPERTS_PER_GROUP * (EXPERTS_PER_GROUP - 1) // 2
N_CLASSES = N_EXPERT_GROUPS * PAIRS_PER_GROUP
N_CLASS_ROWS = -(-N_CLASSES // 8) * 8
PAIR_LO = tuple(a for a in range(EXPERTS_PER_GROUP) for b in range(a + 1, EXPERTS_PER_GROUP))
PAIR_HI = tuple(b for a in range(EXPERTS_PER_GROUP) for b in range(a + 1, EXPERTS_PER_GROUP))
ROW_EXTRA = 128


def _pack_bf16_pair(lo, hi):
    lo_bits = lax.bitcast_convert_type(lo.astype(BF16).astype(F32), U32)
    hi_bits = lax.bitcast_convert_type(hi.astype(BF16).astype(F32), U32)
    return lax.shift_right_logical(lo_bits, U32(16)) | (hi_bits & U32(0xFFFF0000))


def _unpack_bf16_pair(words):
    lo = lax.bitcast_convert_type(lax.shift_left(words, U32(16)), F32)
    hi = lax.bitcast_convert_type(words & U32(0xFFFF0000), F32)
    return lo, hi


def _params(*sem):
    return pltpu.CompilerParams(dimension_semantics=tuple(sem), vmem_limit_bytes=VMEM_LIMIT)


def _norm_mod(x, g, shift, scale):
    y = x * lax.rsqrt(jnp.mean(x * x, axis=-1, keepdims=True) + EPS)
    return (y * g) * (1 + scale) + shift


def _mod_kernel(c_ref, w_ref, b_ref, o_ref):
    c = c_ref[...]
    s = c * jax.nn.sigmoid(c)
    o_ref[0] = jnp.dot(s, w_ref[0], preferred_element_type=F32, precision=HIGHEST) + b_ref[0]


def _modulations(c, c_ctx, w_mod, b_mod):
    depth, d, nmd = w_mod.shape
    b = c.shape[0]
    rows = -(-(b + 1) // 8) * 8
    c_all = jnp.zeros((rows, d), F32).at[:b].set(c).at[b].set(c_ctx)
    tn = nmd // 4
    out = pl.pallas_call(
        _mod_kernel,
        grid=(depth, nmd // tn),
        in_specs=[pl.BlockSpec((rows, d), lambda l, j: (0, 0)),
                  pl.BlockSpec((1, d, tn), lambda l, j: (l, 0, j)),
                  pl.BlockSpec((1, 1, tn), lambda l, j: (l, 0, j))],
        out_specs=pl.BlockSpec((1, rows, tn), lambda l, j: (l, 0, j)),
        out_shape=jax.ShapeDtypeStruct((depth, rows, nmd), F32),
        compiler_params=_params("arbitrary", "arbitrary"),
        name="modulation",
    )(c_all, w_mod, b_mod.reshape(depth, 1, nmd))
    return out.reshape(depth, rows, N_MOD, d)


def _inproj_kernel(x_ref, mod_ref, g_ref, wt_ref, qg_ref, kg_ref, rope_ref,
                   q_ref, k_ref, v_ref, u_ref):
    x = x_ref[0]
    tm = x.shape[0]
    h = _norm_mod(x, g_ref[...], mod_ref[0, 0:1, :], mod_ref[0, 1:2, :])
    pt = lax.dot_general(wt_ref[...], h.astype(BF16), (((1,), (1,)), ((), ())),
                         preferred_element_type=F32)
    rope = rope_ref[...]
    qd = ROPE_QUARTER
    cr, sr = rope[0:qd][None], rope[qd:2 * qd][None]
    cc, sc = rope[2 * qd:3 * qd][None], rope[3 * qd:4 * qd][None]

    def norm_rope(t, gain, nh):
        t = t.reshape(nh, HEAD_DIM, tm)
        t = t * lax.rsqrt(jnp.mean(t * t, axis=1, keepdims=True) + EPS) * gain[None]
        a, b = t[:, 0:qd], t[:, qd:2 * qd]
        c, d = t[:, 2 * qd:3 * qd], t[:, 3 * qd:4 * qd]
        return jnp.concatenate([a * cr - b * sr, b * cr + a * sr,
                                c * cc - d * sc, d * cc + c * sc], axis=1)

    q = norm_rope(pt[0:ATTN_WIDTH], qg_ref[...], N_Q_HEADS) * (ATTN_SCALE * LOG2_E)
    q_ref[0] = q.astype(BF16)
    k = norm_rope(pt[ATTN_WIDTH:ATTN_WIDTH + KV_WIDTH], kg_ref[...], N_KV_HEADS)
    kt = k.reshape(KV_WIDTH, tm).T
    for g in range(N_KV_HEADS):
        k_ref[0, g, 0, :, 0:HEAD_DIM] = kt[:, g * HEAD_DIM:(g + 1) * HEAD_DIM].astype(BF16)
    pad_col = lax.broadcasted_iota(I32, (N_KV_HEADS, tm, K_COLS - HEAD_DIM), 2)
    k_ref[0, :, 0, :, HEAD_DIM:K_COLS] = jnp.where(pad_col == 0, 1.0, 0.0).astype(BF16)
    v = pt[ATTN_WIDTH + KV_WIDTH:ATTN_WIDTH + 2 * KV_WIDTH]
    v_ref[0, :, 0, 0:HEAD_DIM] = v.reshape(N_KV_HEADS, HEAD_DIM, tm).astype(BF16)
    pad_row = lax.broadcasted_iota(I32, (N_KV_HEADS, V_ROWS - HEAD_DIM, tm), 1)
    v_ref[0, :, 0, HEAD_DIM:V_ROWS] = jnp.where(pad_row == 0, 1.0, 0.0).astype(BF16)
    u_ref[0, 0] = pt[ATTN_WIDTH + 2 * KV_WIDTH:].astype(BF16)


def _inproj(x, mods, mod_row, g, wt, qg, kg, rope):
    b, l, d = x.shape
    nt = l // TM
    p = wt.shape[0]
    uw = p - ATTN_WIDTH - 2 * KV_WIDTH
    if mod_row is None:
        mod_map = lambda bi, i: (bi, 0, 0)
    else:
        mod_map = lambda bi, i: (mod_row, 0, 0)
    return pl.pallas_call(
        _inproj_kernel,
        grid=(b, nt),
        in_specs=[pl.BlockSpec((1, TM, d), lambda bi, i: (bi, i, 0)),
                  pl.BlockSpec((1, N_MOD, d), mod_map),
                  pl.BlockSpec((1, d), lambda bi, i: (0, 0)),
                  pl.BlockSpec((p, d), lambda bi, i: (0, 0)),
                  pl.BlockSpec((HEAD_DIM, 1), lambda bi, i: (0, 0)),
                  pl.BlockSpec((HEAD_DIM, 1), lambda bi, i: (0, 0)),
                  pl.BlockSpec((HEAD_DIM, TM), lambda bi, i: (0, i))],
        out_specs=[pl.BlockSpec((1, N_Q_HEADS, HEAD_DIM, TM), lambda bi, i: (bi, 0, 0, i)),
                   pl.BlockSpec((1, N_KV_HEADS, 1, TM, K_COLS), lambda bi, i: (bi, 0, i, 0, 0)),
                   pl.BlockSpec((1, N_KV_HEADS, 1, V_ROWS, TM), lambda bi, i: (bi, 0, i, 0, 0)),
                   pl.BlockSpec((1, 1, uw, TM), lambda bi, i: (bi, i, 0, 0))],
        out_shape=[jax.ShapeDtypeStruct((b, N_Q_HEADS, HEAD_DIM, l), BF16),
                   jax.ShapeDtypeStruct((b, N_KV_HEADS, nt, TM, K_COLS), BF16),
                   jax.ShapeDtypeStruct((b, N_KV_HEADS, nt, V_ROWS, TM), BF16),
                   jax.ShapeDtypeStruct((b, nt, uw, TM), BF16)],
        compiler_params=_params("arbitrary", "arbitrary"),
        name="inproj",
    )(x, mods, g, wt, qg, kg, rope)


def _attn_kernel(q_ref, k_ref, v_ref, kmax_ref, o_ref, acc_ref, m_ref):
    nk = k_ref.shape[2]
    tq = q_ref.shape[-1]
    nq = KV_REP * tq
    q = jnp.concatenate([q_ref[0, r] for r in range(KV_REP)], axis=1)
    pad = jnp.zeros((K_COLS - HEAD_DIM - 1, nq), F32)

    def write_out():
        acc = acc_ref[...]
        out = acc[0:HEAD_DIM] / acc[HEAD_DIM:HEAD_DIM + 1]
        for r in range(KV_REP):
            o_ref[0, 0, r * HEAD_DIM:(r + 1) * HEAD_DIM, :] = out[:, r * tq:(r + 1) * tq].astype(o_ref.dtype)

    def run(chunk):
        def group(i, carry):
            for t in range(ATTN_UNROLL):
                chunk(ATTN_UNROLL * i + t)
            return carry

        lax.fori_loop(0, nk // ATTN_UNROLL, group, 0)
        for j in range(nk - nk % ATTN_UNROLL, nk):
            chunk(j)

    qf = q.astype(F32)
    qnorm = jnp.sqrt(jnp.sum(qf * qf, axis=0, keepdims=True))
    bound = qnorm * (kmax_ref[0, 0][:, 0:1] * SCORE_BOUND_SLACK)
    q_shift = jnp.concatenate([q, jnp.concatenate([-bound, pad], axis=0).astype(BF16)], axis=0)
    acc_ref[...] = jnp.zeros_like(acc_ref)

    def chunk_shifted(j):
        s = jnp.dot(k_ref[0, 0, j], q_shift, preferred_element_type=F32)
        p = jnp.exp2(s).astype(BF16)
        acc_ref[...] += jnp.dot(v_ref[0, 0, j], p, preferred_element_type=F32)

    run(chunk_shifted)
    denom_ok = jnp.min(acc_ref[HEAD_DIM:HEAD_DIM + 1, :]) >= UNDERFLOW_GUARD

    @pl.when(denom_ok)
    def _():
        write_out()

    @pl.when(jnp.logical_not(denom_ok))
    def _():
        q_plain = jnp.concatenate([q, jnp.zeros((K_COLS - HEAD_DIM, nq), BF16)], axis=0)
        acc_ref[...] = jnp.zeros_like(acc_ref)
        m_ref[...] = jnp.full_like(m_ref, -jnp.inf)

        def chunk_online(j):
            s = jnp.dot(k_ref[0, 0, j], q_plain, preferred_element_type=F32)
            m = m_ref[...]
            m_new = jnp.maximum(m, jnp.max(s, axis=0, keepdims=True))
            alpha = jnp.exp2(m - m_new)
            p = jnp.exp2(s - m_new).astype(BF16)
            acc_ref[...] = alpha * acc_ref[...] + jnp.dot(v_ref[0, 0, j], p, preferred_element_type=F32)
            m_ref[...] = m_new

        run(chunk_online)
        write_out()


def _attention(q, k, v):
    b, _, _, l = q.shape
    nk = k.shape[2]
    nt = l // TM
    kf = k[..., :HEAD_DIM].astype(F32)
    kmax = jnp.sqrt(jnp.max(jnp.sum(kf * kf, axis=-1), axis=(2, 3)))
    kmax = jnp.broadcast_to(kmax[:, :, None, None], (b, N_KV_HEADS, 1, 128))
    return pl.pallas_call(
        _attn_kernel,
        grid=(b, N_KV_HEADS, nt),
        in_specs=[pl.BlockSpec((1, KV_REP, HEAD_DIM, TM), lambda bi, g, i: (bi, g, 0, i)),
                  pl.BlockSpec((1, 1, nk, TM, K_COLS), lambda bi, g, i: (bi, g, 0, 0, 0)),
                  pl.BlockSpec((1, 1, nk, V_ROWS, TM), lambda bi, g, i: (bi, g, 0, 0, 0)),
                  pl.BlockSpec((1, 1, 1, 128), lambda bi, g, i: (bi, g, 0, 0))],
        out_specs=pl.BlockSpec((1, 1, KV_REP * HEAD_DIM, TM), lambda bi, g, i: (bi, i, g, 0)),
        out_shape=jax.ShapeDtypeStruct((b, nt, ATTN_WIDTH, TM), BF16),
        scratch_shapes=[pltpu.VMEM((V_ROWS, KV_REP * TM), F32), pltpu.VMEM((1, KV_REP * TM), F32)],
        compiler_params=_params("arbitrary", "arbitrary", "arbitrary"),
        name="attention",
    )(q, k, v, kmax)


def _short_conv_tiles(u_ref, w, bias):
    nt, c, tm = u_ref.shape[1:]
    lane = lax.broadcasted_iota(I32, (c, tm), 1)
    tiles = [u_ref[0, i].astype(F32) for i in range(nt)]
    prev = [pltpu.roll(t, 1, 1) for t in tiles]
    nxt = [pltpu.roll(t, tm - 1, 1) for t in tiles]
    zero = jnp.zeros((c, tm), F32)
    out = []
    for i in range(nt):
        up = jnp.where(lane == 0, prev[i - 1] if i > 0 else zero, prev[i])
        un = jnp.where(lane == tm - 1, nxt[i + 1] if i < nt - 1 else zero, nxt[i])
        out.append(bias + up * w[:, 0:1] + tiles[i] * w[:, 1:2] + un * w[:, 2:3])
    return out


def _sconv_kernel(u0_ref, u1_ref, u2_ref, w_ref, b_ref, g_ref, x0_ref):
    x0 = _short_conv_tiles(u0_ref, w_ref[0], b_ref[0])
    x1 = _short_conv_tiles(u1_ref, w_ref[1], b_ref[1])
    v = _short_conv_tiles(u2_ref, w_ref[2], b_ref[2])
    for i in range(len(x0)):
        g_ref[0, i] = (v[i] * x1[i]).astype(g_ref.dtype)
        x0_ref[0, i] = x0[i].astype(x0_ref.dtype)


def _short_conv_gate(u, cw, cb):
    b, nt, _, _ = u.shape
    w = HYENA_WIDTH
    nc = w // CONV_CB
    uspec = lambda grp: pl.BlockSpec((1, nt, CONV_CB, TM), lambda bi, c: (bi, 0, grp * nc + c, 0))
    ospec = pl.BlockSpec((1, nt, CONV_CB, TM), lambda bi, c: (bi, 0, c, 0))
    return pl.pallas_call(
        _sconv_kernel,
        grid=(b, nc),
        in_specs=[uspec(0), uspec(1), uspec(2),
                  pl.BlockSpec((3, CONV_CB, SHORT_CONV), lambda bi, c: (0, c, 0)),
                  pl.BlockSpec((3, CONV_CB, 1), lambda bi, c: (0, c, 0))],
        out_specs=[ospec, ospec],
        out_shape=[jax.ShapeDtypeStruct((b, nt, w, TM), BF16)] * 2,
        compiler_params=_params("arbitrary", "arbitrary"),
        name="short_conv",
    )(u, u, u, cw, cb)


def _dft_consts(nt):
    n1 = 2 * nt
    n = n1 * TM
    two_pi = 2.0 * math.pi

    def cs(prod, mod):
        ang = (prod % mod).astype(F32) * (two_pi / mod)
        return jnp.cos(ang), jnp.sin(ang)

    f1 = jnp.arange(n1, dtype=I32)
    t1 = jnp.arange(nt, dtype=I32)
    t2 = jnp.arange(TM, dtype=I32)
    c, s = cs(f1[:, None] * t1[None, :], n1)
    fwd1 = jnp.concatenate([c, -s], axis=0).astype(BF16)
    c, s = cs(f1[:, None] * t2[None, :], n)
    tw = jnp.stack([c, -s])
    c, s = cs(t2[:, None] * t2[None, :], TM)
    fwd2 = jnp.stack([jnp.concatenate([c, -s], axis=1),
                      jnp.concatenate([s, c], axis=1)]).astype(BF16)
    inv2 = jnp.stack([jnp.concatenate([c, s], axis=1),
                      jnp.concatenate([-s, c], axis=1)]).astype(BF16)
    c, s = cs(t1[:, None] * f1[None, :], n1)
    inv1 = (jnp.stack([c, -s]) * (1.0 / n)).astype(BF16)
    return fwd1, tw, fwd2, inv2, inv1


def _fft_fwd(g, fwd1, tw, fwd2, cb):
    n1 = fwd1.shape[0] // 2
    a = jnp.dot(fwd1, g, preferred_element_type=F32)
    are = jnp.concatenate([a[:n1, c * TM:(c + 1) * TM] for c in range(cb)], axis=0)
    aim = jnp.concatenate([a[n1:, c * TM:(c + 1) * TM] for c in range(cb)], axis=0)
    are = are.reshape(cb, n1, TM)
    aim = aim.reshape(cb, n1, TM)
    twr, twi = tw[0][None], tw[1][None]
    pr = (are * twr - aim * twi).reshape(cb * n1, TM).astype(BF16)
    pi = (are * twi + aim * twr).reshape(cb * n1, TM).astype(BF16)
    return (jnp.dot(pr, fwd2[0], preferred_element_type=F32)
            + jnp.dot(pi, fwd2[1], preferred_element_type=F32))


def _fftconv_kernel(g_ref, x0_ref, k_ref, fb_ref, fwd1_ref, tw_ref, fwd2_ref, inv2_ref, inv1_ref, o_ref):
    cb = g_ref.shape[-1] // TM
    n1 = fwd1_ref.shape[0] // 2
    g = g_ref[0]
    tw = tw_ref[...]
    x = _fft_fwd(g, fwd1_ref[...], tw, fwd2_ref[...], cb)
    kf = k_ref[...]
    xr, xi = x[:, :TM], x[:, TM:]
    kr, ki = kf[:, :TM], kf[:, TM:]
    yr = (xr * kr - xi * ki).astype(BF16)
    yi = (xr * ki + xi * kr).astype(BF16)
    bc = (jnp.dot(yr, inv2_ref[0], preferred_element_type=F32)
          + jnp.dot(yi, inv2_ref[1], preferred_element_type=F32))
    br = bc[:, :TM].reshape(cb, n1, TM)
    bi = bc[:, TM:].reshape(cb, n1, TM)
    twr, twi = tw[0][None], tw[1][None]
    pr = br * twr + bi * twi
    pi = bi * twr - br * twi
    prl = jnp.concatenate([pr[c] for c in range(cb)], axis=1).astype(BF16)
    pil = jnp.concatenate([pi[c] for c in range(cb)], axis=1).astype(BF16)
    y = (jnp.dot(inv1_ref[0], prl, preferred_element_type=F32)
         + jnp.dot(inv1_ref[1], pil, preferred_element_type=F32))
    gf = g.astype(F32)
    o_ref[0] = (x0_ref[0].astype(F32) * (y + fb_ref[...] * gf)).astype(o_ref.dtype)


def _fft_conv(g, x0, kf, fb, consts):
    b, nt, w, _ = g.shape
    n1 = 2 * nt
    cb = FFT_CB
    g2 = g.reshape(b, nt, w * TM)
    x2 = x0.reshape(b, nt, w * TM)
    fwd1, tw, fwd2, inv2, inv1 = consts
    dspec = pl.BlockSpec((1, nt, cb * TM), lambda c, bi: (bi, 0, c))
    full = lambda a: pl.BlockSpec(a.shape, lambda c, bi: (0,) * a.ndim)
    out = pl.pallas_call(
        _fftconv_kernel,
        grid=(w // cb, b),
        in_specs=[dspec, dspec,
                  pl.BlockSpec((cb * n1, 2 * TM), lambda c, bi: (c, 0)),
                  pl.BlockSpec((1, cb * TM), lambda c, bi: (0, c)),
                  full(fwd1), full(tw), full(fwd2), full(inv2), full(inv1)],
        out_specs=dspec,
        out_shape=jax.ShapeDtypeStruct((b, nt, w * TM), BF16),
        compiler_params=_params("arbitrary", "arbitrary"),
        name="long_conv",
    )(g2, x2, kf, fb, fwd1, tw, fwd2, inv2, inv1)
    return out.reshape(b, nt, w, TM)


def _spectrum_kernel(hf_ref, hb_ref, fwd1_ref, tw_ref, fwd2_ref, k_ref):
    cb = hf_ref.shape[-1] // TM
    tw = tw_ref[...]
    xf = _fft_fwd(hf_ref[...].astype(BF16), fwd1_ref[...], tw, fwd2_ref[...], cb)
    xb = _fft_fwd(hb_ref[...].astype(BF16), fwd1_ref[...], tw, fwd2_ref[...], cb)
    k_ref[...] = jnp.concatenate([xf[:, :TM] + xb[:, :TM], xf[:, TM:] - xb[:, TM:]], axis=1)


def _filter_spectrum(hf, hb, consts):
    nt, w, _ = hf.shape
    n1 = 2 * nt
    cb = FFT_CB
    fwd1, tw, fwd2, _, _ = consts
    hspec = pl.BlockSpec((nt, cb * TM), lambda c: (0, c))
    full = lambda a: pl.BlockSpec(a.shape, lambda c: (0,) * a.ndim)
    return pl.pallas_call(
        _spectrum_kernel,
        grid=(w // cb,),
        in_specs=[hspec, hspec, full(fwd1), full(tw), full(fwd2)],
        out_specs=pl.BlockSpec((cb * n1, 2 * TM), lambda c: (c, 0)),
        out_shape=jax.ShapeDtypeStruct((w * n1, 2 * TM), F32),
        compiler_params=_params("arbitrary"),
        name="filter_spectrum",
    )(hf.reshape(nt, w * TM), hb.reshape(nt, w * TM), fwd1, tw, fwd2)


def _filter_kernel(z_ref, t_ref, w1_ref, b1_ref, w2_ref, b2_ref, fr_ref, w3f_ref, w3b_ref, dl_ref,
                   hf_ref, hb_ref):
    nt = hf_ref.shape[0]
    dot = functools.partial(jnp.dot, preferred_element_type=F32, precision=HIGHEST)
    fr = fr_ref[...]
    h = jnp.sin(fr * (dot(w1_ref[...], z_ref[...]) + b1_ref[...]))
    h = jnp.sin(fr * (dot(w2_ref[...], h) + b2_ref[...]))
    decay = jnp.exp(-t_ref[...] * jnp.abs(dl_ref[...]))
    hf = dot(w3f_ref[...], h) * decay
    hb = dot(w3b_ref[...], h) * decay
    norm = (jnp.sum(jnp.abs(hf), axis=1, keepdims=True)
            + jnp.sum(jnp.abs(hb), axis=1, keepdims=True))
    hf = hf / norm
    hb = hb / norm
    lane = lax.broadcasted_iota(I32, hb.shape, 1)
    hb = jnp.where(lane == 0, 0.0, hb)
    for i in range(nt):
        hf_ref[i] = hf[:, i * TM:(i + 1) * TM]
        hb_ref[i] = hb[:, i * TM:(i + 1) * TM]


def _implicit_filters(l, fw1, fb1, fw2, fb2, fw3, ffreq):
    nt = l // TM
    w = HYENA_WIDTH
    t = jnp.linspace(0.0, 1.0, l, dtype=F32)[None, :]
    bands = (FILTER_EMB_DIM - 1) // 2
    wv = 2.0 * math.pi * jnp.arange(l, dtype=F32)[None, :] / l
    f = jnp.linspace(1e-4, bands - 1, bands, dtype=F32)[:, None]
    z = jnp.concatenate([t, jnp.cos(f * wv), -jnp.sin(f * wv)], axis=0)
    min_decay = math.log(DECAY_TARGET) / SLOW_DECAY_PCT
    max_decay = math.log(DECAY_TARGET) / FAST_DECAY_PCT
    deltas = jnp.linspace(min_decay, max_decay, w, dtype=F32)[:, None]
    w3t = fw3.T
    nc = w // FILT_CB
    col = lambda a: a.reshape(-1, 1)
    full = lambda a: pl.BlockSpec(a.shape, lambda c: (0,) * a.ndim)
    args = (z, t, fw1.T, col(fb1), fw2.T, col(fb2), col(ffreq))
    ospec = pl.BlockSpec((nt, FILT_CB, TM), lambda c: (0, c, 0))
    return pl.pallas_call(
        _filter_kernel,
        grid=(nc,),
        in_specs=[full(a) for a in args] + [
            pl.BlockSpec((FILT_CB, FILTER_HIDDEN), lambda c: (c, 0)),
            pl.BlockSpec((FILT_CB, FILTER_HIDDEN), lambda c: (nc + c, 0)),
            pl.BlockSpec((FILT_CB, 1), lambda c: (c, 0))],
        out_specs=[ospec, ospec],
        out_shape=[jax.ShapeDtypeStruct((nt, w, TM), F32)] * 2,
        compiler_params=_params("arbitrary"),
        name="implicit_filter",
    )(*args, w3t, w3t, deltas)


def _dense_dft_consts():
    n = 2 * TM
    two_pi = 2.0 * math.pi
    t = jnp.arange(TM, dtype=I32)
    f = jnp.arange(n, dtype=I32)
    ang = ((t[:, None] * f[None, :]) % n).astype(F32) * (two_pi / n)
    fwd = jnp.concatenate([jnp.cos(ang), -jnp.sin(ang)], axis=1).astype(BF16)
    inv = (jnp.stack([jnp.cos(ang).T, -jnp.sin(ang).T]) * (1.0 / n)).astype(BF16)
    return fwd, inv


def _hyena_tile_kernel(u_ref, w_ref, b_ref, hf_ref, hb_ref, fb_ref, fwd_ref, inv_ref, o_ref):
    w = HYENA_WIDTH
    n = 2 * TM
    x0 = _short_conv_tiles(u_ref.at[:, :, 0:w], w_ref[0], b_ref[0])[0]
    x1 = _short_conv_tiles(u_ref.at[:, :, w:2 * w], w_ref[1], b_ref[1])[0]
    v = _short_conv_tiles(u_ref.at[:, :, 2 * w:3 * w], w_ref[2], b_ref[2])[0]
    g = (v * x1).astype(BF16)
    fwd = fwd_ref[...]
    dot = functools.partial(jnp.dot, preferred_element_type=F32)
    kf = dot(hf_ref[0].astype(BF16), fwd)
    kb = dot(hb_ref[0].astype(BF16), fwd)
    kr = kf[:, :n] + kb[:, :n]
    ki = kf[:, n:] - kb[:, n:]
    x = dot(g, fwd)
    xr, xi = x[:, :n], x[:, n:]
    yr = (xr * kr - xi * ki).astype(BF16)
    yi = (xr * ki + xi * kr).astype(BF16)
    y = dot(yr, inv_ref[0]) + dot(yi, inv_ref[1])
    o_ref[0, 0] = (x0 * (y + fb_ref[...] * g.astype(F32))).astype(o_ref.dtype)


def _hyena_one_tile(u, cw, cb, hf, hb, fbias, consts):
    b = u.shape[0]
    w = HYENA_WIDTH
    fwd, inv = consts
    full = lambda a: pl.BlockSpec(a.shape, lambda bi: (0,) * a.ndim)
    return pl.pallas_call(
        _hyena_tile_kernel,
        grid=(b,),
        in_specs=[pl.BlockSpec((1, 1, 3 * w, TM), lambda bi: (bi, 0, 0, 0)),
                  full(cw), full(cb), full(hf), full(hb), full(fbias), full(fwd), full(inv)],
        out_specs=pl.BlockSpec((1, 1, w, TM), lambda bi: (bi, 0, 0, 0)),
        out_shape=jax.ShapeDtypeStruct((b, 1, w, TM), BF16),
        compiler_params=_params("arbitrary"),
        name="hyena_context",
    )(u, cw, cb, hf, hb, fbias, fwd, inv)


def _top2_sum(a, b, c, d):
    hi1, lo1 = jnp.maximum(a, b), jnp.minimum(a, b)
    hi2, lo2 = jnp.maximum(c, d), jnp.minimum(c, d)
    return jnp.maximum(hi1, hi2) + jnp.maximum(jnp.minimum(hi1, hi2), jnp.maximum(lo1, lo2))


def _outproj_kernel(a_ref, y_ref, x_ref, mod_ref, ga_ref, gy_ref, wo_ref, n2_ref, rw_ref, rb_ref,
                    cnt_ref, tri_ref, xo_ref, row_ref, route_ref, cnto_ref, cnt_sc):
    first = jnp.logical_and(pl.program_id(0) == 0, pl.program_id(1) == 0)

    @pl.when(first)
    def _():
        cnt_sc[...] = cnt_ref[...]

    def group_norm(t, gain):
        t = t.astype(F32)
        return t * lax.rsqrt(jnp.mean(t * t, axis=0, keepdims=True) + EPS) * gain

    mix = jnp.concatenate([group_norm(a_ref[0, 0], ga_ref[...]),
                           group_norm(y_ref[0, 0], gy_ref[...])], axis=0).astype(BF16)
    o = lax.dot_general(mix, wo_ref[...], (((0,), (0,)), ((), ())), preferred_element_type=F32)
    xn = x_ref[0] + mod_ref[0, 2:3, :] * o
    xo_ref[0] = xn
    h2 = _norm_mod(xn, n2_ref[...], mod_ref[0, 3:4, :], mod_ref[0, 4:5, :])

    tm = xn.shape[0]
    logits = lax.dot_general(rw_ref[...], h2, (((1,), (1,)), ((), ())),
                             preferred_element_type=F32, precision=HIGHEST)
    score = jax.nn.sigmoid(logits)
    sel = score + rb_ref[...]
    srow = [sel[e:e + 1] for e in range(N_EXPERTS)]
    prow = [score[e:e + 1] for e in range(N_EXPERTS)]
    epg = EXPERTS_PER_GROUP
    gs = [_top2_sum(*srow[g * epg:(g + 1) * epg]) for g in range(N_EXPERT_GROUPS)]
    bg = jnp.zeros((1, tm), I32)
    best = gs[0]
    for g in range(1, N_EXPERT_GROUPS):
        upd = gs[g] > best
        bg = jnp.where(upd, g, bg)
        best = jnp.where(upd, gs[g], best)

    def pick_group(rows, j):
        out = rows[j]
        for g in range(1, N_EXPERT_GROUPS):
            out = jnp.where(bg == g, rows[g * epg + j], out)
        return out

    cand = [pick_group(srow, j) for j in range(epg)]
    cprob = [pick_group(prow, j) for j in range(epg)]
    i1 = jnp.zeros((1, tm), I32)
    v1, w1 = cand[0], cprob[0]
    for j in range(1, epg):
        upd = cand[j] > v1
        i1 = jnp.where(upd, j, i1)
        v1 = jnp.where(upd, cand[j], v1)
        w1 = jnp.where(upd, cprob[j], w1)
    i2 = jnp.zeros((1, tm), I32)
    v2 = jnp.full((1, tm), -jnp.inf, F32)
    w2 = jnp.zeros((1, tm), F32)
    for j in range(epg):
        upd = jnp.logical_and(i1 != j, cand[j] > v2)
        i2 = jnp.where(upd, j, i2)
        v2 = jnp.where(upd, cand[j], v2)
        w2 = jnp.where(upd, cprob[j], w2)
    den = w1 + w2
    w1, w2 = w1 / den, w2 / den
    first_lower = i1 < i2
    lo = jnp.where(first_lower, i1, i2)
    hi = jnp.where(first_lower, i2, i1)
    w_lo = jnp.where(first_lower, w1, w2)
    w_hi = jnp.where(first_lower, w2, w1)
    pair = jnp.where(lo == 0, hi - 1, jnp.where(lo == 1, hi + 1, PAIRS_PER_GROUP - 1))
    cls = bg * PAIRS_PER_GROUP + pair

    cio = lax.broadcasted_iota(I32, (N_CLASS_ROWS, tm), 0)
    onehot = jnp.where(cio == cls, 1.0, 0.0)
    cum = jnp.dot(onehot.astype(BF16), tri_ref[...], preferred_element_type=F32)
    base = cnt_sc[...]
    rank = jnp.sum(onehot * (base + cum), axis=0, keepdims=True)
    route_ref[0, 0] = jnp.concatenate([cls, rank.astype(I32)], axis=0)
    new = base + jnp.sum(onehot, axis=1, keepdims=True)
    cnt_sc[...] = new
    cnto_ref[...] = new

    half = h2.shape[1] // 2
    wrows = jnp.concatenate([w_lo, w_hi, jnp.zeros((ROW_EXTRA - 2, tm), F32)], axis=0)
    row_ref[0] = jnp.concatenate([_pack_bf16_pair(h2[:, :half], h2[:, half:]),
                                  lax.bitcast_convert_type(wrows.T, U32)], axis=1)


def _outproj_route(a, y, x, mods, mod_row, ga, gy, wo, n2g, rwt, rb, cnt, tri):
    b, l, d = x.shape
    nt = l // TM
    if mod_row is None:
        mod_map = lambda bi, i: (bi, 0, 0)
    else:
        mod_map = lambda bi, i: (mod_row, 0, 0)
    full = lambda arr: pl.BlockSpec(arr.shape, lambda bi, i: (0,) * arr.ndim)
    tile = pl.BlockSpec((1, 1, ATTN_WIDTH, TM), lambda bi, i: (bi, i, 0, 0))
    xspec = pl.BlockSpec((1, TM, d), lambda bi, i: (bi, i, 0))
    rspec = pl.BlockSpec((1, 1, 2, TM), lambda bi, i: (bi, i, 0, 0))
    row_w = d // 2 + ROW_EXTRA
    return pl.pallas_call(
        _outproj_kernel,
        grid=(b, nt),
        in_specs=[tile, tile, xspec, pl.BlockSpec((1, N_MOD, d), mod_map),
                  full(ga), full(gy), full(wo), full(n2g), full(rwt), full(rb), full(cnt), full(tri)],
        out_specs=[xspec, pl.BlockSpec((1, TM, row_w), lambda bi, i: (bi, i, 0)), rspec, full(cnt)],
        out_shape=[jax.ShapeDtypeStruct((b, l, d), F32),
                   jax.ShapeDtypeStruct((b, l, row_w), U32),
                   jax.ShapeDtypeStruct((b, nt, 2, TM), I32),
                   jax.ShapeDtypeStruct(cnt.shape, F32)],
        scratch_shapes=[pltpu.VMEM(cnt.shape, F32)],
        compiler_params=_params("arbitrary", "arbitrary"),
        name="outproj_route",
    )(a, y, x, mods, ga, gy, wo, n2g, rwt, rb, cnt, tri)


def _dispatch_kernel(slot_ref, h_ref, xs_in_ref, xs_ref, sem):
    del xs_in_ref
    tm = h_ref.shape[0]

    def copy(r, s):
        return pltpu.make_async_copy(h_ref.at[pl.ds(r, 1)], xs_ref.at[pl.ds(s, 1)], sem)

    def issue(r, carry):
        copy(r, slot_ref[0, 0, r]).start()
        return carry

    lax.fori_loop(0, tm, issue, 0, unroll=8)

    def drain(r, carry):
        copy(0, 0).wait()
        return carry

    lax.fori_loop(0, tm, drain, 0, unroll=8)


def _dispatch(slots, h2, xs):
    t, d = h2.shape
    return pl.pallas_call(
        _dispatch_kernel,
        grid=(t // TM,),
        in_specs=[pl.BlockSpec((1, 1, TM), lambda i: (i, 0, 0), memory_space=pltpu.SMEM),
                  pl.BlockSpec((TM, d), lambda i: (i, 0)),
                  pl.BlockSpec(memory_space=pl.ANY)],
        out_specs=pl.BlockSpec(memory_space=pl.ANY),
        out_shape=jax.ShapeDtypeStruct(xs.shape, xs.dtype),
        scratch_shapes=[pltpu.SemaphoreType.DMA(())],
        input_output_aliases={2: 0},
        compiler_params=_params("arbitrary"),
        name="moe_dispatch",
    )(slots, h2, xs)


def _ffn_kernel(ea_ref, eb_ref, nt_ref, xs_ref, wga_ref, wua_ref, wda_ref, wgb_ref, wub_ref, wdb_ref, ys_ref):
    del ea_ref, eb_ref
    live = pl.program_id(0) < nt_ref[0]
    half = ys_ref.shape[1]

    @pl.when(live)
    def _():
        words = xs_ref[...]
        lo, hi = _unpack_bf16_pair(words[:, :half])
        lo, hi = lo.astype(BF16), hi.astype(BF16)
        wts = lax.bitcast_convert_type(words[:, half:half + 2], F32)

        def expert(wg_ref, wu_ref, wd_ref):
            dot = functools.partial(jnp.dot, preferred_element_type=F32)
            g = dot(lo, wg_ref[0, :half, :]) + dot(hi, wg_ref[0, half:, :])
            u = dot(lo, wu_ref[0, :half, :]) + dot(hi, wu_ref[0, half:, :])
            a = (g * jax.nn.sigmoid(g)) * u
            return dot(a.astype(BF16), wd_ref[0])

        y = (wts[:, 0:1] * expert(wga_ref, wua_ref, wda_ref)
             + wts[:, 1:2] * expert(wgb_ref, wub_ref, wdb_ref))
        ys_ref[...] = _pack_bf16_pair(y[:, :half], y[:, half:])

    @pl.when(jnp.logical_not(live))
    def _():
        ys_ref[...] = jnp.zeros_like(ys_ref)


def _expert_ffn(tile_ea, tile_eb, n_tiles, xs, wg, wu, wd):
    nslot, row_w = xs.shape
    d, f = wg.shape[1:]
    ntile = nslot // FFN_TILE
    row = lambda i, ea, eb, nt: (jnp.minimum(i, nt[0] - 1), 0)
    amap = lambda i, ea, eb, nt: (ea[jnp.minimum(i, nt[0] - 1)], 0, 0)
    bmap = lambda i, ea, eb, nt: (eb[jnp.minimum(i, nt[0] - 1)], 0, 0)
    return pl.pallas_call(
        _ffn_kernel,
        grid_spec=pltpu.PrefetchScalarGridSpec(
            num_scalar_prefetch=3,
            grid=(ntile,),
            in_specs=[pl.BlockSpec((FFN_TILE, row_w), row),
                      pl.BlockSpec((1, d, f), amap), pl.BlockSpec((1, d, f), amap), pl.BlockSpec((1, f, d), amap),
                      pl.BlockSpec((1, d, f), bmap), pl.BlockSpec((1, d, f), bmap), pl.BlockSpec((1, f, d), bmap)],
            out_specs=pl.BlockSpec((FFN_TILE, d // 2), lambda i, ea, eb, nt: (i, 0))),
        out_shape=jax.ShapeDtypeStruct((nslot, d // 2), U32),
        compiler_params=_params("arbitrary"),
        name="moe_experts",
    )(tile_ea, tile_eb, n_tiles, xs, wg, wu, wd, wg, wu, wd)


def _combine_kernel(slot_ref, x_ref, mod_ref, ys_ref, o_ref, buf, sem):
    tm = x_ref.shape[1]

    def copy(r, s):
        return pltpu.make_async_copy(ys_ref.at[pl.ds(s, 1)], buf.at[pl.ds(r, 1)], sem)

    def issue(r, carry):
        copy(r, slot_ref[0, 0, r]).start()
        return carry

    lax.fori_loop(0, tm, issue, 0, unroll=8)

    def drain(r, carry):
        copy(0, 0).wait()
        return carry

    lax.fori_loop(0, tm, drain, 0, unroll=8)
    lo, hi = _unpack_bf16_pair(buf[...])
    y = jnp.concatenate([lo, hi], axis=1)
    o_ref[0] = x_ref[0] + mod_ref[0, 5:6, :] * y


def _combine(slots, x, mods, mod_row, ys):
    b, l, d = x.shape
    nt = l // TM
    if mod_row is None:
        mod_map = lambda bi, i: (bi, 0, 0)
    else:
        mod_map = lambda bi, i: (mod_row, 0, 0)
    xspec = pl.BlockSpec((1, TM, d), lambda bi, i: (bi, i, 0))
    return pl.pallas_call(
        _combine_kernel,
        grid=(b, nt),
        in_specs=[pl.BlockSpec((1, 1, TM), lambda bi, i: (bi * nt + i, 0, 0), memory_space=pltpu.SMEM),
                  xspec,
                  pl.BlockSpec((1, N_MOD, d), mod_map),
                  pl.BlockSpec(memory_space=pl.ANY)],
        out_specs=xspec,
        out_shape=jax.ShapeDtypeStruct((b, l, d), F32),
        scratch_shapes=[pltpu.VMEM((TM, d // 2), U32), pltpu.SemaphoreType.DMA(())],
        compiler_params=_params("arbitrary", "arbitrary"),
        name="moe_combine",
    )(slots, x, mods, ys)


def _moe(streams, wg, wu, wd):
    counts = streams[-1]["counts"].reshape(-1).astype(I32)
    padded = ((counts + FFN_TILE - 1) // FFN_TILE) * FFN_TILE
    ends = jnp.cumsum(padded)
    offs = ends - padded
    total = sum(s["x"].shape[0] * s["x"].shape[1] for s in streams)
    ntile = total // FFN_TILE + N_CLASSES
    nslot = ntile * FFN_TILE
    tile_start = jnp.arange(ntile, dtype=I32) * FFN_TILE
    tile_class = jnp.minimum(jnp.sum((tile_start[:, None] >= ends[None, :N_CLASSES]).astype(I32), axis=1),
                             N_CLASSES - 1)
    group_base = (tile_class // PAIRS_PER_GROUP) * EXPERTS_PER_GROUP
    tile_ea = (group_base + jnp.asarray(PAIR_LO, I32)[tile_class % PAIRS_PER_GROUP]).astype(I32)
    tile_eb = (group_base + jnp.asarray(PAIR_HI, I32)[tile_class % PAIRS_PER_GROUP]).astype(I32)
    n_tiles = (ends[-1] // FFN_TILE).astype(I32).reshape(1)

    row_w = streams[0]["rows"].shape[-1]
    xs = jnp.zeros((nslot, row_w), U32)
    for s in streams:
        b, l, _ = s["x"].shape
        cls, rank = s["route"][:, :, 0], s["route"][:, :, 1]
        slot = rank
        for c in range(N_CLASSES):
            slot = slot + jnp.where(cls == c, offs[c], 0)
        s["slots"] = slot.reshape(b * (l // TM), 1, TM)
        xs = _dispatch(s["slots"], s["rows"].reshape(b * l, row_w), xs)
    ys = _expert_ffn(tile_ea, tile_eb, n_tiles, xs, wg, wu, wd)
    return [_combine(s["slots"], s["x"], s["mods"], s["mod_row"], ys) for s in streams]


def _rope_table(l):
    t = jnp.arange(l)
    row = (t // GRID_W).astype(F32)
    col = (t % GRID_W).astype(F32)
    inv_freq = ROPE_THETA ** (-jnp.arange(ROPE_QUARTER, dtype=F32) / ROPE_QUARTER)
    ar = inv_freq[:, None] * row[None, :]
    ac = inv_freq[:, None] * col[None, :]
    return jnp.concatenate([jnp.cos(ar), jnp.sin(ar), jnp.cos(ac), jnp.sin(ac)], axis=0)


def _identity_rope_table(l):
    one = jnp.ones((ROPE_QUARTER, l), F32)
    zero = jnp.zeros((ROPE_QUARTER, l), F32)
    return jnp.concatenate([one, zero, one, zero], axis=0)


def kernel(x, c, ctx, c_ctx, w_mod, b_mod, norm1_g, w_in, q_norm_g, k_norm_g, conv_w, conv_b, filt_w1, filt_b1, filt_w2, filt_b2, filt_w3, filt_freq, filt_bias, attn_out_g, hyena_out_g, w_out, norm2_g, router_w, router_bias, expert_w_gate, expert_w_up, expert_w_down):
    depth = w_mod.shape[0]
    b, l, d = x.shape
    lc = ctx.shape[1]
    assert l % TM == 0 and lc == TM and d == D_MODEL
    nt = l // TM
    w = HYENA_WIDTH

    mods = _modulations(c, c_ctx, w_mod, b_mod)
    rope_lat = _rope_table(l)
    rope_ctx = _identity_rope_table(lc)
    dft = _dft_consts(nt)
    dft_ctx = _dense_dft_consts()
    tri = (jnp.arange(TM)[:, None] < jnp.arange(TM)[None, :]).astype(BF16)
    rwt = router_w.T
    rb = router_bias.reshape(-1, 1)
    zero_cnt = jnp.zeros((N_CLASS_ROWS, 1), F32)
    col = lambda a: a.reshape(-1, 1)

    for li in range(depth):
        last = li == depth - 1
        m = mods[li]
        wt = w_in[li].T.astype(BF16)
        wo = w_out[li].astype(BF16)
        g1n = norm1_g[li].reshape(1, d)
        g2n = norm2_g[li].reshape(1, d)
        qg, kg = col(q_norm_g[li]), col(k_norm_g[li])
        cw = conv_w[li].T.reshape(3, w, SHORT_CONV)
        cb = conv_b[li].reshape(3, w, 1)
        ga, gy = col(attn_out_g[li]), col(hyena_out_g[li])
        fargs = (filt_w1[li], filt_b1[li], filt_w2[li], filt_b2[li], filt_w3[li], filt_freq[li])

        q, k, v, u = _inproj(x, m, None, g1n, wt, qg, kg, rope_lat)
        qc, kc, vc, uc = _inproj(ctx, m, b, g1n, wt, qg, kg, rope_ctx)
        a = _attention(q, jnp.concatenate([kc, k], axis=2), jnp.concatenate([vc, v], axis=2))
        hf, hb = _implicit_filters(l, *fargs)
        kf = _filter_spectrum(hf, hb, dft)
        g, x0 = _short_conv_gate(u, cw, cb)
        fb_lanes = jnp.repeat(filt_bias[li], TM).reshape(1, w * TM)
        y = _fft_conv(g, x0, kf, fb_lanes, dft)
        lat = dict(zip(("x", "rows", "route", "counts"),
                       _outproj_route(a, y, x, m, None, ga, gy, wo, g2n, rwt, rb, zero_cnt, tri)))
        lat.update(mods=m, mod_row=None)
        streams = [lat]

        if not last:
            ac = _attention(qc, kc, vc)
            hfc, hbc = _implicit_filters(lc, *fargs)
            yc = _hyena_one_tile(uc, cw, cb, hfc, hbc, col(filt_bias[li]), dft_ctx)
            cs = dict(zip(("x", "rows", "route", "counts"),
                          _outproj_route(ac, yc, ctx, m, b, ga, gy, wo, g2n, rwt, rb, lat["counts"], tri)))
            cs.update(mods=m, mod_row=b)
            streams.append(cs)

        outs = _moe(streams, expert_w_gate[li].astype(BF16), expert_w_up[li].astype(BF16),
                    expert_w_down[li].astype(BF16))
        x = outs[0]
        if not last:
            ctx = outs[1]
    return x
```

```python
import functools
import math

import jax
import jax.numpy as jnp
from jax import lax
from jax.experimental import pallas as pl
from jax.experimental.pallas import tpu as pltpu

F32 = jnp.float32
BF16 = jnp.bfloat16
I32 = jnp.int32
HIGHEST = lax.Precision.HIGHEST

D_MODEL = 1024
N_MOD = 6
EPS = 1e-6
N_Q_HEADS = 8
N_KV_HEADS = 2
HEAD_DIM = 64
KV_REP = N_Q_HEADS // N_KV_HEADS
ATTN_WIDTH = N_Q_HEADS * HEAD_DIM
KV_WIDTH = N_KV_HEADS * HEAD_DIM
ATTN_SCALE = HEAD_DIM ** -0.5
LOG2_E = math.log2(math.e)
V_ROWS = HEAD_DIM + 16
K_COLS = HEAD_DIM + 16
SCORE_BOUND_SLACK = 1.0 + 2.0 ** -6
UNDERFLOW_GUARD = 2.0 ** -80
GRID_W = 64
ROPE_THETA = 10000.0
ROPE_QUARTER = HEAD_DIM // 4
HYENA_WIDTH = D_MODEL - ATTN_WIDTH
SHORT_CONV = 3
FILTER_EMB_DIM = 33
FILTER_HIDDEN = 64
DECAY_TARGET = 1e-2
FAST_DECAY_PCT = 0.3
SLOW_DECAY_PCT = 1.5
N_EXPERTS = 16
N_EXPERT_GROUPS = 4
EXPERTS_PER_GROUP = N_EXPERTS // N_EXPERT_GROUPS
D_FF_EXPERT = 512

TM = 256
ATTN_UNROLL = 8
ATTN_TQ = 1024
FFN_TILE = 256
CONV_CB = 128
FFT_CB = 32
FILT_CB = 128
VMEM_LIMIT = 56 * 1024 * 1024


U32 = jnp.uint32
PAIRS_PER_GROUP = EXPERTS_PER_GROUP * (EXPERTS_PER_GROUP - 1) // 2
N_CLASSES = N_EXPERT_GROUPS * PAIRS_PER_GROUP
N_CLASS_ROWS = -(-N_CLASSES // 8) * 8
PAIR_LO = tuple(a for a in range(EXPERTS_PER_GROUP) for b in range(a + 1, EXPERTS_PER_GROUP))
PAIR_HI = tuple(b for a in range(EXPERTS_PER_GROUP) for b in range(a + 1, EXPERTS_PER_GROUP))
ROW_EXTRA = 128


def _pack_bf16_pair(lo, hi):
    lo_bits = lax.bitcast_convert_type(lo.astype(BF16).astype(F32), U32)
    hi_bits = lax.bitcast_convert_type(hi.astype(BF16).astype(F32), U32)
    return lax.shift_right_logical(lo_bits, U32(16)) | (hi_bits & U32(0xFFFF0000))


def _unpack_bf16_pair(words):
    lo = lax.bitcast_convert_type(lax.shift_left(words, U32(16)), F32)
    hi = lax.bitcast_convert_type(words & U32(0xFFFF0000), F32)
    return lo, hi


def _params(*sem):
    return pltpu.CompilerParams(dimension_semantics=tuple(sem), vmem_limit_bytes=VMEM_LIMIT)


def _norm_mod(x, g, shift, scale):
    y = x * lax.rsqrt(jnp.mean(x * x, axis=-1, keepdims=True) + EPS)
    return (y * g) * (1 + scale) + shift


def _mod_kernel(c_ref, w_ref, b_ref, o_ref):
    c = c_ref[...]
    s = c * jax.nn.sigmoid(c)
    o_ref[0] = jnp.dot(s, w_ref[0], preferred_element_type=F32, precision=HIGHEST) + b_ref[0]


def _modulations(c, c_ctx, w_mod, b_mod):
    depth, d, nmd = w_mod.shape
    b = c.shape[0]
    rows = -(-(b + 1) // 8) * 8
    c_all = jnp.zeros((rows, d), F32).at[:b].set(c).at[b].set(c_ctx)
    tn = nmd // 4
    out = pl.pallas_call(
        _mod_kernel,
        grid=(depth, nmd // tn),
        in_specs=[pl.BlockSpec((rows, d), lambda l, j: (0, 0)),
                  pl.BlockSpec((1, d, tn), lambda l, j: (l, 0, j)),
                  pl.BlockSpec((1, 1, tn), lambda l, j: (l, 0, j))],
        out_specs=pl.BlockSpec((1, rows, tn), lambda l, j: (l, 0, j)),
        out_shape=jax.ShapeDtypeStruct((depth, rows, nmd), F32),
        compiler_params=_params("arbitrary", "arbitrary"),
        name="modulation",
    )(c_all, w_mod, b_mod.reshape(depth, 1, nmd))
    return out.reshape(depth, rows, N_MOD, d)


def _inproj_kernel(x_ref, mod_ref, g_ref, wt_ref, qg_ref, kg_ref, rope_ref,
                   q_ref, k_ref, v_ref, u_ref):
    x = x_ref[0]
    tm = x.shape[0]
    h = _norm_mod(x, g_ref[...], mod_ref[0, 0:1, :], mod_ref[0, 1:2, :])
    pt = lax.dot_general(wt_ref[...], h.astype(BF16), (((1,), (1,)), ((), ())),
                         preferred_element_type=F32)
    rope = rope_ref[...]
    qd = ROPE_QUARTER
    cr, sr = rope[0:qd][None], rope[qd:2 * qd][None]
    cc, sc = rope[2 * qd:3 * qd][None], rope[3 * qd:4 * qd][None]

    def norm_rope(t, gain, nh):
        t = t.reshape(nh, HEAD_DIM, tm)
        t = t * lax.rsqrt(jnp.mean(t * t, axis=1, keepdims=True) + EPS) * gain[None]
        a, b = t[:, 0:qd], t[:, qd:2 * qd]
        c, d = t[:, 2 * qd:3 * qd], t[:, 3 * qd:4 * qd]
        return jnp.concatenate([a * cr - b * sr, b * cr + a * sr,
                                c * cc - d * sc, d * cc + c * sc], axis=1)

    q = norm_rope(pt[0:ATTN_WIDTH], qg_ref[...], N_Q_HEADS) * (ATTN_SCALE * LOG2_E)
    q_ref[0] = q.astype(BF16)
    k = norm_rope(pt[ATTN_WIDTH:ATTN_WIDTH + KV_WIDTH], kg_ref[...], N_KV_HEADS)
    kt = k.reshape(KV_WIDTH, tm).T
    for g in range(N_KV_HEADS):
        k_ref[0, g, 0, :, 0:HEAD_DIM] = kt[:, g * HEAD_DIM:(g + 1) * HEAD_DIM].astype(BF16)
    pad_col = lax.broadcasted_iota(I32, (N_KV_HEADS, tm, K_COLS - HEAD_DIM), 2)
    k_ref[0, :, 0, :, HEAD_DIM:K_COLS] = jnp.where(pad_col == 0, 1.0, 0.0).astype(BF16)
    v = pt[ATTN_WIDTH + KV_WIDTH:ATTN_WIDTH + 2 * KV_WIDTH]
    v_ref[0, :, 0, 0:HEAD_DIM] = v.reshape(N_KV_HEADS, HEAD_DIM, tm).astype(BF16)
    pad_row = lax.broadcasted_iota(I32, (N_KV_HEADS, V_ROWS - HEAD_DIM, tm), 1)
    v_ref[0, :, 0, HEAD_DIM:V_ROWS] = jnp.where(pad_row == 0, 1.0, 0.0).astype(BF16)
    u_ref[0, 0] = pt[ATTN_WIDTH + 2 * KV_WIDTH:].astype(BF16)


def _inproj(x, mods, mod_row, g, wt, qg, kg, rope):
    b, l, d = x.shape
    nt = l // TM
    p = wt.shape[0]
    uw = p - ATTN_WIDTH - 2 * KV_WIDTH
    if mod_row is None:
        mod_map = lambda bi, i: (bi, 0, 0)
    else:
        mod_map = lambda bi, i: (mod_row, 0, 0)
    return pl.pallas_call(
        _inproj_kernel,
        grid=(b, nt),
        in_specs=[pl.BlockSpec((1, TM, d), lambda bi, i: (bi, i, 0)),
                  pl.BlockSpec((1, N_MOD, d), mod_map),
                  pl.BlockSpec((1, d), lambda bi, i: (0, 0)),
                  pl.BlockSpec((p, d), lambda bi, i: (0, 0)),
                  pl.BlockSpec((HEAD_DIM, 1), lambda bi, i: (0, 0)),
                  pl.BlockSpec((HEAD_DIM, 1), lambda bi, i: (0, 0)),
                  pl.BlockSpec((HEAD_DIM, TM), lambda bi, i: (0, i))],
        out_specs=[pl.BlockSpec((1, N_Q_HEADS, HEAD_DIM, TM), lambda bi, i: (bi, 0, 0, i)),
                   pl.BlockSpec((1, N_KV_HEADS, 1, TM, K_COLS), lambda bi, i: (bi, 0, i, 0, 0)),
                   pl.BlockSpec((1, N_KV_HEADS, 1, V_ROWS, TM), lambda bi, i: (bi, 0, i, 0, 0)),
                   pl.BlockSpec((1, 1, uw, TM), lambda bi, i: (bi, i, 0, 0))],
        out_shape=[jax.ShapeDtypeStruct((b, N_Q_HEADS, HEAD_DIM, l), BF16),
                   jax.ShapeDtypeStruct((b, N_KV_HEADS, nt, TM, K_COLS), BF16),
                   jax.ShapeDtypeStruct((b, N_KV_HEADS, nt, V_ROWS, TM), BF16),
                   jax.ShapeDtypeStruct((b, nt, uw, TM), BF16)],
        compiler_params=_params("arbitrary", "arbitrary"),
        name="inproj",
    )(x, mods, g, wt, qg, kg, rope)


def _attn_kernel(q_ref, k_ref, v_ref, kmax_ref, o_ref, acc_ref, m_ref):
    nk = k_ref.shape[2]
    tq = q_ref.shape[-1]
    nq = KV_REP * tq
    q = jnp.concatenate([q_ref[0, r] for r in range(KV_REP)], axis=1)
    pad = jnp.zeros((K_COLS - HEAD_DIM - 1, nq), F32)

    def write_out():
        acc = acc_ref[...]
        out = acc[0:HEAD_DIM] / acc[HEAD_DIM:HEAD_DIM + 1]
        for r in range(KV_REP):
            for t in range(tq // TM):
                o_ref[0, t, r * HEAD_DIM:(r + 1) * HEAD_DIM, :] = (
                    out[:, r * tq + t * TM:r * tq + (t + 1) * TM].astype(o_ref.dtype))

    def run(chunk):
        def group(i, carry):
            for t in range(ATTN_UNROLL):
                chunk(ATTN_UNROLL * i + t)
            return carry

        lax.fori_loop(0, nk // ATTN_UNROLL, group, 0)
        for j in range(nk - nk % ATTN_UNROLL, nk):
            chunk(j)

    qf = q.astype(F32)
    qnorm = jnp.sqrt(jnp.sum(qf * qf, axis=0, keepdims=True))
    bound = qnorm * (kmax_ref[0, 0][:, 0:1] * SCORE_BOUND_SLACK)
    q_shift = jnp.concatenate([q, jnp.concatenate([-bound, pad], axis=0).astype(BF16)], axis=0)
    acc_ref[...] = jnp.zeros_like(acc_ref)

    def chunk_shifted(j):
        s = jnp.dot(k_ref[0, 0, j], q_shift, preferred_element_type=F32)
        p = jnp.exp2(s).astype(BF16)
        acc_ref[...] += jnp.dot(v_ref[0, 0, j], p, preferred_element_type=F32)

    run(chunk_shifted)
    denom_ok = jnp.min(acc_ref[HEAD_DIM:HEAD_DIM + 1, :]) >= UNDERFLOW_GUARD

    @pl.when(denom_ok)
    def _():
        write_out()

    @pl.when(jnp.logical_not(denom_ok))
    def _():
        q_plain = jnp.concatenate([q, jnp.zeros((K_COLS - HEAD_DIM, nq), BF16)], axis=0)
        acc_ref[...] = jnp.zeros_like(acc_ref)
        m_ref[...] = jnp.full_like(m_ref, -jnp.inf)

        def chunk_online(j):
            s = jnp.dot(k_ref[0, 0, j], q_plain, preferred_element_type=F32)
            m = m_ref[...]
            m_new = jnp.maximum(m, jnp.max(s, axis=0, keepdims=True))
            alpha = jnp.exp2(m - m_new)
            p = jnp.exp2(s - m_new).astype(BF16)
            acc_ref[...] = alpha * acc_ref[...] + jnp.dot(v_ref[0, 0, j], p, preferred_element_type=F32)
            m_ref[...] = m_new

        run(chunk_online)
        write_out()


def _attention(q, k, v):
    b, _, _, l = q.shape
    nk = k.shape[2]
    nt = l // TM
    tq = min(ATTN_TQ, l)
    kf = k[..., :HEAD_DIM].astype(F32)
    kmax = jnp.sqrt(jnp.max(jnp.sum(kf * kf, axis=-1), axis=(2, 3)))
    kmax = jnp.broadcast_to(kmax[:, :, None, None], (b, N_KV_HEADS, 1, 128))
    return pl.pallas_call(
        _attn_kernel,
        grid=(b, N_KV_HEADS, l // tq),
        in_specs=[pl.BlockSpec((1, KV_REP, HEAD_DIM, tq), lambda bi, g, i: (bi, g, 0, i)),
                  pl.BlockSpec((1, 1, nk, TM, K_COLS), lambda bi, g, i: (bi, g, 0, 0, 0)),
                  pl.BlockSpec((1, 1, nk, V_ROWS, TM), lambda bi, g, i: (bi, g, 0, 0, 0)),
                  pl.BlockSpec((1, 1, 1, 128), lambda bi, g, i: (bi, g, 0, 0))],
        out_specs=pl.BlockSpec((1, tq // TM, KV_REP * HEAD_DIM, TM), lambda bi, g, i: (bi, i, g, 0)),
        out_shape=jax.ShapeDtypeStruct((b, nt, ATTN_WIDTH, TM), BF16),
        scratch_shapes=[pltpu.VMEM((V_ROWS, KV_REP * tq), F32), pltpu.VMEM((1, KV_REP * tq), F32)],
        compiler_params=_params("arbitrary", "arbitrary", "arbitrary"),
        name="attention",
    )(q, k, v, kmax)


def _short_conv_tiles(u_ref, w, bias):
    nt, c, tm = u_ref.shape[1:]
    lane = lax.broadcasted_iota(I32, (c, tm), 1)
    tiles = [u_ref[0, i].astype(F32) for i in range(nt)]
    prev = [pltpu.roll(t, 1, 1) for t in tiles]
    nxt = [pltpu.roll(t, tm - 1, 1) for t in tiles]
    zero = jnp.zeros((c, tm), F32)
    out = []
    for i in range(nt):
        up = jnp.where(lane == 0, prev[i - 1] if i > 0 else zero, prev[i])
        un = jnp.where(lane == tm - 1, nxt[i + 1] if i < nt - 1 else zero, nxt[i])
        out.append(bias + up * w[:, 0:1] + tiles[i] * w[:, 1:2] + un * w[:, 2:3])
    return out


def _sconv_kernel(u0_ref, u1_ref, u2_ref, w_ref, b_ref, g_ref, x0_ref):
    x0 = _short_conv_tiles(u0_ref, w_ref[0], b_ref[0])
    x1 = _short_conv_tiles(u1_ref, w_ref[1], b_ref[1])
    v = _short_conv_tiles(u2_ref, w_ref[2], b_ref[2])
    for i in range(len(x0)):
        g_ref[0, i] = (v[i] * x1[i]).astype(g_ref.dtype)
        x0_ref[0, i] = x0[i].astype(x0_ref.dtype)


def _short_conv_gate(u, cw, cb):
    b, nt, _, _ = u.shape
    w = HYENA_WIDTH
    nc = w // CONV_CB
    uspec = lambda grp: pl.BlockSpec((1, nt, CONV_CB, TM), lambda bi, c: (bi, 0, grp * nc + c, 0))
    ospec = pl.BlockSpec((1, nt, CONV_CB, TM), lambda bi, c: (bi, 0, c, 0))
    return pl.pallas_call(
        _sconv_kernel,
        grid=(b, nc),
        in_specs=[uspec(0), uspec(1), uspec(2),
                  pl.BlockSpec((3, CONV_CB, SHORT_CONV), lambda bi, c: (0, c, 0)),
                  pl.BlockSpec((3, CONV_CB, 1), lambda bi, c: (0, c, 0))],
        out_specs=[ospec, ospec],
        out_shape=[jax.ShapeDtypeStruct((b, nt, w, TM), BF16)] * 2,
        compiler_params=_params("arbitrary", "arbitrary"),
        name="short_conv",
    )(u, u, u, cw, cb)


def _dft_consts(nt):
    n1 = 2 * nt
    n = n1 * TM
    two_pi = 2.0 * math.pi

    def cs(prod, mod):
        ang = (prod % mod).astype(F32) * (two_pi / mod)
        return jnp.cos(ang), jnp.sin(ang)

    f1 = jnp.arange(n1, dtype=I32)
    t1 = jnp.arange(nt, dtype=I32)
    t2 = jnp.arange(TM, dtype=I32)
    c, s = cs(f1[:, None] * t1[None, :], n1)
    fwd1 = jnp.concatenate([c, -s], axis=0).astype(BF16)
    c, s = cs(f1[:, None] * t2[None, :], n)
    tw = jnp.stack([c, -s])
    c, s = cs(t2[:, None] * t2[None, :], TM)
    fwd2 = jnp.stack([jnp.concatenate([c, -s], axis=1),
                      jnp.concatenate([s, c], axis=1)]).astype(BF16)
    inv2 = jnp.stack([jnp.concatenate([c, s], axis=1),
                      jnp.concatenate([-s, c], axis=1)]).astype(BF16)
    c, s = cs(t1[:, None] * f1[None, :], n1)
    inv1 = (jnp.stack([c, -s]) * (1.0 / n)).astype(BF16)
    return fwd1, tw, fwd2, inv2, inv1


def _fft_fwd(g, fwd1, tw, fwd2, cb):
    n1 = fwd1.shape[0] // 2
    a = jnp.dot(fwd1, g, preferred_element_type=F32)
    are = jnp.concatenate([a[:n1, c * TM:(c + 1) * TM] for c in range(cb)], axis=0)
    aim = jnp.concatenate([a[n1:, c * TM:(c + 1) * TM] for c in range(cb)], axis=0)
    are = are.reshape(cb, n1, TM)
    aim = aim.reshape(cb, n1, TM)
    twr, twi = tw[0][None], tw[1][None]
    pr = (are * twr - aim * twi).reshape(cb * n1, TM).astype(BF16)
    pi = (are * twi + aim * twr).reshape(cb * n1, TM).astype(BF16)
    return (jnp.dot(pr, fwd2[0], preferred_element_type=F32)
            + jnp.dot(pi, fwd2[1], preferred_element_type=F32))


def _fftconv_kernel(g_ref, x0_ref, k_ref, fb_ref, fwd1_ref, tw_ref, fwd2_ref, inv2_ref, inv1_ref, o_ref):
    cb = g_ref.shape[-1] // TM
    n1 = fwd1_ref.shape[0] // 2
    g = g_ref[0]
    tw = tw_ref[...]
    x = _fft_fwd(g, fwd1_ref[...], tw, fwd2_ref[...], cb)
    kf = k_ref[...]
    xr, xi = x[:, :TM], x[:, TM:]
    kr, ki = kf[:, :TM], kf[:, TM:]
    yr = (xr * kr - xi * ki).astype(BF16)
    yi = (xr * ki + xi * kr).astype(BF16)
    bc = (jnp.dot(yr, inv2_ref[0], preferred_element_type=F32)
          + jnp.dot(yi, inv2_ref[1], preferred_element_type=F32))
    br = bc[:, :TM].reshape(cb, n1, TM)
    bi = bc[:, TM:].reshape(cb, n1, TM)
    twr, twi = tw[0][None], tw[1][None]
    pr = br * twr + bi * twi
    pi = bi * twr - br * twi
    prl = jnp.concatenate([pr[c] for c in range(cb)], axis=1).astype(BF16)
    pil = jnp.concatenate([pi[c] for c in range(cb)], axis=1).astype(BF16)
    y = (jnp.dot(inv1_ref[0], prl, preferred_element_type=F32)
         + jnp.dot(inv1_ref[1], pil, preferred_element_type=F32))
    gf = g.astype(F32)
    o_ref[0] = (x0_ref[0].astype(F32) * (y + fb_ref[...] * gf)).astype(o_ref.dtype)


def _fft_conv(g, x0, kf, fb, consts):
    b, nt, w, _ = g.shape
    n1 = 2 * nt
    cb = FFT_CB
    g2 = g.reshape(b, nt, w * TM)
    x2 = x0.reshape(b, nt, w * TM)
    fwd1, tw, fwd2, inv2, inv1 = consts
    dspec = pl.BlockSpec((1, nt, cb * TM), lambda c, bi: (bi, 0, c))
    full = lambda a: pl.BlockSpec(a.shape, lambda c, bi: (0,) * a.ndim)
    out = pl.pallas_call(
        _fftconv_kernel,
        grid=(w // cb, b),
        in_specs=[dspec, dspec,
                  pl.BlockSpec((cb * n1, 2 * TM), lambda c, bi: (c, 0)),
                  pl.BlockSpec((1, cb * TM), lambda c, bi: (0, c)),
                  full(fwd1), full(tw), full(fwd2), full(inv2), full(inv1)],
        out_specs=dspec,
        out_shape=jax.ShapeDtypeStruct((b, nt, w * TM), BF16),
        compiler_params=_params("arbitrary", "arbitrary"),
        name="long_conv",
    )(g2, x2, kf, fb, fwd1, tw, fwd2, inv2, inv1)
    return out.reshape(b, nt, w, TM)


def _spectrum_kernel(hf_ref, hb_ref, fwd1_ref, tw_ref, fwd2_ref, k_ref):
    cb = hf_ref.shape[-1] // TM
    tw = tw_ref[...]
    xf = _fft_fwd(hf_ref[...].astype(BF16), fwd1_ref[...], tw, fwd2_ref[...], cb)
    xb = _fft_fwd(hb_ref[...].astype(BF16), fwd1_ref[...], tw, fwd2_ref[...], cb)
    k_ref[...] = jnp.concatenate([xf[:, :TM] + xb[:, :TM], xf[:, TM:] - xb[:, TM:]], axis=1)


def _filter_spectrum(hf, hb, consts):
    nt, w, _ = hf.shape
    n1 = 2 * nt
    cb = FFT_CB
    fwd1, tw, fwd2, _, _ = consts
    hspec = pl.BlockSpec((nt, cb * TM), lambda c: (0, c))
    full = lambda a: pl.BlockSpec(a.shape, lambda c: (0,) * a.ndim)
    return pl.pallas_call(
        _spectrum_kernel,
        grid=(w // cb,),
        in_specs=[hspec, hspec, full(fwd1), full(tw), full(fwd2)],
        out_specs=pl.BlockSpec((cb * n1, 2 * TM), lambda c: (c, 0)),
        out_shape=jax.ShapeDtypeStruct((w * n1, 2 * TM), F32),
        compiler_params=_params("arbitrary"),
        name="filter_spectrum",
    )(hf.reshape(nt, w * TM), hb.reshape(nt, w * TM), fwd1, tw, fwd2)


def _filter_kernel(z_ref, t_ref, w1_ref, b1_ref, w2_ref, b2_ref, fr_ref, w3f_ref, w3b_ref, dl_ref,
                   hf_ref, hb_ref):
    nt = hf_ref.shape[0]
    dot = functools.partial(jnp.dot, preferred_element_type=F32, precision=HIGHEST)
    fr = fr_ref[...]
    h = jnp.sin(fr * (dot(w1_ref[...], z_ref[...]) + b1_ref[...]))
    h = jnp.sin(fr * (dot(w2_ref[...], h) + b2_ref[...]))
    decay = jnp.exp(-t_ref[...] * jnp.abs(dl_ref[...]))
    hf = dot(w3f_ref[...], h) * decay
    hb = dot(w3b_ref[...], h) * decay
    norm = (jnp.sum(jnp.abs(hf), axis=1, keepdims=True)
            + jnp.sum(jnp.abs(hb), axis=1, keepdims=True))
    hf = hf / norm
    hb = hb / norm
    lane = lax.broadcasted_iota(I32, hb.shape, 1)
    hb = jnp.where(lane == 0, 0.0, hb)
    for i in range(nt):
        hf_ref[i] = hf[:, i * TM:(i + 1) * TM]
        hb_ref[i] = hb[:, i * TM:(i + 1) * TM]


def _implicit_filters(l, fw1, fb1, fw2, fb2, fw3, ffreq):
    nt = l // TM
    w = HYENA_WIDTH
    t = jnp.linspace(0.0, 1.0, l, dtype=F32)[None, :]
    bands = (FILTER_EMB_DIM - 1) // 2
    wv = 2.0 * math.pi * jnp.arange(l, dtype=F32)[None, :] / l
    f = jnp.linspace(1e-4, bands - 1, bands, dtype=F32)[:, None]
    z = jnp.concatenate([t, jnp.cos(f * wv), -jnp.sin(f * wv)], axis=0)
    min_decay = math.log(DECAY_TARGET) / SLOW_DECAY_PCT
    max_decay = math.log(DECAY_TARGET) / FAST_DECAY_PCT
    deltas = jnp.linspace(min_decay, max_decay, w, dtype=F32)[:, None]
    w3t = fw3.T
    nc = w // FILT_CB
    col = lambda a: a.reshape(-1, 1)
    full = lambda a: pl.BlockSpec(a.shape, lambda c: (0,) * a.ndim)
    args = (z, t, fw1.T, col(fb1), fw2.T, col(fb2), col(ffreq))
    ospec = pl.BlockSpec((nt, FILT_CB, TM), lambda c: (0, c, 0))
    return pl.pallas_call(
        _filter_kernel,
        grid=(nc,),
        in_specs=[full(a) for a in args] + [
            pl.BlockSpec((FILT_CB, FILTER_HIDDEN), lambda c: (c, 0)),
            pl.BlockSpec((FILT_CB, FILTER_HIDDEN), lambda c: (nc + c, 0)),
            pl.BlockSpec((FILT_CB, 1), lambda c: (c, 0))],
        out_specs=[ospec, ospec],
        out_shape=[jax.ShapeDtypeStruct((nt, w, TM), F32)] * 2,
        compiler_params=_params("arbitrary"),
        name="implicit_filter",
    )(*args, w3t, w3t, deltas)


def _dense_dft_consts():
    n = 2 * TM
    two_pi = 2.0 * math.pi
    t = jnp.arange(TM, dtype=I32)
    f = jnp.arange(n, dtype=I32)
    ang = ((t[:, None] * f[None, :]) % n).astype(F32) * (two_pi / n)
    fwd = jnp.concatenate([jnp.cos(ang), -jnp.sin(ang)], axis=1).astype(BF16)
    inv = (jnp.stack([jnp.cos(ang).T, -jnp.sin(ang).T]) * (1.0 / n)).astype(BF16)
    return fwd, inv


def _hyena_tile_kernel(u_ref, w_ref, b_ref, hf_ref, hb_ref, fb_ref, fwd_ref, inv_ref, o_ref):
    w = HYENA_WIDTH
    n = 2 * TM
    x0 = _short_conv_tiles(u_ref.at[:, :, 0:w], w_ref[0], b_ref[0])[0]
    x1 = _short_conv_tiles(u_ref.at[:, :, w:2 * w], w_ref[1], b_ref[1])[0]
    v = _short_conv_tiles(u_ref.at[:, :, 2 * w:3 * w], w_ref[2], b_ref[2])[0]
    g = (v * x1).astype(BF16)
    fwd = fwd_ref[...]
    dot = functools.partial(jnp.dot, preferred_element_type=F32)
    kf = dot(hf_ref[0].astype(BF16), fwd)
    kb = dot(hb_ref[0].astype(BF16), fwd)
    kr = kf[:, :n] + kb[:, :n]
    ki = kf[:, n:] - kb[:, n:]
    x = dot(g, fwd)
    xr, xi = x[:, :n], x[:, n:]
    yr = (xr * kr - xi * ki).astype(BF16)
    yi = (xr * ki + xi * kr).astype(BF16)
    y = dot(yr, inv_ref[0]) + dot(yi, inv_ref[1])
    o_ref[0, 0] = (x0 * (y + fb_ref[...] * g.astype(F32))).astype(o_ref.dtype)


def _hyena_one_tile(u, cw, cb, hf, hb, fbias, consts):
    b = u.shape[0]
    w = HYENA_WIDTH
    fwd, inv = consts
    full = lambda a: pl.BlockSpec(a.shape, lambda bi: (0,) * a.ndim)
    return pl.pallas_call(
        _hyena_tile_kernel,
        grid=(b,),
        in_specs=[pl.BlockSpec((1, 1, 3 * w, TM), lambda bi: (bi, 0, 0, 0)),
                  full(cw), full(cb), full(hf), full(hb), full(fbias), full(fwd), full(inv)],
        out_specs=pl.BlockSpec((1, 1, w, TM), lambda bi: (bi, 0, 0, 0)),
        out_shape=jax.ShapeDtypeStruct((b, 1, w, TM), BF16),
        compiler_params=_params("arbitrary"),
        name="hyena_context",
    )(u, cw, cb, hf, hb, fbias, fwd, inv)


def _top2_sum(a, b, c, d):
    hi1, lo1 = jnp.maximum(a, b), jnp.minimum(a, b)
    hi2, lo2 = jnp.maximum(c, d), jnp.minimum(c, d)
    return jnp.maximum(hi1, hi2) + jnp.maximum(jnp.minimum(hi1, hi2), jnp.maximum(lo1, lo2))


def _outproj_kernel(a_ref, y_ref, x_ref, mod_ref, ga_ref, gy_ref, wo_ref, n2_ref, rw_ref, rb_ref,
                    cnt_ref, tri_ref, xo_ref, row_ref, route_ref, cnto_ref, cnt_sc):
    first = jnp.logical_and(pl.program_id(0) == 0, pl.program_id(1) == 0)

    @pl.when(first)
    def _():
        cnt_sc[...] = cnt_ref[...]

    def group_norm(t, gain):
        t = t.astype(F32)
        return t * lax.rsqrt(jnp.mean(t * t, axis=0, keepdims=True) + EPS) * gain

    mix = jnp.concatenate([group_norm(a_ref[0, 0], ga_ref[...]),
                           group_norm(y_ref[0, 0], gy_ref[...])], axis=0).astype(BF16)
    o = lax.dot_general(mix, wo_ref[...], (((0,), (0,)), ((), ())), preferred_element_type=F32)
    xn = x_ref[0] + mod_ref[0, 2:3, :] * o
    xo_ref[0] = xn
    h2 = _norm_mod(xn, n2_ref[...], mod_ref[0, 3:4, :], mod_ref[0, 4:5, :])

    tm = xn.shape[0]
    nt_dims = (((1,), (1,)), ((), ()))
    h_hi = h2.astype(BF16)
    h_lo = (h2 - h_hi.astype(F32)).astype(BF16)
    rw = rw_ref[...]
    part = lax.dot_general(rw, h_hi, nt_dims, preferred_element_type=F32)
    logits = (part[:N_EXPERTS] + part[N_EXPERTS:]
              + lax.dot_general(rw[:N_EXPERTS], h_lo, nt_dims, preferred_element_type=F32))
    score = jax.nn.sigmoid(logits)
    sel = score + rb_ref[...]
    srow = [sel[e:e + 1] for e in range(N_EXPERTS)]
    prow = [score[e:e + 1] for e in range(N_EXPERTS)]
    epg = EXPERTS_PER_GROUP
    gs = [_top2_sum(*srow[g * epg:(g + 1) * epg]) for g in range(N_EXPERT_GROUPS)]
    bg = jnp.zeros((1, tm), I32)
    best = gs[0]
    for g in range(1, N_EXPERT_GROUPS):
        upd = gs[g] > best
        bg = jnp.where(upd, g, bg)
        best = jnp.where(upd, gs[g], best)

    def pick_group(rows, j):
        out = rows[j]
        for g in range(1, N_EXPERT_GROUPS):
            out = jnp.where(bg == g, rows[g * epg + j], out)
        return out

    cand = [pick_group(srow, j) for j in range(epg)]
    cprob = [pick_group(prow, j) for j in range(epg)]
    i1 = jnp.zeros((1, tm), I32)
    v1, w1 = cand[0], cprob[0]
    for j in range(1, epg):
        upd = cand[j] > v1
        i1 = jnp.where(upd, j, i1)
        v1 = jnp.where(upd, cand[j], v1)
        w1 = jnp.where(upd, cprob[j], w1)
    i2 = jnp.zeros((1, tm), I32)
    v2 = jnp.full((1, tm), -jnp.inf, F32)
    w2 = jnp.zeros((1, tm), F32)
    for j in range(epg):
        upd = jnp.logical_and(i1 != j, cand[j] > v2)
        i2 = jnp.where(upd, j, i2)
        v2 = jnp.where(upd, cand[j], v2)
        w2 = jnp.where(upd, cprob[j], w2)
    den = w1 + w2
    w1, w2 = w1 / den, w2 / den
    first_lower = i1 < i2
    lo = jnp.where(first_lower, i1, i2)
    hi = jnp.where(first_lower, i2, i1)
    w_lo = jnp.where(first_lower, w1, w2)
    w_hi = jnp.where(first_lower, w2, w1)
    pair = jnp.where(lo == 0, hi - 1, jnp.where(lo == 1, hi + 1, PAIRS_PER_GROUP - 1))
    cls = bg * PAIRS_PER_GROUP + pair

    cio = lax.broadcasted_iota(I32, (N_CLASS_ROWS, tm), 0)
    onehot = jnp.where(cio == cls, 1.0, 0.0)
    cum = jnp.dot(onehot.astype(BF16), tri_ref[...], preferred_element_type=F32)
    base = cnt_sc[...]
    rank = jnp.sum(onehot * (base + cum), axis=0, keepdims=True)
    route_ref[0, 0] = jnp.concatenate([cls, rank.astype(I32)], axis=0)
    new = base + jnp.sum(onehot, axis=1, keepdims=True)
    cnt_sc[...] = new
    cnto_ref[...] = new

    half = h2.shape[1] // 2
    wrows = jnp.concatenate([w_lo, w_hi, jnp.zeros((ROW_EXTRA - 2, tm), F32)], axis=0)
    row_ref[0] = jnp.concatenate([_pack_bf16_pair(h2[:, :half], h2[:, half:]),
                                  lax.bitcast_convert_type(wrows.T, U32)], axis=1)


def _outproj_route(a, y, x, mods, mod_row, ga, gy, wo, n2g, rwt, rb, cnt, tri):
    b, l, d = x.shape
    nt = l // TM
    if mod_row is None:
        mod_map = lambda bi, i: (bi, 0, 0)
    else:
        mod_map = lambda bi, i: (mod_row, 0, 0)
    full = lambda arr: pl.BlockSpec(arr.shape, lambda bi, i: (0,) * arr.ndim)
    tile = pl.BlockSpec((1, 1, ATTN_WIDTH, TM), lambda bi, i: (bi, i, 0, 0))
    xspec = pl.BlockSpec((1, TM, d), lambda bi, i: (bi, i, 0))
    rspec = pl.BlockSpec((1, 1, 2, TM), lambda bi, i: (bi, i, 0, 0))
    row_w = d // 2 + ROW_EXTRA
    return pl.pallas_call(
        _outproj_kernel,
        grid=(b, nt),
        in_specs=[tile, tile, xspec, pl.BlockSpec((1, N_MOD, d), mod_map),
                  full(ga), full(gy), full(wo), full(n2g), full(rwt), full(rb), full(cnt), full(tri)],
        out_specs=[xspec, pl.BlockSpec((1, TM, row_w), lambda bi, i: (bi, i, 0)), rspec, full(cnt)],
        out_shape=[jax.ShapeDtypeStruct((b, l, d), F32),
                   jax.ShapeDtypeStruct((b, l, row_w), U32),
                   jax.ShapeDtypeStruct((b, nt, 2, TM), I32),
                   jax.ShapeDtypeStruct(cnt.shape, F32)],
        scratch_shapes=[pltpu.VMEM(cnt.shape, F32)],
        compiler_params=_params("arbitrary", "arbitrary"),
        name="outproj_route",
    )(a, y, x, mods, ga, gy, wo, n2g, rwt, rb, cnt, tri)


def _dispatch_kernel(slot_ref, h_ref, xs_in_ref, xs_ref, sem):
    del xs_in_ref
    tm = h_ref.shape[0]

    def copy(r, s):
        return pltpu.make_async_copy(h_ref.at[pl.ds(r, 1)], xs_ref.at[pl.ds(s, 1)], sem)

    def issue(r, carry):
        copy(r, slot_ref[0, 0, r]).start()
        return carry

    lax.fori_loop(0, tm, issue, 0, unroll=8)

    def drain(r, carry):
        copy(0, 0).wait()
        return carry

    lax.fori_loop(0, tm, drain, 0, unroll=8)


def _dispatch(slots, h2, xs):
    t, d = h2.shape
    return pl.pallas_call(
        _dispatch_kernel,
        grid=(t // TM,),
        in_specs=[pl.BlockSpec((1, 1, TM), lambda i: (i, 0, 0), memory_space=pltpu.SMEM),
                  pl.BlockSpec((TM, d), lambda i: (i, 0)),
                  pl.BlockSpec(memory_space=pl.ANY)],
        out_specs=pl.BlockSpec(memory_space=pl.ANY),
        out_shape=jax.ShapeDtypeStruct(xs.shape, xs.dtype),
        scratch_shapes=[pltpu.SemaphoreType.DMA(())],
        input_output_aliases={2: 0},
        compiler_params=_params("arbitrary"),
        name="moe_dispatch",
    )(slots, h2, xs)


def _ffn_kernel(ea_ref, eb_ref, nt_ref, xs_ref, wga_ref, wua_ref, wda_ref, wgb_ref, wub_ref, wdb_ref, ys_ref):
    del ea_ref, eb_ref
    live = pl.program_id(0) < nt_ref[0]
    half = ys_ref.shape[1]

    @pl.when(live)
    def _():
        words = xs_ref[...]
        lo, hi = _unpack_bf16_pair(words[:, :half])
        lo, hi = lo.astype(BF16), hi.astype(BF16)
        wts = lax.bitcast_convert_type(words[:, half:half + 2], F32)

        def expert(wg_ref, wu_ref, wd_ref):
            dot = functools.partial(jnp.dot, preferred_element_type=F32)
            g = dot(lo, wg_ref[0, :half, :]) + dot(hi, wg_ref[0, half:, :])
            u = dot(lo, wu_ref[0, :half, :]) + dot(hi, wu_ref[0, half:, :])
            a = (g * jax.nn.sigmoid(g)) * u
            return dot(a.astype(BF16), wd_ref[0])

        y = (wts[:, 0:1] * expert(wga_ref, wua_ref, wda_ref)
             + wts[:, 1:2] * expert(wgb_ref, wub_ref, wdb_ref))
        ys_ref[...] = _pack_bf16_pair(y[:, :half], y[:, half:])

    @pl.when(jnp.logical_not(live))
    def _():
        ys_ref[...] = jnp.zeros_like(ys_ref)


def _expert_ffn(tile_ea, tile_eb, n_tiles, xs, wg, wu, wd):
    nslot, row_w = xs.shape
    d, f = wg.shape[1:]
    ntile = nslot // FFN_TILE
    row = lambda i, ea, eb, nt: (jnp.minimum(i, nt[0] - 1), 0)
    amap = lambda i, ea, eb, nt: (ea[jnp.minimum(i, nt[0] - 1)], 0, 0)
    bmap = lambda i, ea, eb, nt: (eb[jnp.minimum(i, nt[0] - 1)], 0, 0)
    return pl.pallas_call(
        _ffn_kernel,
        grid_spec=pltpu.PrefetchScalarGridSpec(
            num_scalar_prefetch=3,
            grid=(ntile,),
            in_specs=[pl.BlockSpec((FFN_TILE, row_w), row),
                      pl.BlockSpec((1, d, f), amap), pl.BlockSpec((1, d, f), amap), pl.BlockSpec((1, f, d), amap),
                      pl.BlockSpec((1, d, f), bmap), pl.BlockSpec((1, d, f), bmap), pl.BlockSpec((1, f, d), bmap)],
            out_specs=pl.BlockSpec((FFN_TILE, d // 2), lambda i, ea, eb, nt: (i, 0))),
        out_shape=jax.ShapeDtypeStruct((nslot, d // 2), U32),
        compiler_params=_params("arbitrary"),
        name="moe_experts",
    )(tile_ea, tile_eb, n_tiles, xs, wg, wu, wd, wg, wu, wd)


def _combine_kernel(slot_ref, x_ref, mod_ref, ys_ref, o_ref, buf, sem):
    tm = x_ref.shape[1]

    def copy(r, s):
        return pltpu.make_async_copy(ys_ref.at[pl.ds(s, 1)], buf.at[pl.ds(r, 1)], sem)

    def issue(r, carry):
        copy(r, slot_ref[0, 0, r]).start()
        return carry

    lax.fori_loop(0, tm, issue, 0, unroll=8)

    def drain(r, carry):
        copy(0, 0).wait()
        return carry

    lax.fori_loop(0, tm, drain, 0, unroll=8)
    lo, hi = _unpack_bf16_pair(buf[...])
    y = jnp.concatenate([lo, hi], axis=1)
    o_ref[0] = x_ref[0] + mod_ref[0, 5:6, :] * y


def _combine(slots, x, mods, mod_row, ys):
    b, l, d = x.shape
    nt = l // TM
    if mod_row is None:
        mod_map = lambda bi, i: (bi, 0, 0)
    else:
        mod_map = lambda bi, i: (mod_row, 0, 0)
    xspec = pl.BlockSpec((1, TM, d), lambda bi, i: (bi, i, 0))
    return pl.pallas_call(
        _combine_kernel,
        grid=(b, nt),
        in_specs=[pl.BlockSpec((1, 1, TM), lambda bi, i: (bi * nt + i, 0, 0), memory_space=pltpu.SMEM),
                  xspec,
                  pl.BlockSpec((1, N_MOD, d), mod_map),
                  pl.BlockSpec(memory_space=pl.ANY)],
        out_specs=xspec,
        out_shape=jax.ShapeDtypeStruct((b, l, d), F32),
        scratch_shapes=[pltpu.VMEM((TM, d // 2), U32), pltpu.SemaphoreType.DMA(())],
        compiler_params=_params("arbitrary", "arbitrary"),
        name="moe_combine",
    )(slots, x, mods, ys)


def _moe(streams, wg, wu, wd):
    counts = streams[-1]["counts"].reshape(-1).astype(I32)
    padded = ((counts + FFN_TILE - 1) // FFN_TILE) * FFN_TILE
    ends = jnp.cumsum(padded)
    offs = ends - padded
    total = sum(s["x"].shape[0] * s["x"].shape[1] for s in streams)
    ntile = total // FFN_TILE + N_CLASSES
    nslot = ntile * FFN_TILE
    tile_start = jnp.arange(ntile, dtype=I32) * FFN_TILE
    tile_class = jnp.minimum(jnp.sum((tile_start[:, None] >= ends[None, :N_CLASSES]).astype(I32), axis=1),
                             N_CLASSES - 1)
    group_base = (tile_class // PAIRS_PER_GROUP) * EXPERTS_PER_GROUP
    tile_ea = (group_base + jnp.asarray(PAIR_LO, I32)[tile_class % PAIRS_PER_GROUP]).astype(I32)
    tile_eb = (group_base + jnp.asarray(PAIR_HI, I32)[tile_class % PAIRS_PER_GROUP]).astype(I32)
    n_tiles = (ends[-1] // FFN_TILE).astype(I32).reshape(1)

    row_w = streams[0]["rows"].shape[-1]
    xs = jnp.zeros((nslot, row_w), U32)
    for s in streams:
        b, l, _ = s["x"].shape
        cls, rank = s["route"][:, :, 0], s["route"][:, :, 1]
        slot = rank
        for c in range(N_CLASSES):
            slot = slot + jnp.where(cls == c, offs[c], 0)
        s["slots"] = slot.reshape(b * (l // TM), 1, TM)
        xs = _dispatch(s["slots"], s["rows"].reshape(b * l, row_w), xs)
    ys = _expert_ffn(tile_ea, tile_eb, n_tiles, xs, wg, wu, wd)
    return [_combine(s["slots"], s["x"], s["mods"], s["mod_row"], ys) for s in streams]


def _rope_table(l):
    t = jnp.arange(l)
    row = (t // GRID_W).astype(F32)
    col = (t % GRID_W).astype(F32)
    inv_freq = ROPE_THETA ** (-jnp.arange(ROPE_QUARTER, dtype=F32) / ROPE_QUARTER)
    ar = inv_freq[:, None] * row[None, :]
    ac = inv_freq[:, None] * col[None, :]
    return jnp.concatenate([jnp.cos(ar), jnp.sin(ar), jnp.cos(ac), jnp.sin(ac)], axis=0)


def _identity_rope_table(l):
    one = jnp.ones((ROPE_QUARTER, l), F32)
    zero = jnp.zeros((ROPE_QUARTER, l), F32)
    return jnp.concatenate([one, zero, one, zero], axis=0)


def kernel(x, c, ctx, c_ctx, w_mod, b_mod, norm1_g, w_in, q_norm_g, k_norm_g, conv_w, conv_b, filt_w1, filt_b1, filt_w2, filt_b2, filt_w3, filt_freq, filt_bias, attn_out_g, hyena_out_g, w_out, norm2_g, router_w, router_bias, expert_w_gate, expert_w_up, expert_w_down):
    depth = w_mod.shape[0]
    b, l, d = x.shape
    lc = ctx.shape[1]
    assert l % TM == 0 and lc == TM and d == D_MODEL
    nt = l // TM
    w = HYENA_WIDTH

    mods = _modulations(c, c_ctx, w_mod, b_mod)
    rope_lat = _rope_table(l)
    rope_ctx = _identity_rope_table(lc)
    dft = _dft_consts(nt)
    dft_ctx = _dense_dft_consts()
    tri = (jnp.arange(TM)[:, None] < jnp.arange(TM)[None, :]).astype(BF16)
    rwt = router_w.T
    rw_hi = rwt.astype(BF16)
    rwt = jnp.concatenate([rw_hi, (rwt - rw_hi.astype(F32)).astype(BF16)], axis=0)
    rb = router_bias.reshape(-1, 1)
    zero_cnt = jnp.zeros((N_CLASS_ROWS, 1), F32)
    col = lambda a: a.reshape(-1, 1)

    for li in range(depth):
        last = li == depth - 1
        m = mods[li]
        wt = w_in[li].T.astype(BF16)
        wo = w_out[li].astype(BF16)
        g1n = norm1_g[li].reshape(1, d)
        g2n = norm2_g[li].reshape(1, d)
        qg, kg = col(q_norm_g[li]), col(k_norm_g[li])
        cw = conv_w[li].T.reshape(3, w, SHORT_CONV)
        cb = conv_b[li].reshape(3, w, 1)
        ga, gy = col(attn_out_g[li]), col(hyena_out_g[li])
        fargs = (filt_w1[li], filt_b1[li], filt_w2[li], filt_b2[li], filt_w3[li], filt_freq[li])

        q, k, v, u = _inproj(x, m, None, g1n, wt, qg, kg, rope_lat)
        qc, kc, vc, uc = _inproj(ctx, m, b, g1n, wt, qg, kg, rope_ctx)
        a = _attention(q, jnp.concatenate([kc, k], axis=2), jnp.concatenate([vc, v], axis=2))
        hf, hb = _implicit_filters(l, *fargs)
        kf = _filter_spectrum(hf, hb, dft)
        g, x0 = _short_conv_gate(u, cw, cb)
        fb_lanes = jnp.repeat(filt_bias[li], TM).reshape(1, w * TM)
        y = _fft_conv(g, x0, kf, fb_lanes, dft)
        lat = dict(zip(("x", "rows", "route", "counts"),
                       _outproj_route(a, y, x, m, None, ga, gy, wo, g2n, rwt, rb, zero_cnt, tri)))
        lat.update(mods=m, mod_row=None)
        streams = [lat]

        if not last:
            ac = _attention(qc, kc, vc)
            hfc, hbc = _implicit_filters(lc, *fargs)
            yc = _hyena_one_tile(uc, cw, cb, hfc, hbc, col(filt_bias[li]), dft_ctx)
            cs = dict(zip(("x", "rows", "route", "counts"),
                          _outproj_route(ac, yc, ctx, m, b, ga, gy, wo, g2n, rwt, rb, lat["counts"], tri)))
            cs.update(mods=m, mod_row=b)
            streams.append(cs)

        outs = _moe(streams, expert_w_gate[li].astype(BF16), expert_w_up[li].astype(BF16),
                    expert_w_down[li].astype(BF16))
        x = outs[0]
        if not last:
            ctx = outs[1]
    return x
```

```python
import functools
import math

import jax
import jax.numpy as jnp
from jax import lax
from jax.experimental import pallas as pl
from jax.experimental.pallas import tpu as pltpu

F32 = jnp.float32
BF16 = jnp.bfloat16
I32 = jnp.int32
HIGHEST = lax.Precision.HIGHEST

D_MODEL = 1024
N_MOD = 6
EPS = 1e-6
N_Q_HEADS = 8
N_KV_HEADS = 2
HEAD_DIM = 64
KV_REP = N_Q_HEADS // N_KV_HEADS
ATTN_WIDTH = N_Q_HEADS * HEAD_DIM
KV_WIDTH = N_KV_HEADS * HEAD_DIM
ATTN_SCALE = HEAD_DIM ** -0.5
LOG2_E = math.log2(math.e)
V_ROWS = HEAD_DIM + 16
K_COLS = HEAD_DIM + 16
SCORE_BOUND_SLACK = 1.0 + 2.0 ** -6
UNDERFLOW_GUARD = 2.0 ** -80
GRID_W = 64
ROPE_THETA = 10000.0
ROPE_QUARTER = HEAD_DIM // 4
HYENA_WIDTH = D_MODEL - ATTN_WIDTH
SHORT_CONV = 3
FILTER_EMB_DIM = 33
FILTER_HIDDEN = 64
DECAY_TARGET = 1e-2
FAST_DECAY_PCT = 0.3
SLOW_DECAY_PCT = 1.5
N_EXPERTS = 16
N_EXPERT_GROUPS = 4
EXPERTS_PER_GROUP = N_EXPERTS // N_EXPERT_GROUPS
D_FF_EXPERT = 512

TM = 256
ATTN_UNROLL = 8
ATTN_TQ = 1024
FFN_TILE = 256
CONV_CB = 128
FFT_CB = 32
FILT_CB = 128
VMEM_LIMIT = 56 * 1024 * 1024


U32 = jnp.uint32
PAIRS_PER_GROUP = EXPERTS_PER_GROUP * (EXPERTS_PER_GROUP - 1) // 2
N_CLASSES = N_EXPERT_GROUPS * PAIRS_PER_GROUP
N_CLASS_ROWS = -(-N_CLASSES // 8) * 8
PAIR_LO = tuple(a for a in range(EXPERTS_PER_GROUP) for b in range(a + 1, EXPERTS_PER_GROUP))
PAIR_HI = tuple(b for a in range(EXPERTS_PER_GROUP) for b in range(a + 1, EXPERTS_PER_GROUP))
ROW_EXTRA = 128
MOVE_ROWS = 8
SORT_ROWS = -(-(TM + N_CLASSES * (MOVE_ROWS - 1)) // 16) * 16
META_LANES = 128
assert 3 * N_CLASSES <= META_LANES


def _pack_bf16_pair(lo, hi):
    lo_bits = lax.bitcast_convert_type(lo.astype(BF16).astype(F32), U32)
    hi_bits = lax.bitcast_convert_type(hi.astype(BF16).astype(F32), U32)
    return lax.shift_right_logical(lo_bits, U32(16)) | (hi_bits & U32(0xFFFF0000))


def _unpack_bf16_pair(words):
    lo = lax.bitcast_convert_type(lax.shift_left(words, U32(16)), F32)
    hi = lax.bitcast_convert_type(words & U32(0xFFFF0000), F32)
    return lo, hi


def _params(*sem):
    return pltpu.CompilerParams(dimension_semantics=tuple(sem), vmem_limit_bytes=VMEM_LIMIT)


def _norm_mod(x, g, shift, scale):
    y = x * lax.rsqrt(jnp.mean(x * x, axis=-1, keepdims=True) + EPS)
    return (y * g) * (1 + scale) + shift


def _mod_kernel(c_ref, w_ref, b_ref, o_ref):
    c = c_ref[...]
    s = c * jax.nn.sigmoid(c)
    o_ref[0] = jnp.dot(s, w_ref[0], preferred_element_type=F32, precision=HIGHEST) + b_ref[0]


def _modulations(c, c_ctx, w_mod, b_mod):
    depth, d, nmd = w_mod.shape
    b = c.shape[0]
    rows = -(-(b + 1) // 8) * 8
    c_all = jnp.zeros((rows, d), F32).at[:b].set(c).at[b].set(c_ctx)
    tn = nmd // 4
    out = pl.pallas_call(
        _mod_kernel,
        grid=(depth, nmd // tn),
        in_specs=[pl.BlockSpec((rows, d), lambda l, j: (0, 0)),
                  pl.BlockSpec((1, d, tn), lambda l, j: (l, 0, j)),
                  pl.BlockSpec((1, 1, tn), lambda l, j: (l, 0, j))],
        out_specs=pl.BlockSpec((1, rows, tn), lambda l, j: (l, 0, j)),
        out_shape=jax.ShapeDtypeStruct((depth, rows, nmd), F32),
        compiler_params=_params("arbitrary", "arbitrary"),
        name="modulation",
    )(c_all, w_mod, b_mod.reshape(depth, 1, nmd))
    return out.reshape(depth, rows, N_MOD, d)


def _inproj_kernel(x_ref, mod_ref, g_ref, wt_ref, qg_ref, kg_ref, rope_ref,
                   q_ref, k_ref, v_ref, u_ref):
    x = x_ref[0]
    tm = x.shape[0]
    h = _norm_mod(x, g_ref[...], mod_ref[0, 0:1, :], mod_ref[0, 1:2, :])
    pt = lax.dot_general(wt_ref[...], h.astype(BF16), (((1,), (1,)), ((), ())),
                         preferred_element_type=F32)
    rope = rope_ref[...]
    qd = ROPE_QUARTER
    cr, sr = rope[0:qd][None], rope[qd:2 * qd][None]
    cc, sc = rope[2 * qd:3 * qd][None], rope[3 * qd:4 * qd][None]

    def norm_rope(t, gain, nh):
        t = t.reshape(nh, HEAD_DIM, tm)
        t = t * lax.rsqrt(jnp.mean(t * t, axis=1, keepdims=True) + EPS) * gain[None]
        a, b = t[:, 0:qd], t[:, qd:2 * qd]
        c, d = t[:, 2 * qd:3 * qd], t[:, 3 * qd:4 * qd]
        return jnp.concatenate([a * cr - b * sr, b * cr + a * sr,
                                c * cc - d * sc, d * cc + c * sc], axis=1)

    q = norm_rope(pt[0:ATTN_WIDTH], qg_ref[...], N_Q_HEADS) * (ATTN_SCALE * LOG2_E)
    q_ref[0] = q.astype(BF16)
    k = norm_rope(pt[ATTN_WIDTH:ATTN_WIDTH + KV_WIDTH], kg_ref[...], N_KV_HEADS)
    kt = k.reshape(KV_WIDTH, tm).T
    for g in range(N_KV_HEADS):
        k_ref[0, g, 0, :, 0:HEAD_DIM] = kt[:, g * HEAD_DIM:(g + 1) * HEAD_DIM].astype(BF16)
    pad_col = lax.broadcasted_iota(I32, (N_KV_HEADS, tm, K_COLS - HEAD_DIM), 2)
    k_ref[0, :, 0, :, HEAD_DIM:K_COLS] = jnp.where(pad_col == 0, 1.0, 0.0).astype(BF16)
    v = pt[ATTN_WIDTH + KV_WIDTH:ATTN_WIDTH + 2 * KV_WIDTH]
    v_ref[0, :, 0, 0:HEAD_DIM] = v.reshape(N_KV_HEADS, HEAD_DIM, tm).astype(BF16)
    pad_row = lax.broadcasted_iota(I32, (N_KV_HEADS, V_ROWS - HEAD_DIM, tm), 1)
    v_ref[0, :, 0, HEAD_DIM:V_ROWS] = jnp.where(pad_row == 0, 1.0, 0.0).astype(BF16)
    u_ref[0, 0] = pt[ATTN_WIDTH + 2 * KV_WIDTH:].astype(BF16)


def _inproj(x, mods, mod_row, g, wt, qg, kg, rope):
    b, l, d = x.shape
    nt = l // TM
    p = wt.shape[0]
    uw = p - ATTN_WIDTH - 2 * KV_WIDTH
    if mod_row is None:
        mod_map = lambda bi, i: (bi, 0, 0)
    else:
        mod_map = lambda bi, i: (mod_row, 0, 0)
    return pl.pallas_call(
        _inproj_kernel,
        grid=(b, nt),
        in_specs=[pl.BlockSpec((1, TM, d), lambda bi, i: (bi, i, 0)),
                  pl.BlockSpec((1, N_MOD, d), mod_map),
                  pl.BlockSpec((1, d), lambda bi, i: (0, 0)),
                  pl.BlockSpec((p, d), lambda bi, i: (0, 0)),
                  pl.BlockSpec((HEAD_DIM, 1), lambda bi, i: (0, 0)),
                  pl.BlockSpec((HEAD_DIM, 1), lambda bi, i: (0, 0)),
                  pl.BlockSpec((HEAD_DIM, TM), lambda bi, i: (0, i))],
        out_specs=[pl.BlockSpec((1, N_Q_HEADS, HEAD_DIM, TM), lambda bi, i: (bi, 0, 0, i)),
                   pl.BlockSpec((1, N_KV_HEADS, 1, TM, K_COLS), lambda bi, i: (bi, 0, i, 0, 0)),
                   pl.BlockSpec((1, N_KV_HEADS, 1, V_ROWS, TM), lambda bi, i: (bi, 0, i, 0, 0)),
                   pl.BlockSpec((1, 1, uw, TM), lambda bi, i: (bi, i, 0, 0))],
        out_shape=[jax.ShapeDtypeStruct((b, N_Q_HEADS, HEAD_DIM, l), BF16),
                   jax.ShapeDtypeStruct((b, N_KV_HEADS, nt, TM, K_COLS), BF16),
                   jax.ShapeDtypeStruct((b, N_KV_HEADS, nt, V_ROWS, TM), BF16),
                   jax.ShapeDtypeStruct((b, nt, uw, TM), BF16)],
        compiler_params=_params("arbitrary", "arbitrary"),
        name="inproj",
    )(x, mods, g, wt, qg, kg, rope)


def _attn_kernel(q_ref, k_ref, v_ref, kmax_ref, o_ref, acc_ref, m_ref):
    nk = k_ref.shape[2]
    tq = q_ref.shape[-1]
    nq = KV_REP * tq
    q = jnp.concatenate([q_ref[0, r] for r in range(KV_REP)], axis=1)
    pad = jnp.zeros((K_COLS - HEAD_DIM - 1, nq), F32)

    def write_out():
        acc = acc_ref[...]
        out = acc[0:HEAD_DIM] / acc[HEAD_DIM:HEAD_DIM + 1]
        for r in range(KV_REP):
            for t in range(tq // TM):
                o_ref[0, t, r * HEAD_DIM:(r + 1) * HEAD_DIM, :] = (
                    out[:, r * tq + t * TM:r * tq + (t + 1) * TM].astype(o_ref.dtype))

    def run(chunk):
        def group(i, carry):
            for t in range(ATTN_UNROLL):
                chunk(ATTN_UNROLL * i + t)
            return carry

        lax.fori_loop(0, nk // ATTN_UNROLL, group, 0)
        for j in range(nk - nk % ATTN_UNROLL, nk):
            chunk(j)

    qf = q.astype(F32)
    qnorm = jnp.sqrt(jnp.sum(qf * qf, axis=0, keepdims=True))
    bound = qnorm * (kmax_ref[0, 0][:, 0:1] * SCORE_BOUND_SLACK)
    q_shift = jnp.concatenate([q, jnp.concatenate([-bound, pad], axis=0).astype(BF16)], axis=0)
    acc_ref[...] = jnp.zeros_like(acc_ref)

    def chunk_shifted(j):
        s = jnp.dot(k_ref[0, 0, j], q_shift, preferred_element_type=F32)
        p = jnp.exp2(s).astype(BF16)
        acc_ref[...] += jnp.dot(v_ref[0, 0, j], p, preferred_element_type=F32)

    run(chunk_shifted)
    denom_ok = jnp.min(acc_ref[HEAD_DIM:HEAD_DIM + 1, :]) >= UNDERFLOW_GUARD

    @pl.when(denom_ok)
    def _():
        write_out()

    @pl.when(jnp.logical_not(denom_ok))
    def _():
        q_plain = jnp.concatenate([q, jnp.zeros((K_COLS - HEAD_DIM, nq), BF16)], axis=0)
        acc_ref[...] = jnp.zeros_like(acc_ref)
        m_ref[...] = jnp.full_like(m_ref, -jnp.inf)

        def chunk_online(j):
            s = jnp.dot(k_ref[0, 0, j], q_plain, preferred_element_type=F32)
            m = m_ref[...]
            m_new = jnp.maximum(m, jnp.max(s, axis=0, keepdims=True))
            alpha = jnp.exp2(m - m_new)
            p = jnp.exp2(s - m_new).astype(BF16)
            acc_ref[...] = alpha * acc_ref[...] + jnp.dot(v_ref[0, 0, j], p, preferred_element_type=F32)
            m_ref[...] = m_new

        run(chunk_online)
        write_out()


def _attention(q, k, v):
    b, _, _, l = q.shape
    nk = k.shape[2]
    nt = l // TM
    tq = min(ATTN_TQ, l)
    kf = k[..., :HEAD_DIM].astype(F32)
    kmax = jnp.sqrt(jnp.max(jnp.sum(kf * kf, axis=-1), axis=(2, 3)))
    kmax = jnp.broadcast_to(kmax[:, :, None, None], (b, N_KV_HEADS, 1, 128))
    return pl.pallas_call(
        _attn_kernel,
        grid=(b, N_KV_HEADS, l // tq),
        in_specs=[pl.BlockSpec((1, KV_REP, HEAD_DIM, tq), lambda bi, g, i: (bi, g, 0, i)),
                  pl.BlockSpec((1, 1, nk, TM, K_COLS), lambda bi, g, i: (bi, g, 0, 0, 0)),
                  pl.BlockSpec((1, 1, nk, V_ROWS, TM), lambda bi, g, i: (bi, g, 0, 0, 0)),
                  pl.BlockSpec((1, 1, 1, 128), lambda bi, g, i: (bi, g, 0, 0))],
        out_specs=pl.BlockSpec((1, tq // TM, KV_REP * HEAD_DIM, TM), lambda bi, g, i: (bi, i, g, 0)),
        out_shape=jax.ShapeDtypeStruct((b, nt, ATTN_WIDTH, TM), BF16),
        scratch_shapes=[pltpu.VMEM((V_ROWS, KV_REP * tq), F32), pltpu.VMEM((1, KV_REP * tq), F32)],
        compiler_params=_params("arbitrary", "arbitrary", "arbitrary"),
        name="attention",
    )(q, k, v, kmax)


def _short_conv_tiles(u_ref, w, bias):
    nt, c, tm = u_ref.shape[1:]
    lane = lax.broadcasted_iota(I32, (c, tm), 1)
    tiles = [u_ref[0, i].astype(F32) for i in range(nt)]
    prev = [pltpu.roll(t, 1, 1) for t in tiles]
    nxt = [pltpu.roll(t, tm - 1, 1) for t in tiles]
    zero = jnp.zeros((c, tm), F32)
    out = []
    for i in range(nt):
        up = jnp.where(lane == 0, prev[i - 1] if i > 0 else zero, prev[i])
        un = jnp.where(lane == tm - 1, nxt[i + 1] if i < nt - 1 else zero, nxt[i])
        out.append(bias + up * w[:, 0:1] + tiles[i] * w[:, 1:2] + un * w[:, 2:3])
    return out


def _sconv_kernel(u0_ref, u1_ref, u2_ref, w_ref, b_ref, g_ref, x0_ref):
    x0 = _short_conv_tiles(u0_ref, w_ref[0], b_ref[0])
    x1 = _short_conv_tiles(u1_ref, w_ref[1], b_ref[1])
    v = _short_conv_tiles(u2_ref, w_ref[2], b_ref[2])
    for i in range(len(x0)):
        g_ref[0, i] = (v[i] * x1[i]).astype(g_ref.dtype)
        x0_ref[0, i] = x0[i].astype(x0_ref.dtype)


def _short_conv_gate(u, cw, cb):
    b, nt, _, _ = u.shape
    w = HYENA_WIDTH
    nc = w // CONV_CB
    uspec = lambda grp: pl.BlockSpec((1, nt, CONV_CB, TM), lambda bi, c: (bi, 0, grp * nc + c, 0))
    ospec = pl.BlockSpec((1, nt, CONV_CB, TM), lambda bi, c: (bi, 0, c, 0))
    return pl.pallas_call(
        _sconv_kernel,
        grid=(b, nc),
        in_specs=[uspec(0), uspec(1), uspec(2),
                  pl.BlockSpec((3, CONV_CB, SHORT_CONV), lambda bi, c: (0, c, 0)),
                  pl.BlockSpec((3, CONV_CB, 1), lambda bi, c: (0, c, 0))],
        out_specs=[ospec, ospec],
        out_shape=[jax.ShapeDtypeStruct((b, nt, w, TM), BF16)] * 2,
        compiler_params=_params("arbitrary", "arbitrary"),
        name="short_conv",
    )(u, u, u, cw, cb)


def _dft_consts(nt):
    n1 = 2 * nt
    n = n1 * TM
    two_pi = 2.0 * math.pi

    def cs(prod, mod):
        ang = (prod % mod).astype(F32) * (two_pi / mod)
        return jnp.cos(ang), jnp.sin(ang)

    f1 = jnp.arange(n1, dtype=I32)
    t1 = jnp.arange(nt, dtype=I32)
    t2 = jnp.arange(TM, dtype=I32)
    c, s = cs(f1[:, None] * t1[None, :], n1)
    fwd1 = jnp.concatenate([c, -s], axis=0).astype(BF16)
    c, s = cs(f1[:, None] * t2[None, :], n)
    tw = jnp.stack([c, -s])
    c, s = cs(t2[:, None] * t2[None, :], TM)
    fwd2 = jnp.stack([jnp.concatenate([c, -s], axis=1),
                      jnp.concatenate([s, c], axis=1)]).astype(BF16)
    inv2 = jnp.stack([jnp.concatenate([c, s], axis=1),
                      jnp.concatenate([-s, c], axis=1)]).astype(BF16)
    c, s = cs(t1[:, None] * f1[None, :], n1)
    inv1 = (jnp.stack([c, -s]) * (1.0 / n)).astype(BF16)
    return fwd1, tw, fwd2, inv2, inv1


def _fft_fwd(g, fwd1, tw, fwd2, cb):
    n1 = fwd1.shape[0] // 2
    a = jnp.dot(fwd1, g, preferred_element_type=F32)
    are = jnp.concatenate([a[:n1, c * TM:(c + 1) * TM] for c in range(cb)], axis=0)
    aim = jnp.concatenate([a[n1:, c * TM:(c + 1) * TM] for c in range(cb)], axis=0)
    are = are.reshape(cb, n1, TM)
    aim = aim.reshape(cb, n1, TM)
    twr, twi = tw[0][None], tw[1][None]
    pr = (are * twr - aim * twi).reshape(cb * n1, TM).astype(BF16)
    pi = (are * twi + aim * twr).reshape(cb * n1, TM).astype(BF16)
    return (jnp.dot(pr, fwd2[0], preferred_element_type=F32)
            + jnp.dot(pi, fwd2[1], preferred_element_type=F32))


def _fftconv_kernel(g_ref, x0_ref, k_ref, fb_ref, fwd1_ref, tw_ref, fwd2_ref, inv2_ref, inv1_ref, o_ref):
    cb = g_ref.shape[-1] // TM
    n1 = fwd1_ref.shape[0] // 2
    g = g_ref[0]
    tw = tw_ref[...]
    x = _fft_fwd(g, fwd1_ref[...], tw, fwd2_ref[...], cb)
    kf = k_ref[...]
    xr, xi = x[:, :TM], x[:, TM:]
    kr, ki = kf[:, :TM], kf[:, TM:]
    yr = (xr * kr - xi * ki).astype(BF16)
    yi = (xr * ki + xi * kr).astype(BF16)
    bc = (jnp.dot(yr, inv2_ref[0], preferred_element_type=F32)
          + jnp.dot(yi, inv2_ref[1], preferred_element_type=F32))
    br = bc[:, :TM].reshape(cb, n1, TM)
    bi = bc[:, TM:].reshape(cb, n1, TM)
    twr, twi = tw[0][None], tw[1][None]
    pr = br * twr + bi * twi
    pi = bi * twr - br * twi
    prl = jnp.concatenate([pr[c] for c in range(cb)], axis=1).astype(BF16)
    pil = jnp.concatenate([pi[c] for c in range(cb)], axis=1).astype(BF16)
    y = (jnp.dot(inv1_ref[0], prl, preferred_element_type=F32)
         + jnp.dot(inv1_ref[1], pil, preferred_element_type=F32))
    gf = g.astype(F32)
    o_ref[0] = (x0_ref[0].astype(F32) * (y + fb_ref[...] * gf)).astype(o_ref.dtype)


def _fft_conv(g, x0, kf, fb, consts):
    b, nt, w, _ = g.shape
    n1 = 2 * nt
    cb = FFT_CB
    g2 = g.reshape(b, nt, w * TM)
    x2 = x0.reshape(b, nt, w * TM)
    fwd1, tw, fwd2, inv2, inv1 = consts
    dspec = pl.BlockSpec((1, nt, cb * TM), lambda c, bi: (bi, 0, c))
    full = lambda a: pl.BlockSpec(a.shape, lambda c, bi: (0,) * a.ndim)
    out = pl.pallas_call(
        _fftconv_kernel,
        grid=(w // cb, b),
        in_specs=[dspec, dspec,
                  pl.BlockSpec((cb * n1, 2 * TM), lambda c, bi: (c, 0)),
                  pl.BlockSpec((1, cb * TM), lambda c, bi: (0, c)),
                  full(fwd1), full(tw), full(fwd2), full(inv2), full(inv1)],
        out_specs=dspec,
        out_shape=jax.ShapeDtypeStruct((b, nt, w * TM), BF16),
        compiler_params=_params("arbitrary", "arbitrary"),
        name="long_conv",
    )(g2, x2, kf, fb, fwd1, tw, fwd2, inv2, inv1)
    return out.reshape(b, nt, w, TM)


def _spectrum_kernel(hf_ref, hb_ref, fwd1_ref, tw_ref, fwd2_ref, k_ref):
    cb = hf_ref.shape[-1] // TM
    tw = tw_ref[...]
    xf = _fft_fwd(hf_ref[...].astype(BF16), fwd1_ref[...], tw, fwd2_ref[...], cb)
    xb = _fft_fwd(hb_ref[...].astype(BF16), fwd1_ref[...], tw, fwd2_ref[...], cb)
    k_ref[...] = jnp.concatenate([xf[:, :TM] + xb[:, :TM], xf[:, TM:] - xb[:, TM:]], axis=1)


def _filter_spectrum(hf, hb, consts):
    nt, w, _ = hf.shape
    n1 = 2 * nt
    cb = FFT_CB
    fwd1, tw, fwd2, _, _ = consts
    hspec = pl.BlockSpec((nt, cb * TM), lambda c: (0, c))
    full = lambda a: pl.BlockSpec(a.shape, lambda c: (0,) * a.ndim)
    return pl.pallas_call(
        _spectrum_kernel,
        grid=(w // cb,),
        in_specs=[hspec, hspec, full(fwd1), full(tw), full(fwd2)],
        out_specs=pl.BlockSpec((cb * n1, 2 * TM), lambda c: (c, 0)),
        out_shape=jax.ShapeDtypeStruct((w * n1, 2 * TM), F32),
        compiler_params=_params("arbitrary"),
        name="filter_spectrum",
    )(hf.reshape(nt, w * TM), hb.reshape(nt, w * TM), fwd1, tw, fwd2)


def _filter_kernel(z_ref, t_ref, w1_ref, b1_ref, w2_ref, b2_ref, fr_ref, w3f_ref, w3b_ref, dl_ref,
                   hf_ref, hb_ref):
    nt = hf_ref.shape[0]
    dot = functools.partial(jnp.dot, preferred_element_type=F32, precision=HIGHEST)
    fr = fr_ref[...]
    h = jnp.sin(fr * (dot(w1_ref[...], z_ref[...]) + b1_ref[...]))
    h = jnp.sin(fr * (dot(w2_ref[...], h) + b2_ref[...]))
    decay = jnp.exp(-t_ref[...] * jnp.abs(dl_ref[...]))
    hf = dot(w3f_ref[...], h) * decay
    hb = dot(w3b_ref[...], h) * decay
    norm = (jnp.sum(jnp.abs(hf), axis=1, keepdims=True)
            + jnp.sum(jnp.abs(hb), axis=1, keepdims=True))
    hf = hf / norm
    hb = hb / norm
    lane = lax.broadcasted_iota(I32, hb.shape, 1)
    hb = jnp.where(lane == 0, 0.0, hb)
    for i in range(nt):
        hf_ref[i] = hf[:, i * TM:(i + 1) * TM]
        hb_ref[i] = hb[:, i * TM:(i + 1) * TM]


def _implicit_filters(l, fw1, fb1, fw2, fb2, fw3, ffreq):
    nt = l // TM
    w = HYENA_WIDTH
    t = jnp.linspace(0.0, 1.0, l, dtype=F32)[None, :]
    bands = (FILTER_EMB_DIM - 1) // 2
    wv = 2.0 * math.pi * jnp.arange(l, dtype=F32)[None, :] / l
    f = jnp.linspace(1e-4, bands - 1, bands, dtype=F32)[:, None]
    z = jnp.concatenate([t, jnp.cos(f * wv), -jnp.sin(f * wv)], axis=0)
    min_decay = math.log(DECAY_TARGET) / SLOW_DECAY_PCT
    max_decay = math.log(DECAY_TARGET) / FAST_DECAY_PCT
    deltas = jnp.linspace(min_decay, max_decay, w, dtype=F32)[:, None]
    w3t = fw3.T
    nc = w // FILT_CB
    col = lambda a: a.reshape(-1, 1)
    full = lambda a: pl.BlockSpec(a.shape, lambda c: (0,) * a.ndim)
    args = (z, t, fw1.T, col(fb1), fw2.T, col(fb2), col(ffreq))
    ospec = pl.BlockSpec((nt, FILT_CB, TM), lambda c: (0, c, 0))
    return pl.pallas_call(
        _filter_kernel,
        grid=(nc,),
        in_specs=[full(a) for a in args] + [
            pl.BlockSpec((FILT_CB, FILTER_HIDDEN), lambda c: (c, 0)),
            pl.BlockSpec((FILT_CB, FILTER_HIDDEN), lambda c: (nc + c, 0)),
            pl.BlockSpec((FILT_CB, 1), lambda c: (c, 0))],
        out_specs=[ospec, ospec],
        out_shape=[jax.ShapeDtypeStruct((nt, w, TM), F32)] * 2,
        compiler_params=_params("arbitrary"),
        name="implicit_filter",
    )(*args, w3t, w3t, deltas)


def _dense_dft_consts():
    n = 2 * TM
    two_pi = 2.0 * math.pi
    t = jnp.arange(TM, dtype=I32)
    f = jnp.arange(n, dtype=I32)
    ang = ((t[:, None] * f[None, :]) % n).astype(F32) * (two_pi / n)
    fwd = jnp.concatenate([jnp.cos(ang), -jnp.sin(ang)], axis=1).astype(BF16)
    inv = (jnp.stack([jnp.cos(ang).T, -jnp.sin(ang).T]) * (1.0 / n)).astype(BF16)
    return fwd, inv


def _hyena_tile_kernel(u_ref, w_ref, b_ref, hf_ref, hb_ref, fb_ref, fwd_ref, inv_ref, o_ref):
    w = HYENA_WIDTH
    n = 2 * TM
    x0 = _short_conv_tiles(u_ref.at[:, :, 0:w], w_ref[0], b_ref[0])[0]
    x1 = _short_conv_tiles(u_ref.at[:, :, w:2 * w], w_ref[1], b_ref[1])[0]
    v = _short_conv_tiles(u_ref.at[:, :, 2 * w:3 * w], w_ref[2], b_ref[2])[0]
    g = (v * x1).astype(BF16)
    fwd = fwd_ref[...]
    dot = functools.partial(jnp.dot, preferred_element_type=F32)
    kf = dot(hf_ref[0].astype(BF16), fwd)
    kb = dot(hb_ref[0].astype(BF16), fwd)
    kr = kf[:, :n] + kb[:, :n]
    ki = kf[:, n:] - kb[:, n:]
    x = dot(g, fwd)
    xr, xi = x[:, :n], x[:, n:]
    yr = (xr * kr - xi * ki).astype(BF16)
    yi = (xr * ki + xi * kr).astype(BF16)
    y = dot(yr, inv_ref[0]) + dot(yi, inv_ref[1])
    o_ref[0, 0] = (x0 * (y + fb_ref[...] * g.astype(F32))).astype(o_ref.dtype)


def _hyena_one_tile(u, cw, cb, hf, hb, fbias, consts):
    b = u.shape[0]
    w = HYENA_WIDTH
    fwd, inv = consts
    full = lambda a: pl.BlockSpec(a.shape, lambda bi: (0,) * a.ndim)
    return pl.pallas_call(
        _hyena_tile_kernel,
        grid=(b,),
        in_specs=[pl.BlockSpec((1, 1, 3 * w, TM), lambda bi: (bi, 0, 0, 0)),
                  full(cw), full(cb), full(hf), full(hb), full(fbias), full(fwd), full(inv)],
        out_specs=pl.BlockSpec((1, 1, w, TM), lambda bi: (bi, 0, 0, 0)),
        out_shape=jax.ShapeDtypeStruct((b, 1, w, TM), BF16),
        compiler_params=_params("arbitrary"),
        name="hyena_context",
    )(u, cw, cb, hf, hb, fbias, fwd, inv)


def _top2_sum(a, b, c, d):
    hi1, lo1 = jnp.maximum(a, b), jnp.minimum(a, b)
    hi2, lo2 = jnp.maximum(c, d), jnp.minimum(c, d)
    return jnp.maximum(hi1, hi2) + jnp.maximum(jnp.minimum(hi1, hi2), jnp.maximum(lo1, lo2))


def _outproj_kernel(a_ref, y_ref, x_ref, mod_ref, ga_ref, gy_ref, wo_ref, n2_ref, rw_ref, rb_ref,
                    cnt_ref, tri_ref, ctri_ref, xo_ref, row_ref, route_ref, tile_ref, cnto_ref, cnt_sc):
    first = jnp.logical_and(pl.program_id(0) == 0, pl.program_id(1) == 0)

    @pl.when(first)
    def _():
        cnt_sc[...] = cnt_ref[...]

    def group_norm(t, gain):
        t = t.astype(F32)
        return t * lax.rsqrt(jnp.mean(t * t, axis=0, keepdims=True) + EPS) * gain

    mix = jnp.concatenate([group_norm(a_ref[0, 0], ga_ref[...]),
                           group_norm(y_ref[0, 0], gy_ref[...])], axis=0).astype(BF16)
    o = lax.dot_general(mix, wo_ref[...], (((0,), (0,)), ((), ())), preferred_element_type=F32)
    xn = x_ref[0] + mod_ref[0, 2:3, :] * o
    xo_ref[0] = xn
    h2 = _norm_mod(xn, n2_ref[...], mod_ref[0, 3:4, :], mod_ref[0, 4:5, :])

    tm = xn.shape[0]
    nt_dims = (((1,), (1,)), ((), ()))
    h_hi = h2.astype(BF16)
    h_lo = (h2 - h_hi.astype(F32)).astype(BF16)
    rw = rw_ref[...]
    part = lax.dot_general(rw, h_hi, nt_dims, preferred_element_type=F32)
    logits = (part[:N_EXPERTS] + part[N_EXPERTS:]
              + lax.dot_general(rw[:N_EXPERTS], h_lo, nt_dims, preferred_element_type=F32))
    score = jax.nn.sigmoid(logits)
    sel = score + rb_ref[...]
    srow = [sel[e:e + 1] for e in range(N_EXPERTS)]
    prow = [score[e:e + 1] for e in range(N_EXPERTS)]
    epg = EXPERTS_PER_GROUP
    gs = [_top2_sum(*srow[g * epg:(g + 1) * epg]) for g in range(N_EXPERT_GROUPS)]
    bg = jnp.zeros((1, tm), I32)
    best = gs[0]
    for g in range(1, N_EXPERT_GROUPS):
        upd = gs[g] > best
        bg = jnp.where(upd, g, bg)
        best = jnp.where(upd, gs[g], best)

    def pick_group(rows, j):
        out = rows[j]
        for g in range(1, N_EXPERT_GROUPS):
            out = jnp.where(bg == g, rows[g * epg + j], out)
        return out

    cand = [pick_group(srow, j) for j in range(epg)]
    cprob = [pick_group(prow, j) for j in range(epg)]
    i1 = jnp.zeros((1, tm), I32)
    v1, w1 = cand[0], cprob[0]
    for j in range(1, epg):
        upd = cand[j] > v1
        i1 = jnp.where(upd, j, i1)
        v1 = jnp.where(upd, cand[j], v1)
        w1 = jnp.where(upd, cprob[j], w1)
    i2 = jnp.zeros((1, tm), I32)
    v2 = jnp.full((1, tm), -jnp.inf, F32)
    w2 = jnp.zeros((1, tm), F32)
    for j in range(epg):
        upd = jnp.logical_and(i1 != j, cand[j] > v2)
        i2 = jnp.where(upd, j, i2)
        v2 = jnp.where(upd, cand[j], v2)
        w2 = jnp.where(upd, cprob[j], w2)
    den = w1 + w2
    w1, w2 = w1 / den, w2 / den
    first_lower = i1 < i2
    lo = jnp.where(first_lower, i1, i2)
    hi = jnp.where(first_lower, i2, i1)
    w_lo = jnp.where(first_lower, w1, w2)
    w_hi = jnp.where(first_lower, w2, w1)
    pair = jnp.where(lo == 0, hi - 1, jnp.where(lo == 1, hi + 1, PAIRS_PER_GROUP - 1))
    cls = bg * PAIRS_PER_GROUP + pair

    cio = lax.broadcasted_iota(I32, (N_CLASS_ROWS, tm), 0)
    onehot = jnp.where(cio == cls, 1.0, 0.0)
    cum = jnp.dot(onehot.astype(BF16), tri_ref[...], preferred_element_type=F32)
    tot = jnp.sum(onehot, axis=1, keepdims=True)
    tot_al = jnp.floor((tot + (MOVE_ROWS - 1)) * (1.0 / MOVE_ROWS)) * MOVE_ROWS
    run_start = jnp.dot(ctri_ref[...], jnp.broadcast_to(tot_al, (N_CLASS_ROWS, 128)).astype(BF16),
                        preferred_element_type=F32)[:, 0:1]
    pos = jnp.sum(onehot * (run_start + cum), axis=0, keepdims=True).astype(I32)
    route_ref[0, 0] = jnp.concatenate([cls, pos], axis=0)
    base = cnt_sc[...]
    lane = lax.broadcasted_iota(I32, (N_CLASS_ROWS, 128), 1)
    tile_ref[0, 0] = jnp.where(lane == 0, tot, jnp.where(lane == 1, base, 0.0)).astype(I32)
    new = base + tot_al
    cnt_sc[...] = new
    cnto_ref[...] = new

    perm = jnp.where(lax.broadcasted_iota(I32, (SORT_ROWS, tm), 0) == pos, 1.0, 0.0).astype(BF16)
    h_sorted = jnp.dot(perm, h_hi, preferred_element_type=F32)
    half = h2.shape[1] // 2
    wcols = jnp.concatenate([w_lo, w_hi, jnp.zeros((ROW_EXTRA - 2, tm), F32)], axis=0).T
    w_a = wcols.astype(BF16)
    w_b = (wcols - w_a.astype(F32)).astype(BF16)
    w_c = (wcols - w_a.astype(F32) - w_b.astype(F32)).astype(BF16)
    w_sorted = (jnp.dot(perm, w_a, preferred_element_type=F32) + jnp.dot(perm, w_b, preferred_element_type=F32)
                + jnp.dot(perm, w_c, preferred_element_type=F32))
    row_ref[0, 0] = jnp.concatenate([_pack_bf16_pair(h_sorted[:, :half], h_sorted[:, half:]),
                                     lax.bitcast_convert_type(w_sorted, U32)], axis=1)


def _outproj_route(a, y, x, mods, mod_row, ga, gy, wo, n2g, rwt, rb, cnt, tri):
    b, l, d = x.shape
    nt = l // TM
    if mod_row is None:
        mod_map = lambda bi, i: (bi, 0, 0)
    else:
        mod_map = lambda bi, i: (mod_row, 0, 0)
    full = lambda arr: pl.BlockSpec(arr.shape, lambda bi, i: (0,) * arr.ndim)
    tile = pl.BlockSpec((1, 1, ATTN_WIDTH, TM), lambda bi, i: (bi, i, 0, 0))
    xspec = pl.BlockSpec((1, TM, d), lambda bi, i: (bi, i, 0))
    rspec = pl.BlockSpec((1, 1, 2, TM), lambda bi, i: (bi, i, 0, 0))
    row_w = d // 2 + ROW_EXTRA
    ctri = (jnp.arange(N_CLASS_ROWS)[None, :] < jnp.arange(N_CLASS_ROWS)[:, None]).astype(BF16)
    return pl.pallas_call(
        _outproj_kernel,
        grid=(b, nt),
        in_specs=[tile, tile, xspec, pl.BlockSpec((1, N_MOD, d), mod_map),
                  full(ga), full(gy), full(wo), full(n2g), full(rwt), full(rb), full(cnt), full(tri), full(ctri)],
        out_specs=[xspec, pl.BlockSpec((1, 1, SORT_ROWS, row_w), lambda bi, i: (bi, i, 0, 0)), rspec,
                   pl.BlockSpec((1, 1, N_CLASS_ROWS, 128), lambda bi, i: (bi, i, 0, 0)), full(cnt)],
        out_shape=[jax.ShapeDtypeStruct((b, l, d), F32),
                   jax.ShapeDtypeStruct((b, nt, SORT_ROWS, row_w), U32),
                   jax.ShapeDtypeStruct((b, nt, 2, TM), I32),
                   jax.ShapeDtypeStruct((b, nt, N_CLASS_ROWS, 128), I32),
                   jax.ShapeDtypeStruct(cnt.shape, F32)],
        scratch_shapes=[pltpu.VMEM(cnt.shape, F32)],
        compiler_params=_params("arbitrary", "arbitrary"),
        name="outproj_route",
    )(a, y, x, mods, ga, gy, wo, n2g, rwt, rb, cnt, tri, ctri)


def _move_class_windows(meta_ref, make_copy):
    total = 0
    for c in range(N_CLASSES):
        seg_row = meta_ref[0, 0, c]
        tile_row = meta_ref[0, 0, N_CLASSES + c]
        nwin = meta_ref[0, 0, 2 * N_CLASSES + c]

        def issue(k, carry, seg_row=seg_row, tile_row=tile_row):
            make_copy(pl.multiple_of(tile_row + k * MOVE_ROWS, MOVE_ROWS),
                      pl.multiple_of(seg_row + k * MOVE_ROWS, MOVE_ROWS)).start()
            return carry

        lax.fori_loop(0, nwin, issue, 0)
        total = total + nwin

    def drain(k, carry):
        make_copy(0, 0).wait()
        return carry

    lax.fori_loop(0, total, drain, 0)


def _dispatch_kernel(meta_ref, rows_ref, xs_in_ref, xs_ref, sem):
    del xs_in_ref
    _move_class_windows(meta_ref, lambda tile_row, seg_row: pltpu.make_async_copy(
        rows_ref.at[0, pl.ds(tile_row, MOVE_ROWS)], xs_ref.at[pl.ds(seg_row, MOVE_ROWS)], sem))


def _dispatch(meta, rows, xs):
    nt, sr, w = rows.shape
    return pl.pallas_call(
        _dispatch_kernel,
        grid=(nt,),
        in_specs=[pl.BlockSpec((1, 1, META_LANES), lambda i: (i, 0, 0), memory_space=pltpu.SMEM),
                  pl.BlockSpec((1, sr, w), lambda i: (i, 0, 0)),
                  pl.BlockSpec(memory_space=pl.ANY)],
        out_specs=pl.BlockSpec(memory_space=pl.ANY),
        out_shape=jax.ShapeDtypeStruct(xs.shape, xs.dtype),
        scratch_shapes=[pltpu.SemaphoreType.DMA(())],
        input_output_aliases={2: 0},
        compiler_params=_params("arbitrary"),
        name="moe_dispatch",
    )(meta, rows, xs)


def _ffn_kernel(ea_ref, eb_ref, nt_ref, xs_ref, wga_ref, wua_ref, wda_ref, wgb_ref, wub_ref, wdb_ref, ys_ref):
    del ea_ref, eb_ref
    live = pl.program_id(0) < nt_ref[0]
    half = ys_ref.shape[1]

    @pl.when(live)
    def _():
        words = xs_ref[...]
        lo, hi = _unpack_bf16_pair(words[:, :half])
        lo, hi = lo.astype(BF16), hi.astype(BF16)
        wts = lax.bitcast_convert_type(words[:, half:half + 2], F32)

        def expert(wg_ref, wu_ref, wd_ref):
            dot = functools.partial(jnp.dot, preferred_element_type=F32)
            g = dot(lo, wg_ref[0, :half, :]) + dot(hi, wg_ref[0, half:, :])
            u = dot(lo, wu_ref[0, :half, :]) + dot(hi, wu_ref[0, half:, :])
            a = (g * jax.nn.sigmoid(g)) * u
            return dot(a.astype(BF16), wd_ref[0])

        y = (wts[:, 0:1] * expert(wga_ref, wua_ref, wda_ref)
             + wts[:, 1:2] * expert(wgb_ref, wub_ref, wdb_ref))
        ys_ref[...] = _pack_bf16_pair(y[:, :half], y[:, half:])

    @pl.when(jnp.logical_not(live))
    def _():
        ys_ref[...] = jnp.zeros_like(ys_ref)


def _expert_ffn(tile_ea, tile_eb, n_tiles, xs, wg, wu, wd):
    nslot, row_w = xs.shape
    d, f = wg.shape[1:]
    ntile = nslot // FFN_TILE
    row = lambda i, ea, eb, nt: (jnp.minimum(i, nt[0] - 1), 0)
    amap = lambda i, ea, eb, nt: (ea[jnp.minimum(i, nt[0] - 1)], 0, 0)
    bmap = lambda i, ea, eb, nt: (eb[jnp.minimum(i, nt[0] - 1)], 0, 0)
    return pl.pallas_call(
        _ffn_kernel,
        grid_spec=pltpu.PrefetchScalarGridSpec(
            num_scalar_prefetch=3,
            grid=(ntile,),
            in_specs=[pl.BlockSpec((FFN_TILE, row_w), row),
                      pl.BlockSpec((1, d, f), amap), pl.BlockSpec((1, d, f), amap), pl.BlockSpec((1, f, d), amap),
                      pl.BlockSpec((1, d, f), bmap), pl.BlockSpec((1, d, f), bmap), pl.BlockSpec((1, f, d), bmap)],
            out_specs=pl.BlockSpec((FFN_TILE, d // 2), lambda i, ea, eb, nt: (i, 0))),
        out_shape=jax.ShapeDtypeStruct((nslot, d // 2), U32),
        compiler_params=_params("arbitrary"),
        name="moe_experts",
    )(tile_ea, tile_eb, n_tiles, xs, wg, wu, wd, wg, wu, wd)


def _combine_kernel(meta_ref, route_ref, x_ref, mod_ref, ys_ref, o_ref, buf, sem):
    tm = x_ref.shape[1]
    buf[...] = jnp.zeros_like(buf)
    _move_class_windows(meta_ref, lambda tile_row, seg_row: pltpu.make_async_copy(
        ys_ref.at[pl.ds(seg_row, MOVE_ROWS)], buf.at[pl.ds(tile_row, MOVE_ROWS)], sem))
    pos = route_ref[0, 0, 1:2, :]
    perm = jnp.where(lax.broadcasted_iota(I32, (SORT_ROWS, tm), 0) == pos, 1.0, 0.0).astype(BF16)
    lo, hi = _unpack_bf16_pair(buf[...])
    tn_dims = (((0,), (0,)), ((), ()))
    y = jnp.concatenate([lax.dot_general(perm, lo.astype(BF16), tn_dims, preferred_element_type=F32),
                         lax.dot_general(perm, hi.astype(BF16), tn_dims, preferred_element_type=F32)],
                        axis=1)
    o_ref[0] = x_ref[0] + mod_ref[0, 5:6, :] * y


def _combine(meta, route, x, mods, mod_row, ys):
    b, l, d = x.shape
    nt = l // TM
    if mod_row is None:
        mod_map = lambda bi, i: (bi, 0, 0)
    else:
        mod_map = lambda bi, i: (mod_row, 0, 0)
    xspec = pl.BlockSpec((1, TM, d), lambda bi, i: (bi, i, 0))
    return pl.pallas_call(
        _combine_kernel,
        grid=(b, nt),
        in_specs=[pl.BlockSpec((1, 1, META_LANES), lambda bi, i: (bi * nt + i, 0, 0), memory_space=pltpu.SMEM),
                  pl.BlockSpec((1, 1, 2, TM), lambda bi, i: (bi, i, 0, 0)),
                  xspec,
                  pl.BlockSpec((1, N_MOD, d), mod_map),
                  pl.BlockSpec(memory_space=pl.ANY)],
        out_specs=xspec,
        out_shape=jax.ShapeDtypeStruct((b, l, d), F32),
        scratch_shapes=[pltpu.VMEM((SORT_ROWS, d // 2), U32), pltpu.SemaphoreType.DMA(())],
        compiler_params=_params("arbitrary", "arbitrary"),
        name="moe_combine",
    )(meta, route, x, mods, ys)


def _moe(streams, wg, wu, wd):
    counts = streams[-1]["counts"].reshape(-1).astype(I32)
    padded = ((counts + FFN_TILE - 1) // FFN_TILE) * FFN_TILE
    ends = jnp.cumsum(padded)
    offs = ends - padded
    total = sum(s["x"].shape[0] * s["x"].shape[1] for s in streams)
    worst_rows = total + (total // TM) * N_CLASSES * (MOVE_ROWS - 1)
    ntile = -(-worst_rows // FFN_TILE) + N_CLASSES
    nslot = ntile * FFN_TILE
    tile_start = jnp.arange(ntile, dtype=I32) * FFN_TILE
    tile_class = jnp.minimum(jnp.sum((tile_start[:, None] >= ends[None, :N_CLASSES]).astype(I32), axis=1),
                             N_CLASSES - 1)
    group_base = (tile_class // PAIRS_PER_GROUP) * EXPERTS_PER_GROUP
    tile_ea = (group_base + jnp.asarray(PAIR_LO, I32)[tile_class % PAIRS_PER_GROUP]).astype(I32)
    tile_eb = (group_base + jnp.asarray(PAIR_HI, I32)[tile_class % PAIRS_PER_GROUP]).astype(I32)
    n_tiles = (ends[-1] // FFN_TILE).astype(I32).reshape(1)

    row_w = streams[0]["rows"].shape[-1]
    xs = jnp.zeros((nslot, row_w), U32)
    for s in streams:
        b, l, _ = s["x"].shape
        nt = l // TM
        tile_tot = s["tiles"][..., 0].reshape(b * nt, -1)[:, :N_CLASSES]
        tile_base = s["tiles"][..., 1].reshape(b * nt, -1)[:, :N_CLASSES]
        nwin = (tile_tot + MOVE_ROWS - 1) // MOVE_ROWS
        run_rows = nwin * MOVE_ROWS
        tile_row = jnp.cumsum(run_rows, axis=1) - run_rows
        seg_row = offs[None, :N_CLASSES] + tile_base
        meta = jnp.concatenate([seg_row, tile_row, nwin,
                                jnp.zeros((b * nt, META_LANES - 3 * N_CLASSES), I32)], axis=1)
        s["meta"] = meta.reshape(b * nt, 1, META_LANES)
        xs = _dispatch(s["meta"], s["rows"].reshape(b * nt, SORT_ROWS, row_w), xs)
    ys = _expert_ffn(tile_ea, tile_eb, n_tiles, xs, wg, wu, wd)
    return [_combine(s["meta"], s["route"], s["x"], s["mods"], s["mod_row"], ys) for s in streams]


def _rope_table(l):
    t = jnp.arange(l)
    row = (t // GRID_W).astype(F32)
    col = (t % GRID_W).astype(F32)
    inv_freq = ROPE_THETA ** (-jnp.arange(ROPE_QUARTER, dtype=F32) / ROPE_QUARTER)
    ar = inv_freq[:, None] * row[None, :]
    ac = inv_freq[:, None] * col[None, :]
    return jnp.concatenate([jnp.cos(ar), jnp.sin(ar), jnp.cos(ac), jnp.sin(ac)], axis=0)


def _identity_rope_table(l):
    one = jnp.ones((ROPE_QUARTER, l), F32)
    zero = jnp.zeros((ROPE_QUARTER, l), F32)
    return jnp.concatenate([one, zero, one, zero], axis=0)


def kernel(x, c, ctx, c_ctx, w_mod, b_mod, norm1_g, w_in, q_norm_g, k_norm_g, conv_w, conv_b, filt_w1, filt_b1, filt_w2, filt_b2, filt_w3, filt_freq, filt_bias, attn_out_g, hyena_out_g, w_out, norm2_g, router_w, router_bias, expert_w_gate, expert_w_up, expert_w_down):
    depth = w_mod.shape[0]
    b, l, d = x.shape
    lc = ctx.shape[1]
    assert l % TM == 0 and lc == TM and d == D_MODEL
    nt = l // TM
    w = HYENA_WIDTH

    mods = _modulations(c, c_ctx, w_mod, b_mod)
    rope_lat = _rope_table(l)
    rope_ctx = _identity_rope_table(lc)
    dft = _dft_consts(nt)
    dft_ctx = _dense_dft_consts()
    tri = (jnp.arange(TM)[:, None] < jnp.arange(TM)[None, :]).astype(BF16)
    rwt = router_w.T
    rw_hi = rwt.astype(BF16)
    rwt = jnp.concatenate([rw_hi, (rwt - rw_hi.astype(F32)).astype(BF16)], axis=0)
    rb = router_bias.reshape(-1, 1)
    zero_cnt = jnp.zeros((N_CLASS_ROWS, 1), F32)
    col = lambda a: a.reshape(-1, 1)

    for li in range(depth):
        last = li == depth - 1
        m = mods[li]
        wt = w_in[li].T.astype(BF16)
        wo = w_out[li].astype(BF16)
        g1n = norm1_g[li].reshape(1, d)
        g2n = norm2_g[li].reshape(1, d)
        qg, kg = col(q_norm_g[li]), col(k_norm_g[li])
        cw = conv_w[li].T.reshape(3, w, SHORT_CONV)
        cb = conv_b[li].reshape(3, w, 1)
        ga, gy = col(attn_out_g[li]), col(hyena_out_g[li])
        fargs = (filt_w1[li], filt_b1[li], filt_w2[li], filt_b2[li], filt_w3[li], filt_freq[li])

        q, k, v, u = _inproj(x, m, None, g1n, wt, qg, kg, rope_lat)
        qc, kc, vc, uc = _inproj(ctx, m, b, g1n, wt, qg, kg, rope_ctx)
        a = _attention(q, jnp.concatenate([kc, k], axis=2), jnp.concatenate([vc, v], axis=2))
        hf, hb = _implicit_filters(l, *fargs)
        kf = _filter_spectrum(hf, hb, dft)
        g, x0 = _short_conv_gate(u, cw, cb)
        fb_lanes = jnp.repeat(filt_bias[li], TM).reshape(1, w * TM)
        y = _fft_conv(g, x0, kf, fb_lanes, dft)
        lat = dict(zip(("x", "rows", "route", "tiles", "counts"),
                       _outproj_route(a, y, x, m, None, ga, gy, wo, g2n, rwt, rb, zero_cnt, tri)))
        lat.update(mods=m, mod_row=None)
        streams = [lat]

        if not last:
            ac = _attention(qc, kc, vc)
            hfc, hbc = _implicit_filters(lc, *fargs)
            yc = _hyena_one_tile(uc, cw, cb, hfc, hbc, col(filt_bias[li]), dft_ctx)
            cs = dict(zip(("x", "rows", "route", "tiles", "counts"),
                          _outproj_route(ac, yc, ctx, m, b, ga, gy, wo, g2n, rwt, rb, lat["counts"], tri)))
            cs.update(mods=m, mod_row=b)
            streams.append(cs)

        outs = _moe(streams, expert_w_gate[li].astype(BF16), expert_w_up[li].astype(BF16),
                    expert_w_down[li].astype(BF16))
        x = outs[0]
        if not last:
            ctx = outs[1]
    return x
```

```python
import functools
import math

import jax
import jax.numpy as jnp
from jax import lax
from jax.experimental import pallas as pl
from jax.experimental.pallas import tpu as pltpu

F32 = jnp.float32
BF16 = jnp.bfloat16
I32 = jnp.int32
HIGHEST = lax.Precision.HIGHEST

D_MODEL = 1024
N_MOD = 6
EPS = 1e-6
N_Q_HEADS = 8
N_KV_HEADS = 2
HEAD_DIM = 64
KV_REP = N_Q_HEADS // N_KV_HEADS
ATTN_WIDTH = N_Q_HEADS * HEAD_DIM
KV_WIDTH = N_KV_HEADS * HEAD_DIM
ATTN_SCALE = HEAD_DIM ** -0.5
LOG2_E = math.log2(math.e)
V_ROWS = HEAD_DIM + 16
K_COLS = HEAD_DIM + 16
SCORE_BOUND_SLACK = 1.0 + 2.0 ** -6
UNDERFLOW_GUARD = 2.0 ** -80
GRID_W = 64
ROPE_THETA = 10000.0
ROPE_QUARTER = HEAD_DIM // 4
HYENA_WIDTH = D_MODEL - ATTN_WIDTH
SHORT_CONV = 3
FILTER_EMB_DIM = 33
FILTER_HIDDEN = 64
DECAY_TARGET = 1e-2
FAST_DECAY_PCT = 0.3
SLOW_DECAY_PCT = 1.5
N_EXPERTS = 16
N_EXPERT_GROUPS = 4
EXPERTS_PER_GROUP = N_EXPERTS // N_EXPERT_GROUPS
D_FF_EXPERT = 512

TM = 256
ATTN_UNROLL = 8
ATTN_TQ = 1024
FFN_TILE = 256
CONV_CB = 128
FFT_CB = 32
FILT_CB = 128
VMEM_LIMIT = 56 * 1024 * 1024


U32 = jnp.uint32
PAIRS_PER_GROUP = EXPERTS_PER_GROUP * (EXPERTS_PER_GROUP - 1) // 2
N_CLASSES = N_EXPERT_GROUPS * PAIRS_PER_GROUP
N_CLASS_ROWS = -(-N_CLASSES // 8) * 8
PAIR_LO = tuple(a for a in range(EXPERTS_PER_GROUP) for b in range(a + 1, EXPERTS_PER_GROUP))
PAIR_HI = tuple(b for a in range(EXPERTS_PER_GROUP) for b in range(a + 1, EXPERTS_PER_GROUP))
ROW_EXTRA = 128


def _pack_bf16_pair(lo, hi):
    lo_bits = lax.bitcast_convert_type(lo.astype(BF16).astype(F32), U32)
    hi_bits = lax.bitcast_convert_type(hi.astype(BF16).astype(F32), U32)
    return lax.shift_right_logical(lo_bits, U32(16)) | (hi_bits & U32(0xFFFF0000))


def _unpack_bf16_pair(words):
    lo = lax.bitcast_convert_type(lax.shift_left(words, U32(16)), F32)
    hi = lax.bitcast_convert_type(words & U32(0xFFFF0000), F32)
    return lo, hi


def _params(*sem):
    return pltpu.CompilerParams(dimension_semantics=tuple(sem), vmem_limit_bytes=VMEM_LIMIT)


def _norm_mod(x, g, shift, scale):
    y = x * lax.rsqrt(jnp.mean(x * x, axis=-1, keepdims=True) + EPS)
    return (y * g) * (1 + scale) + shift


def _mod_kernel(c_ref, w_ref, b_ref, o_ref):
    c = c_ref[...]
    s = c * jax.nn.sigmoid(c)
    o_ref[0] = jnp.dot(s, w_ref[0], preferred_element_type=F32, precision=HIGHEST) + b_ref[0]


def _modulations(c, c_ctx, w_mod, b_mod):
    depth, d, nmd = w_mod.shape
    b = c.shape[0]
    rows = -(-(b + 1) // 8) * 8
    c_all = jnp.zeros((rows, d), F32).at[:b].set(c).at[b].set(c_ctx)
    tn = nmd // 4
    out = pl.pallas_call(
        _mod_kernel,
        grid=(depth, nmd // tn),
        in_specs=[pl.BlockSpec((rows, d), lambda l, j: (0, 0)),
                  pl.BlockSpec((1, d, tn), lambda l, j: (l, 0, j)),
                  pl.BlockSpec((1, 1, tn), lambda l, j: (l, 0, j))],
        out_specs=pl.BlockSpec((1, rows, tn), lambda l, j: (l, 0, j)),
        out_shape=jax.ShapeDtypeStruct((depth, rows, nmd), F32),
        compiler_params=_params("arbitrary", "arbitrary"),
        name="modulation",
    )(c_all, w_mod, b_mod.reshape(depth, 1, nmd))
    return out.reshape(depth, rows, N_MOD, d)


def _inproj_kernel(x_ref, mod_ref, g_ref, wt_ref, qg_ref, kg_ref, rope_ref,
                   q_ref, k_ref, v_ref, u_ref):
    x = x_ref[0]
    tm = x.shape[0]
    h = _norm_mod(x, g_ref[...], mod_ref[0, 0:1, :], mod_ref[0, 1:2, :])
    pt = lax.dot_general(wt_ref[...], h.astype(BF16), (((1,), (1,)), ((), ())),
                         preferred_element_type=F32)
    rope = rope_ref[...]
    qd = ROPE_QUARTER
    cr, sr = rope[0:qd][None], rope[qd:2 * qd][None]
    cc, sc = rope[2 * qd:3 * qd][None], rope[3 * qd:4 * qd][None]

    def norm_rope(t, gain, nh):
        t = t.reshape(nh, HEAD_DIM, tm)
        t = t * lax.rsqrt(jnp.mean(t * t, axis=1, keepdims=True) + EPS) * gain[None]
        a, b = t[:, 0:qd], t[:, qd:2 * qd]
        c, d = t[:, 2 * qd:3 * qd], t[:, 3 * qd:4 * qd]
        return jnp.concatenate([a * cr - b * sr, b * cr + a * sr,
                                c * cc - d * sc, d * cc + c * sc], axis=1)

    q = norm_rope(pt[0:ATTN_WIDTH], qg_ref[...], N_Q_HEADS) * (ATTN_SCALE * LOG2_E)
    q_ref[0] = q.astype(BF16)
    k = norm_rope(pt[ATTN_WIDTH:ATTN_WIDTH + KV_WIDTH], kg_ref[...], N_KV_HEADS)
    kt = k.reshape(KV_WIDTH, tm).T
    for g in range(N_KV_HEADS):
        k_ref[0, g, 0, :, 0:HEAD_DIM] = kt[:, g * HEAD_DIM:(g + 1) * HEAD_DIM].astype(BF16)
    pad_col = lax.broadcasted_iota(I32, (N_KV_HEADS, tm, K_COLS - HEAD_DIM), 2)
    k_ref[0, :, 0, :, HEAD_DIM:K_COLS] = jnp.where(pad_col == 0, 1.0, 0.0).astype(BF16)
    v = pt[ATTN_WIDTH + KV_WIDTH:ATTN_WIDTH + 2 * KV_WIDTH]
    v_ref[0, :, 0, 0:HEAD_DIM] = v.reshape(N_KV_HEADS, HEAD_DIM, tm).astype(BF16)
    pad_row = lax.broadcasted_iota(I32, (N_KV_HEADS, V_ROWS - HEAD_DIM, tm), 1)
    v_ref[0, :, 0, HEAD_DIM:V_ROWS] = jnp.where(pad_row == 0, 1.0, 0.0).astype(BF16)
    u_ref[0, 0] = pt[ATTN_WIDTH + 2 * KV_WIDTH:].astype(BF16)


def _inproj(x, mods, mod_row, g, wt, qg, kg, rope):
    b, l, d = x.shape
    nt = l // TM
    p = wt.shape[0]
    uw = p - ATTN_WIDTH - 2 * KV_WIDTH
    if mod_row is None:
        mod_map = lambda bi, i: (bi, 0, 0)
    else:
        mod_map = lambda bi, i: (mod_row, 0, 0)
    return pl.pallas_call(
        _inproj_kernel,
        grid=(b, nt),
        in_specs=[pl.BlockSpec((1, TM, d), lambda bi, i: (bi, i, 0)),
                  pl.BlockSpec((1, N_MOD, d), mod_map),
                  pl.BlockSpec((1, d), lambda bi, i: (0, 0)),
                  pl.BlockSpec((p, d), lambda bi, i: (0, 0)),
                  pl.BlockSpec((HEAD_DIM, 1), lambda bi, i: (0, 0)),
                  pl.BlockSpec((HEAD_DIM, 1), lambda bi, i: (0, 0)),
                  pl.BlockSpec((HEAD_DIM, TM), lambda bi, i: (0, i))],
        out_specs=[pl.BlockSpec((1, N_Q_HEADS, HEAD_DIM, TM), lambda bi, i: (bi, 0, 0, i)),
                   pl.BlockSpec((1, N_KV_HEADS, 1, TM, K_COLS), lambda bi, i: (bi, 0, i, 0, 0)),
                   pl.BlockSpec((1, N_KV_HEADS, 1, V_ROWS, TM), lambda bi, i: (bi, 0, i, 0, 0)),
                   pl.BlockSpec((1, 1, uw, TM), lambda bi, i: (bi, i, 0, 0))],
        out_shape=[jax.ShapeDtypeStruct((b, N_Q_HEADS, HEAD_DIM, l), BF16),
                   jax.ShapeDtypeStruct((b, N_KV_HEADS, nt, TM, K_COLS), BF16),
                   jax.ShapeDtypeStruct((b, N_KV_HEADS, nt, V_ROWS, TM), BF16),
                   jax.ShapeDtypeStruct((b, nt, uw, TM), BF16)],
        compiler_params=_params("arbitrary", "arbitrary"),
        name="inproj",
    )(x, mods, g, wt, qg, kg, rope)


def _attn_kernel(q_ref, k_ref, v_ref, kmax_ref, o_ref, acc_ref, m_ref):
    nk = k_ref.shape[2]
    tq = q_ref.shape[-1]
    nq = KV_REP * tq
    q = jnp.concatenate([q_ref[0, r] for r in range(KV_REP)], axis=1)
    pad = jnp.zeros((K_COLS - HEAD_DIM - 1, nq), F32)

    def write_out():
        acc = acc_ref[...]
        out = acc[0:HEAD_DIM] / acc[HEAD_DIM:HEAD_DIM + 1]
        for r in range(KV_REP):
            for t in range(tq // TM):
                o_ref[0, t, r * HEAD_DIM:(r + 1) * HEAD_DIM, :] = (
                    out[:, r * tq + t * TM:r * tq + (t + 1) * TM].astype(o_ref.dtype))

    def run(chunk):
        def group(i, carry):
            for t in range(ATTN_UNROLL):
                chunk(ATTN_UNROLL * i + t)
            return carry

        lax.fori_loop(0, nk // ATTN_UNROLL, group, 0)
        for j in range(nk - nk % ATTN_UNROLL, nk):
            chunk(j)

    qf = q.astype(F32)
    qnorm = jnp.sqrt(jnp.sum(qf * qf, axis=0, keepdims=True))
    bound = qnorm * (kmax_ref[0, 0][:, 0:1] * SCORE_BOUND_SLACK)
    q_shift = jnp.concatenate([q, jnp.concatenate([-bound, pad], axis=0).astype(BF16)], axis=0)
    acc_ref[...] = jnp.zeros_like(acc_ref)

    def chunk_shifted(j):
        s = jnp.dot(k_ref[0, 0, j], q_shift, preferred_element_type=F32)
        p = jnp.exp2(s).astype(BF16)
        acc_ref[...] += jnp.dot(v_ref[0, 0, j], p, preferred_element_type=F32)

    run(chunk_shifted)
    denom_ok = jnp.min(acc_ref[HEAD_DIM:HEAD_DIM + 1, :]) >= UNDERFLOW_GUARD

    @pl.when(denom_ok)
    def _():
        write_out()

    @pl.when(jnp.logical_not(denom_ok))
    def _():
        q_plain = jnp.concatenate([q, jnp.zeros((K_COLS - HEAD_DIM, nq), BF16)], axis=0)
        acc_ref[...] = jnp.zeros_like(acc_ref)
        m_ref[...] = jnp.full_like(m_ref, -jnp.inf)

        def chunk_online(j):
            s = jnp.dot(k_ref[0, 0, j], q_plain, preferred_element_type=F32)
            m = m_ref[...]
            m_new = jnp.maximum(m, jnp.max(s, axis=0, keepdims=True))
            alpha = jnp.exp2(m - m_new)
            p = jnp.exp2(s - m_new).astype(BF16)
            acc_ref[...] = alpha * acc_ref[...] + jnp.dot(v_ref[0, 0, j], p, preferred_element_type=F32)
            m_ref[...] = m_new

        run(chunk_online)
        write_out()


def _attention(q, k, v):
    b, _, _, l = q.shape
    nk = k.shape[2]
    nt = l // TM
    tq = min(ATTN_TQ, l)
    kf = k[..., :HEAD_DIM].astype(F32)
    kmax = jnp.sqrt(jnp.max(jnp.sum(kf * kf, axis=-1), axis=(2, 3)))
    kmax = jnp.broadcast_to(kmax[:, :, None, None], (b, N_KV_HEADS, 1, 128))
    return pl.pallas_call(
        _attn_kernel,
        grid=(b, N_KV_HEADS, l // tq),
        in_specs=[pl.BlockSpec((1, KV_REP, HEAD_DIM, tq), lambda bi, g, i: (bi, g, 0, i)),
                  pl.BlockSpec((1, 1, nk, TM, K_COLS), lambda bi, g, i: (bi, g, 0, 0, 0)),
                  pl.BlockSpec((1, 1, nk, V_ROWS, TM), lambda bi, g, i: (bi, g, 0, 0, 0)),
                  pl.BlockSpec((1, 1, 1, 128), lambda bi, g, i: (bi, g, 0, 0))],
        out_specs=pl.BlockSpec((1, tq // TM, KV_REP * HEAD_DIM, TM), lambda bi, g, i: (bi, i, g, 0)),
        out_shape=jax.ShapeDtypeStruct((b, nt, ATTN_WIDTH, TM), BF16),
        scratch_shapes=[pltpu.VMEM((V_ROWS, KV_REP * tq), F32), pltpu.VMEM((1, KV_REP * tq), F32)],
        compiler_params=_params("arbitrary", "arbitrary", "arbitrary"),
        name="attention",
    )(q, k, v, kmax)


def _short_conv_tiles(u_ref, w, bias):
    nt, c, tm = u_ref.shape[1:]
    lane = lax.broadcasted_iota(I32, (c, tm), 1)
    tiles = [u_ref[0, i].astype(F32) for i in range(nt)]
    prev = [pltpu.roll(t, 1, 1) for t in tiles]
    nxt = [pltpu.roll(t, tm - 1, 1) for t in tiles]
    zero = jnp.zeros((c, tm), F32)
    out = []
    for i in range(nt):
        up = jnp.where(lane == 0, prev[i - 1] if i > 0 else zero, prev[i])
        un = jnp.where(lane == tm - 1, nxt[i + 1] if i < nt - 1 else zero, nxt[i])
        out.append(bias + up * w[:, 0:1] + tiles[i] * w[:, 1:2] + un * w[:, 2:3])
    return out


def _sconv_kernel(u0_ref, u1_ref, u2_ref, w_ref, b_ref, g_ref, x0_ref):
    x0 = _short_conv_tiles(u0_ref, w_ref[0], b_ref[0])
    x1 = _short_conv_tiles(u1_ref, w_ref[1], b_ref[1])
    v = _short_conv_tiles(u2_ref, w_ref[2], b_ref[2])
    for i in range(len(x0)):
        g_ref[0, i] = (v[i] * x1[i]).astype(g_ref.dtype)
        x0_ref[0, i] = x0[i].astype(x0_ref.dtype)


def _short_conv_gate(u, cw, cb):
    b, nt, _, _ = u.shape
    w = HYENA_WIDTH
    nc = w // CONV_CB
    uspec = lambda grp: pl.BlockSpec((1, nt, CONV_CB, TM), lambda bi, c: (bi, 0, grp * nc + c, 0))
    ospec = pl.BlockSpec((1, nt, CONV_CB, TM), lambda bi, c: (bi, 0, c, 0))
    return pl.pallas_call(
        _sconv_kernel,
        grid=(b, nc),
        in_specs=[uspec(0), uspec(1), uspec(2),
                  pl.BlockSpec((3, CONV_CB, SHORT_CONV), lambda bi, c: (0, c, 0)),
                  pl.BlockSpec((3, CONV_CB, 1), lambda bi, c: (0, c, 0))],
        out_specs=[ospec, ospec],
        out_shape=[jax.ShapeDtypeStruct((b, nt, w, TM), BF16)] * 2,
        compiler_params=_params("arbitrary", "arbitrary"),
        name="short_conv",
    )(u, u, u, cw, cb)


def _dft_consts(nt):
    n1 = 2 * nt
    n = n1 * TM
    two_pi = 2.0 * math.pi

    def cs(prod, mod):
        ang = (prod % mod).astype(F32) * (two_pi / mod)
        return jnp.cos(ang), jnp.sin(ang)

    f1 = jnp.arange(n1, dtype=I32)
    t1 = jnp.arange(nt, dtype=I32)
    t2 = jnp.arange(TM, dtype=I32)
    c, s = cs(f1[:, None] * t1[None, :], n1)
    fwd1 = jnp.concatenate([c, -s], axis=0).astype(BF16)
    c, s = cs(f1[:, None] * t2[None, :], n)
    tw = jnp.stack([c, -s])
    c, s = cs(t2[:, None] * t2[None, :], TM)
    fwd2 = jnp.stack([jnp.concatenate([c, -s], axis=1),
                      jnp.concatenate([s, c], axis=1)]).astype(BF16)
    inv2 = jnp.stack([jnp.concatenate([c, s], axis=1),
                      jnp.concatenate([-s, c], axis=1)]).astype(BF16)
    c, s = cs(t1[:, None] * f1[None, :], n1)
    inv1 = (jnp.stack([c, -s]) * (1.0 / n)).astype(BF16)
    return fwd1, tw, fwd2, inv2, inv1


def _fft_fwd(g, fwd1, tw, fwd2, cb):
    n1 = fwd1.shape[0] // 2
    a = jnp.dot(fwd1, g, preferred_element_type=F32)
    are = jnp.concatenate([a[:n1, c * TM:(c + 1) * TM] for c in range(cb)], axis=0)
    aim = jnp.concatenate([a[n1:, c * TM:(c + 1) * TM] for c in range(cb)], axis=0)
    are = are.reshape(cb, n1, TM)
    aim = aim.reshape(cb, n1, TM)
    twr, twi = tw[0][None], tw[1][None]
    pr = (are * twr - aim * twi).reshape(cb * n1, TM).astype(BF16)
    pi = (are * twi + aim * twr).reshape(cb * n1, TM).astype(BF16)
    return (jnp.dot(pr, fwd2[0], preferred_element_type=F32)
            + jnp.dot(pi, fwd2[1], preferred_element_type=F32))


def _fftconv_kernel(g_ref, x0_ref, k_ref, fb_ref, fwd1_ref, tw_ref, fwd2_ref, inv2_ref, inv1_ref, o_ref):
    cb = g_ref.shape[-1] // TM
    n1 = fwd1_ref.shape[0] // 2
    g = g_ref[0]
    tw = tw_ref[...]
    x = _fft_fwd(g, fwd1_ref[...], tw, fwd2_ref[...], cb)
    kf = k_ref[...]
    xr, xi = x[:, :TM], x[:, TM:]
    kr, ki = kf[:, :TM], kf[:, TM:]
    yr = (xr * kr - xi * ki).astype(BF16)
    yi = (xr * ki + xi * kr).astype(BF16)
    bc = (jnp.dot(yr, inv2_ref[0], preferred_element_type=F32)
          + jnp.dot(yi, inv2_ref[1], preferred_element_type=F32))
    br = bc[:, :TM].reshape(cb, n1, TM)
    bi = bc[:, TM:].reshape(cb, n1, TM)
    twr, twi = tw[0][None], tw[1][None]
    pr = br * twr + bi * twi
    pi = bi * twr - br * twi
    prl = jnp.concatenate([pr[c] for c in range(cb)], axis=1).astype(BF16)
    pil = jnp.concatenate([pi[c] for c in range(cb)], axis=1).astype(BF16)
    y = (jnp.dot(inv1_ref[0], prl, preferred_element_type=F32)
         + jnp.dot(inv1_ref[1], pil, preferred_element_type=F32))
    gf = g.astype(F32)
    o_ref[0] = (x0_ref[0].astype(F32) * (y + fb_ref[...] * gf)).astype(o_ref.dtype)


def _fft_conv(g, x0, kf, fb, consts):
    b, nt, w, _ = g.shape
    n1 = 2 * nt
    cb = FFT_CB
    g2 = g.reshape(b, nt, w * TM)
    x2 = x0.reshape(b, nt, w * TM)
    fwd1, tw, fwd2, inv2, inv1 = consts
    dspec = pl.BlockSpec((1, nt, cb * TM), lambda c, bi: (bi, 0, c))
    full = lambda a: pl.BlockSpec(a.shape, lambda c, bi: (0,) * a.ndim)
    out = pl.pallas_call(
        _fftconv_kernel,
        grid=(w // cb, b),
        in_specs=[dspec, dspec,
                  pl.BlockSpec((cb * n1, 2 * TM), lambda c, bi: (c, 0)),
                  pl.BlockSpec((1, cb * TM), lambda c, bi: (0, c)),
                  full(fwd1), full(tw), full(fwd2), full(inv2), full(inv1)],
        out_specs=dspec,
        out_shape=jax.ShapeDtypeStruct((b, nt, w * TM), BF16),
        compiler_params=_params("arbitrary", "arbitrary"),
        name="long_conv",
    )(g2, x2, kf, fb, fwd1, tw, fwd2, inv2, inv1)
    return out.reshape(b, nt, w, TM)


def _spectrum_kernel(hf_ref, hb_ref, fwd1_ref, tw_ref, fwd2_ref, k_ref):
    cb = hf_ref.shape[-1] // TM
    tw = tw_ref[...]
    xf = _fft_fwd(hf_ref[...].astype(BF16), fwd1_ref[...], tw, fwd2_ref[...], cb)
    xb = _fft_fwd(hb_ref[...].astype(BF16), fwd1_ref[...], tw, fwd2_ref[...], cb)
    k_ref[...] = jnp.concatenate([xf[:, :TM] + xb[:, :TM], xf[:, TM:] - xb[:, TM:]], axis=1)


def _filter_spectrum(hf, hb, consts):
    nt, w, _ = hf.shape
    n1 = 2 * nt
    cb = FFT_CB
    fwd1, tw, fwd2, _, _ = consts
    hspec = pl.BlockSpec((nt, cb * TM), lambda c: (0, c))
    full = lambda a: pl.BlockSpec(a.shape, lambda c: (0,) * a.ndim)
    return pl.pallas_call(
        _spectrum_kernel,
        grid=(w // cb,),
        in_specs=[hspec, hspec, full(fwd1), full(tw), full(fwd2)],
        out_specs=pl.BlockSpec((cb * n1, 2 * TM), lambda c: (c, 0)),
        out_shape=jax.ShapeDtypeStruct((w * n1, 2 * TM), F32),
        compiler_params=_params("arbitrary"),
        name="filter_spectrum",
    )(hf.reshape(nt, w * TM), hb.reshape(nt, w * TM), fwd1, tw, fwd2)


def _filter_kernel(z_ref, t_ref, w1_ref, b1_ref, w2_ref, b2_ref, fr_ref, w3f_ref, w3b_ref, dl_ref,
                   hf_ref, hb_ref):
    nt = hf_ref.shape[0]
    dot = functools.partial(jnp.dot, preferred_element_type=F32, precision=HIGHEST)
    fr = fr_ref[...]
    h = jnp.sin(fr * (dot(w1_ref[...], z_ref[...]) + b1_ref[...]))
    h = jnp.sin(fr * (dot(w2_ref[...], h) + b2_ref[...]))
    decay = jnp.exp(-t_ref[...] * jnp.abs(dl_ref[...]))
    hf = dot(w3f_ref[...], h) * decay
    hb = dot(w3b_ref[...], h) * decay
    norm = (jnp.sum(jnp.abs(hf), axis=1, keepdims=True)
            + jnp.sum(jnp.abs(hb), axis=1, keepdims=True))
    hf = hf / norm
    hb = hb / norm
    lane = lax.broadcasted_iota(I32, hb.shape, 1)
    hb = jnp.where(lane == 0, 0.0, hb)
    for i in range(nt):
        hf_ref[i] = hf[:, i * TM:(i + 1) * TM]
        hb_ref[i] = hb[:, i * TM:(i + 1) * TM]


def _implicit_filters(l, fw1, fb1, fw2, fb2, fw3, ffreq):
    nt = l // TM
    w = HYENA_WIDTH
    t = jnp.linspace(0.0, 1.0, l, dtype=F32)[None, :]
    bands = (FILTER_EMB_DIM - 1) // 2
    wv = 2.0 * math.pi * jnp.arange(l, dtype=F32)[None, :] / l
    f = jnp.linspace(1e-4, bands - 1, bands, dtype=F32)[:, None]
    z = jnp.concatenate([t, jnp.cos(f * wv), -jnp.sin(f * wv)], axis=0)
    min_decay = math.log(DECAY_TARGET) / SLOW_DECAY_PCT
    max_decay = math.log(DECAY_TARGET) / FAST_DECAY_PCT
    deltas = jnp.linspace(min_decay, max_decay, w, dtype=F32)[:, None]
    w3t = fw3.T
    nc = w // FILT_CB
    col = lambda a: a.reshape(-1, 1)
    full = lambda a: pl.BlockSpec(a.shape, lambda c: (0,) * a.ndim)
    args = (z, t, fw1.T, col(fb1), fw2.T, col(fb2), col(ffreq))
    ospec = pl.BlockSpec((nt, FILT_CB, TM), lambda c: (0, c, 0))
    return pl.pallas_call(
        _filter_kernel,
        grid=(nc,),
        in_specs=[full(a) for a in args] + [
            pl.BlockSpec((FILT_CB, FILTER_HIDDEN), lambda c: (c, 0)),
            pl.BlockSpec((FILT_CB, FILTER_HIDDEN), lambda c: (nc + c, 0)),
            pl.BlockSpec((FILT_CB, 1), lambda c: (c, 0))],
        out_specs=[ospec, ospec],
        out_shape=[jax.ShapeDtypeStruct((nt, w, TM), F32)] * 2,
        compiler_params=_params("arbitrary"),
        name="implicit_filter",
    )(*args, w3t, w3t, deltas)


def _dense_dft_consts():
    n = 2 * TM
    two_pi = 2.0 * math.pi
    t = jnp.arange(TM, dtype=I32)
    f = jnp.arange(n, dtype=I32)
    ang = ((t[:, None] * f[None, :]) % n).astype(F32) * (two_pi / n)
    fwd = jnp.concatenate([jnp.cos(ang), -jnp.sin(ang)], axis=1).astype(BF16)
    inv = (jnp.stack([jnp.cos(ang).T, -jnp.sin(ang).T]) * (1.0 / n)).astype(BF16)
    return fwd, inv


def _hyena_tile_kernel(u_ref, w_ref, b_ref, hf_ref, hb_ref, fb_ref, fwd_ref, inv_ref, o_ref):
    w = HYENA_WIDTH
    n = 2 * TM
    x0 = _short_conv_tiles(u_ref.at[:, :, 0:w], w_ref[0], b_ref[0])[0]
    x1 = _short_conv_tiles(u_ref.at[:, :, w:2 * w], w_ref[1], b_ref[1])[0]
    v = _short_conv_tiles(u_ref.at[:, :, 2 * w:3 * w], w_ref[2], b_ref[2])[0]
    g = (v * x1).astype(BF16)
    fwd = fwd_ref[...]
    dot = functools.partial(jnp.dot, preferred_element_type=F32)
    kf = dot(hf_ref[0].astype(BF16), fwd)
    kb = dot(hb_ref[0].astype(BF16), fwd)
    kr = kf[:, :n] + kb[:, :n]
    ki = kf[:, n:] - kb[:, n:]
    x = dot(g, fwd)
    xr, xi = x[:, :n], x[:, n:]
    yr = (xr * kr - xi * ki).astype(BF16)
    yi = (xr * ki + xi * kr).astype(BF16)
    y = dot(yr, inv_ref[0]) + dot(yi, inv_ref[1])
    o_ref[0, 0] = (x0 * (y + fb_ref[...] * g.astype(F32))).astype(o_ref.dtype)


def _hyena_one_tile(u, cw, cb, hf, hb, fbias, consts):
    b = u.shape[0]
    w = HYENA_WIDTH
    fwd, inv = consts
    full = lambda a: pl.BlockSpec(a.shape, lambda bi: (0,) * a.ndim)
    return pl.pallas_call(
        _hyena_tile_kernel,
        grid=(b,),
        in_specs=[pl.BlockSpec((1, 1, 3 * w, TM), lambda bi: (bi, 0, 0, 0)),
                  full(cw), full(cb), full(hf), full(hb), full(fbias), full(fwd), full(inv)],
        out_specs=pl.BlockSpec((1, 1, w, TM), lambda bi: (bi, 0, 0, 0)),
        out_shape=jax.ShapeDtypeStruct((b, 1, w, TM), BF16),
        compiler_params=_params("arbitrary"),
        name="hyena_context",
    )(u, cw, cb, hf, hb, fbias, fwd, inv)


def _top2_sum(a, b, c, d):
    hi1, lo1 = jnp.maximum(a, b), jnp.minimum(a, b)
    hi2, lo2 = jnp.maximum(c, d), jnp.minimum(c, d)
    return jnp.maximum(hi1, hi2) + jnp.maximum(jnp.minimum(hi1, hi2), jnp.maximum(lo1, lo2))


def _outproj_kernel(a_ref, y_ref, x_ref, mod_ref, ga_ref, gy_ref, wo_ref, n2_ref, rw_ref, rb_ref,
                    cnt_ref, tri_ref, xo_ref, row_ref, route_ref, cnto_ref, cnt_sc):
    first = jnp.logical_and(pl.program_id(0) == 0, pl.program_id(1) == 0)

    @pl.when(first)
    def _():
        cnt_sc[...] = cnt_ref[...]

    def group_norm(t, gain):
        t = t.astype(F32)
        return t * lax.rsqrt(jnp.mean(t * t, axis=0, keepdims=True) + EPS) * gain

    mix = jnp.concatenate([group_norm(a_ref[0, 0], ga_ref[...]),
                           group_norm(y_ref[0, 0], gy_ref[...])], axis=0).astype(BF16)
    o = lax.dot_general(mix, wo_ref[...], (((0,), (0,)), ((), ())), preferred_element_type=F32)
    xn = x_ref[0] + mod_ref[0, 2:3, :] * o
    xo_ref[0] = xn
    h2 = _norm_mod(xn, n2_ref[...], mod_ref[0, 3:4, :], mod_ref[0, 4:5, :])

    tm = xn.shape[0]
    nt_dims = (((1,), (1,)), ((), ()))
    h_hi = h2.astype(BF16)
    h_lo = (h2 - h_hi.astype(F32)).astype(BF16)
    rw = rw_ref[...]
    part = lax.dot_general(rw, h_hi, nt_dims, preferred_element_type=F32)
    logits = (part[:N_EXPERTS] + part[N_EXPERTS:]
              + lax.dot_general(rw[:N_EXPERTS], h_lo, nt_dims, preferred_element_type=F32))
    score = jax.nn.sigmoid(logits)
    sel = score + rb_ref[...]
    srow = [sel[e:e + 1] for e in range(N_EXPERTS)]
    prow = [score[e:e + 1] for e in range(N_EXPERTS)]
    epg = EXPERTS_PER_GROUP
    gs = [_top2_sum(*srow[g * epg:(g + 1) * epg]) for g in range(N_EXPERT_GROUPS)]
    bg = jnp.zeros((1, tm), I32)
    best = gs[0]
    for g in range(1, N_EXPERT_GROUPS):
        upd = gs[g] > best
        bg = jnp.where(upd, g, bg)
        best = jnp.where(upd, gs[g], best)

    def pick_group(rows, j):
        out = rows[j]
        for g in range(1, N_EXPERT_GROUPS):
            out = jnp.where(bg == g, rows[g * epg + j], out)
        return out

    cand = [pick_group(srow, j) for j in range(epg)]
    cprob = [pick_group(prow, j) for j in range(epg)]
    i1 = jnp.zeros((1, tm), I32)
    v1, w1 = cand[0], cprob[0]
    for j in range(1, epg):
        upd = cand[j] > v1
        i1 = jnp.where(upd, j, i1)
        v1 = jnp.where(upd, cand[j], v1)
        w1 = jnp.where(upd, cprob[j], w1)
    i2 = jnp.zeros((1, tm), I32)
    v2 = jnp.full((1, tm), -jnp.inf, F32)
    w2 = jnp.zeros((1, tm), F32)
    for j in range(epg):
        upd = jnp.logical_and(i1 != j, cand[j] > v2)
        i2 = jnp.where(upd, j, i2)
        v2 = jnp.where(upd, cand[j], v2)
        w2 = jnp.where(upd, cprob[j], w2)
    den = w1 + w2
    w1, w2 = w1 / den, w2 / den
    first_lower = i1 < i2
    lo = jnp.where(first_lower, i1, i2)
    hi = jnp.where(first_lower, i2, i1)
    w_lo = jnp.where(first_lower, w1, w2)
    w_hi = jnp.where(first_lower, w2, w1)
    pair = jnp.where(lo == 0, hi - 1, jnp.where(lo == 1, hi + 1, PAIRS_PER_GROUP - 1))
    cls = bg * PAIRS_PER_GROUP + pair

    cio = lax.broadcasted_iota(I32, (N_CLASS_ROWS, tm), 0)
    onehot = jnp.where(cio == cls, 1.0, 0.0)
    cum = jnp.dot(onehot.astype(BF16), tri_ref[...], preferred_element_type=F32)
    base = cnt_sc[...]
    rank = jnp.sum(onehot * (base + cum), axis=0, keepdims=True)
    route_ref[0, 0] = jnp.concatenate([cls, rank.astype(I32)], axis=0)
    new = base + jnp.sum(onehot, axis=1, keepdims=True)
    cnt_sc[...] = new
    cnto_ref[...] = new

    half = h2.shape[1] // 2
    wrows = jnp.concatenate([w_lo, w_hi, jnp.zeros((ROW_EXTRA - 2, tm), F32)], axis=0)
    row_ref[0] = jnp.concatenate([_pack_bf16_pair(h2[:, :half], h2[:, half:]),
                                  lax.bitcast_convert_type(wrows.T, U32)], axis=1)


def _outproj_route(a, y, x, mods, mod_row, ga, gy, wo, n2g, rwt, rb, cnt, tri):
    b, l, d = x.shape
    nt = l // TM
    if mod_row is None:
        mod_map = lambda bi, i: (bi, 0, 0)
    else:
        mod_map = lambda bi, i: (mod_row, 0, 0)
    full = lambda arr: pl.BlockSpec(arr.shape, lambda bi, i: (0,) * arr.ndim)
    tile = pl.BlockSpec((1, 1, ATTN_WIDTH, TM), lambda bi, i: (bi, i, 0, 0))
    xspec = pl.BlockSpec((1, TM, d), lambda bi, i: (bi, i, 0))
    rspec = pl.BlockSpec((1, 1, 2, TM), lambda bi, i: (bi, i, 0, 0))
    row_w = d // 2 + ROW_EXTRA
    return pl.pallas_call(
        _outproj_kernel,
        grid=(b, nt),
        in_specs=[tile, tile, xspec, pl.BlockSpec((1, N_MOD, d), mod_map),
                  full(ga), full(gy), full(wo), full(n2g), full(rwt), full(rb), full(cnt), full(tri)],
        out_specs=[xspec, pl.BlockSpec((1, TM, row_w), lambda bi, i: (bi, i, 0)), rspec, full(cnt)],
        out_shape=[jax.ShapeDtypeStruct((b, l, d), F32),
                   jax.ShapeDtypeStruct((b, l, row_w), U32),
                   jax.ShapeDtypeStruct((b, nt, 2, TM), I32),
                   jax.ShapeDtypeStruct(cnt.shape, F32)],
        scratch_shapes=[pltpu.VMEM(cnt.shape, F32)],
        compiler_params=_params("arbitrary", "arbitrary"),
        name="outproj_route",
    )(a, y, x, mods, ga, gy, wo, n2g, rwt, rb, cnt, tri)


def _dispatch_kernel(slot_ref, h_ref, xs_in_ref, xs_ref, stage, sem, *, nstep):
    del xs_in_ref
    tm = h_ref.shape[0]
    step = pl.program_id(0)
    cur = lax.rem(step, 2)

    def copy(side, r, s):
        return pltpu.make_async_copy(stage.at[side, pl.ds(r, 1)], xs_ref.at[pl.ds(s, 1)], sem.at[side])

    def drain(side):
        def wait_one(r, carry):
            copy(side, 0, 0).wait()
            return carry

        lax.fori_loop(0, tm, wait_one, 0, unroll=8)

    @pl.when(step >= 2)
    def _():
        drain(cur)

    stage[cur] = h_ref[...]

    def issue(r, carry):
        copy(cur, r, slot_ref[0, 0, r]).start()
        return carry

    lax.fori_loop(0, tm, issue, 0, unroll=8)

    @pl.when(step == nstep - 1)
    def _():
        @pl.when(step >= 1)
        def _():
            drain(1 - cur)

        drain(cur)


def _dispatch(slots, h2, xs):
    t, d = h2.shape
    return pl.pallas_call(
        functools.partial(_dispatch_kernel, nstep=t // TM),
        grid=(t // TM,),
        in_specs=[pl.BlockSpec((1, 1, TM), lambda i: (i, 0, 0), memory_space=pltpu.SMEM),
                  pl.BlockSpec((TM, d), lambda i: (i, 0)),
                  pl.BlockSpec(memory_space=pl.ANY)],
        out_specs=pl.BlockSpec(memory_space=pl.ANY),
        out_shape=jax.ShapeDtypeStruct(xs.shape, xs.dtype),
        scratch_shapes=[pltpu.VMEM((2, TM, d), xs.dtype), pltpu.SemaphoreType.DMA((2,))],
        input_output_aliases={2: 0},
        compiler_params=_params("arbitrary"),
        name="moe_dispatch",
    )(slots, h2, xs)


def _ffn_kernel(ea_ref, eb_ref, nt_ref, xs_ref, wga_ref, wua_ref, wda_ref, wgb_ref, wub_ref, wdb_ref, ys_ref):
    del ea_ref, eb_ref
    live = pl.program_id(0) < nt_ref[0]
    half = ys_ref.shape[1]

    @pl.when(live)
    def _():
        words = xs_ref[...]
        lo, hi = _unpack_bf16_pair(words[:, :half])
        lo, hi = lo.astype(BF16), hi.astype(BF16)
        wts = lax.bitcast_convert_type(words[:, half:half + 2], F32)

        def expert(wg_ref, wu_ref, wd_ref):
            dot = functools.partial(jnp.dot, preferred_element_type=F32)
            g = dot(lo, wg_ref[0, :half, :]) + dot(hi, wg_ref[0, half:, :])
            u = dot(lo, wu_ref[0, :half, :]) + dot(hi, wu_ref[0, half:, :])
            a = (g * jax.nn.sigmoid(g)) * u
            return dot(a.astype(BF16), wd_ref[0])

        y = (wts[:, 0:1] * expert(wga_ref, wua_ref, wda_ref)
             + wts[:, 1:2] * expert(wgb_ref, wub_ref, wdb_ref))
        ys_ref[...] = _pack_bf16_pair(y[:, :half], y[:, half:])

    @pl.when(jnp.logical_not(live))
    def _():
        ys_ref[...] = jnp.zeros_like(ys_ref)


def _expert_ffn(tile_ea, tile_eb, n_tiles, xs, wg, wu, wd):
    nslot, row_w = xs.shape
    d, f = wg.shape[1:]
    ntile = nslot // FFN_TILE
    row = lambda i, ea, eb, nt: (jnp.minimum(i, nt[0] - 1), 0)
    amap = lambda i, ea, eb, nt: (ea[jnp.minimum(i, nt[0] - 1)], 0, 0)
    bmap = lambda i, ea, eb, nt: (eb[jnp.minimum(i, nt[0] - 1)], 0, 0)
    return pl.pallas_call(
        _ffn_kernel,
        grid_spec=pltpu.PrefetchScalarGridSpec(
            num_scalar_prefetch=3,
            grid=(ntile,),
            in_specs=[pl.BlockSpec((FFN_TILE, row_w), row),
                      pl.BlockSpec((1, d, f), amap), pl.BlockSpec((1, d, f), amap), pl.BlockSpec((1, f, d), amap),
                      pl.BlockSpec((1, d, f), bmap), pl.BlockSpec((1, d, f), bmap), pl.BlockSpec((1, f, d), bmap)],
            out_specs=pl.BlockSpec((FFN_TILE, d // 2), lambda i, ea, eb, nt: (i, 0))),
        out_shape=jax.ShapeDtypeStruct((nslot, d // 2), U32),
        compiler_params=_params("arbitrary"),
        name="moe_experts",
    )(tile_ea, tile_eb, n_tiles, xs, wg, wu, wd, wg, wu, wd)


def _combine_kernel(slot_ref, next_slot_ref, x_ref, mod_ref, ys_ref, o_ref, buf, sem, *, nt, nstep):
    tm = x_ref.shape[1]
    step = pl.program_id(0) * nt + pl.program_id(1)
    cur = lax.rem(step, 2)

    def copy(side, r, s):
        return pltpu.make_async_copy(ys_ref.at[pl.ds(s, 1)], buf.at[side, pl.ds(r, 1)], sem.at[side])

    def gather(side, slots):
        def issue(r, carry):
            copy(side, r, slots[0, 0, r]).start()
            return carry

        lax.fori_loop(0, tm, issue, 0, unroll=8)

    @pl.when(step == 0)
    def _():
        gather(cur, slot_ref)

    @pl.when(step + 1 < nstep)
    def _():
        gather(1 - cur, next_slot_ref)

    def wait_one(r, carry):
        copy(cur, 0, 0).wait()
        return carry

    lax.fori_loop(0, tm, wait_one, 0, unroll=8)
    lo, hi = _unpack_bf16_pair(buf[cur])
    y = jnp.concatenate([lo, hi], axis=1)
    o_ref[0] = x_ref[0] + mod_ref[0, 5:6, :] * y


def _combine(slots, x, mods, mod_row, ys):
    b, l, d = x.shape
    nt = l // TM
    if mod_row is None:
        mod_map = lambda bi, i: (bi, 0, 0)
    else:
        mod_map = lambda bi, i: (mod_row, 0, 0)
    xspec = pl.BlockSpec((1, TM, d), lambda bi, i: (bi, i, 0))
    last = b * nt - 1
    return pl.pallas_call(
        functools.partial(_combine_kernel, nt=nt, nstep=b * nt),
        grid=(b, nt),
        in_specs=[pl.BlockSpec((1, 1, TM), lambda bi, i: (bi * nt + i, 0, 0), memory_space=pltpu.SMEM),
                  pl.BlockSpec((1, 1, TM), lambda bi, i: (jnp.minimum(bi * nt + i + 1, last), 0, 0),
                               memory_space=pltpu.SMEM),
                  xspec,
                  pl.BlockSpec((1, N_MOD, d), mod_map),
                  pl.BlockSpec(memory_space=pl.ANY)],
        out_specs=xspec,
        out_shape=jax.ShapeDtypeStruct((b, l, d), F32),
        scratch_shapes=[pltpu.VMEM((2, TM, d // 2), U32), pltpu.SemaphoreType.DMA((2,))],
        compiler_params=_params("arbitrary", "arbitrary"),
        name="moe_combine",
    )(slots, slots, x, mods, ys)


def _moe(streams, wg, wu, wd):
    counts = streams[-1]["counts"].reshape(-1).astype(I32)
    padded = ((counts + FFN_TILE - 1) // FFN_TILE) * FFN_TILE
    ends = jnp.cumsum(padded)
    offs = ends - padded
    total = sum(s["x"].shape[0] * s["x"].shape[1] for s in streams)
    ntile = total // FFN_TILE + N_CLASSES
    nslot = ntile * FFN_TILE
    tile_start = jnp.arange(ntile, dtype=I32) * FFN_TILE
    tile_class = jnp.minimum(jnp.sum((tile_start[:, None] >= ends[None, :N_CLASSES]).astype(I32), axis=1),
                             N_CLASSES - 1)
    group_base = (tile_class // PAIRS_PER_GROUP) * EXPERTS_PER_GROUP
    tile_ea = (group_base + jnp.asarray(PAIR_LO, I32)[tile_class % PAIRS_PER_GROUP]).astype(I32)
    tile_eb = (group_base + jnp.asarray(PAIR_HI, I32)[tile_class % PAIRS_PER_GROUP]).astype(I32)
    n_tiles = (ends[-1] // FFN_TILE).astype(I32).reshape(1)

    row_w = streams[0]["rows"].shape[-1]
    xs = jnp.zeros((nslot, row_w), U32)
    for s in streams:
        b, l, _ = s["x"].shape
        cls, rank = s["route"][:, :, 0], s["route"][:, :, 1]
        slot = rank
        for c in range(N_CLASSES):
            slot = slot + jnp.where(cls == c, offs[c], 0)
        s["slots"] = slot.reshape(b * (l // TM), 1, TM)
        xs = _dispatch(s["slots"], s["rows"].reshape(b * l, row_w), xs)
    ys = _expert_ffn(tile_ea, tile_eb, n_tiles, xs, wg, wu, wd)
    return [_combine(s["slots"], s["x"], s["mods"], s["mod_row"], ys) for s in streams]


def _rope_table(l):
    t = jnp.arange(l)
    row = (t // GRID_W).astype(F32)
    col = (t % GRID_W).astype(F32)
    inv_freq = ROPE_THETA ** (-jnp.arange(ROPE_QUARTER, dtype=F32) / ROPE_QUARTER)
    ar = inv_freq[:, None] * row[None, :]
    ac = inv_freq[:, None] * col[None, :]
    return jnp.concatenate([jnp.cos(ar), jnp.sin(ar), jnp.cos(ac), jnp.sin(ac)], axis=0)


def _identity_rope_table(l):
    one = jnp.ones((ROPE_QUARTER, l), F32)
    zero = jnp.zeros((ROPE_QUARTER, l), F32)
    return jnp.concatenate([one, zero, one, zero], axis=0)


def kernel(x, c, ctx, c_ctx, w_mod, b_mod, norm1_g, w_in, q_norm_g, k_norm_g, conv_w, conv_b, filt_w1, filt_b1, filt_w2, filt_b2, filt_w3, filt_freq, filt_bias, attn_out_g, hyena_out_g, w_out, norm2_g, router_w, router_bias, expert_w_gate, expert_w_up, expert_w_down):
    depth = w_mod.shape[0]
    b, l, d = x.shape
    lc = ctx.shape[1]
    assert l % TM == 0 and lc == TM and d == D_MODEL
    nt = l // TM
    w = HYENA_WIDTH

    mods = _modulations(c, c_ctx, w_mod, b_mod)
    rope_lat = _rope_table(l)
    rope_ctx = _identity_rope_table(lc)
    dft = _dft_consts(nt)
    dft_ctx = _dense_dft_consts()
    tri = (jnp.arange(TM)[:, None] < jnp.arange(TM)[None, :]).astype(BF16)
    rwt = router_w.T
    rw_hi = rwt.astype(BF16)
    rwt = jnp.concatenate([rw_hi, (rwt - rw_hi.astype(F32)).astype(BF16)], axis=0)
    rb = router_bias.reshape(-1, 1)
    zero_cnt = jnp.zeros((N_CLASS_ROWS, 1), F32)
    col = lambda a: a.reshape(-1, 1)

    for li in range(depth):
        last = li == depth - 1
        m = mods[li]
        wt = w_in[li].T.astype(BF16)
        wo = w_out[li].astype(BF16)
        g1n = norm1_g[li].reshape(1, d)
        g2n = norm2_g[li].reshape(1, d)
        qg, kg = col(q_norm_g[li]), col(k_norm_g[li])
        cw = conv_w[li].T.reshape(3, w, SHORT_CONV)
        cb = conv_b[li].reshape(3, w, 1)
        ga, gy = col(attn_out_g[li]), col(hyena_out_g[li])
        fargs = (filt_w1[li], filt_b1[li], filt_w2[li], filt_b2[li], filt_w3[li], filt_freq[li])

        q, k, v, u = _inproj(x, m, None, g1n, wt, qg, kg, rope_lat)
        qc, kc, vc, uc = _inproj(ctx, m, b, g1n, wt, qg, kg, rope_ctx)
        a = _attention(q, jnp.concatenate([kc, k], axis=2), jnp.concatenate([vc, v], axis=2))
        hf, hb = _implicit_filters(l, *fargs)
        kf = _filter_spectrum(hf, hb, dft)
        g, x0 = _short_conv_gate(u, cw, cb)
        fb_lanes = jnp.repeat(filt_bias[li], TM).reshape(1, w * TM)
        y = _fft_conv(g, x0, kf, fb_lanes, dft)
        lat = dict(zip(("x", "rows", "route", "counts"),
                       _outproj_route(a, y, x, m, None, ga, gy, wo, g2n, rwt, rb, zero_cnt, tri)))
        lat.update(mods=m, mod_row=None)
        streams = [lat]

        if not last:
            ac = _attention(qc, kc, vc)
            hfc, hbc = _implicit_filters(lc, *fargs)
            yc = _hyena_one_tile(uc, cw, cb, hfc, hbc, col(filt_bias[li]), dft_ctx)
            cs = dict(zip(("x", "rows", "route", "counts"),
                          _outproj_route(ac, yc, ctx, m, b, ga, gy, wo, g2n, rwt, rb, lat["counts"], tri)))
            cs.update(mods=m, mod_row=b)
            streams.append(cs)

        outs = _moe(streams, expert_w_gate[li].astype(BF16), expert_w_up[li].astype(BF16),
                    expert_w_down[li].astype(BF16))
        x = outs[0]
        if not last:
            ctx = outs[1]
    return x
```

```python
import functools
import math

import jax
import jax.numpy as jnp
from jax import lax
from jax.experimental import pallas as pl
from jax.experimental.pallas import tpu as pltpu

F32 = jnp.float32
BF16 = jnp.bfloat16
I32 = jnp.int32
HIGHEST = lax.Precision.HIGHEST

D_MODEL = 1024
N_MOD = 6
EPS = 1e-6
N_Q_HEADS = 8
N_KV_HEADS = 2
HEAD_DIM = 64
KV_REP = N_Q_HEADS // N_KV_HEADS
ATTN_WIDTH = N_Q_HEADS * HEAD_DIM
KV_WIDTH = N_KV_HEADS * HEAD_DIM
ATTN_SCALE = HEAD_DIM ** -0.5
LOG2_E = math.log2(math.e)
V_ROWS = HEAD_DIM + 16
K_COLS = HEAD_DIM + 16
SCORE_BOUND_SLACK = 1.0 + 2.0 ** -6
UNDERFLOW_GUARD = 2.0 ** -80
GRID_W = 64
ROPE_THETA = 10000.0
ROPE_QUARTER = HEAD_DIM // 4
HYENA_WIDTH = D_MODEL - ATTN_WIDTH
SHORT_CONV = 3
FILTER_EMB_DIM = 33
FILTER_HIDDEN = 64
DECAY_TARGET = 1e-2
FAST_DECAY_PCT = 0.3
SLOW_DECAY_PCT = 1.5
N_EXPERTS = 16
N_EXPERT_GROUPS = 4
EXPERTS_PER_GROUP = N_EXPERTS // N_EXPERT_GROUPS
D_FF_EXPERT = 512

TM = 256
ATTN_UNROLL = 8
ATTN_TQ = 1024
FFN_TILE = 256
CONV_CB = 128
FFT_CB = 32
FILT_CB = 128
VMEM_LIMIT = 56 * 1024 * 1024


U32 = jnp.uint32
PAIRS_PER_GROUP = EXPERTS_PER_GROUP * (EXPERTS_PER_GROUP - 1) // 2
N_CLASSES = N_EXPERT_GROUPS * PAIRS_PER_GROUP
N_CLASS_ROWS = -(-N_CLASSES // 8) * 8
PAIR_LO = tuple(a for a in range(EXPERTS_PER_GROUP) for b in range(a + 1, EXPERTS_PER_GROUP))
PAIR_HI = tuple(b for a in range(EXPERTS_PER_GROUP) for b in range(a + 1, EXPERTS_PER_GROUP))
ROW_EXTRA = 128


def _pack_bf16_pair(lo, hi):
    lo_bits = lax.bitcast_convert_type(lo.astype(BF16).astype(F32), U32)
    hi_bits = lax.bitcast_convert_type(hi.astype(BF16).astype(F32), U32)
    return lax.shift_right_logical(lo_bits, U32(16)) | (hi_bits & U32(0xFFFF0000))


def _unpack_bf16_pair(words):
    lo = lax.bitcast_convert_type(lax.shift_left(words, U32(16)), F32)
    hi = lax.bitcast_convert_type(words & U32(0xFFFF0000), F32)
    return lo, hi


def _params(*sem):
    return pltpu.CompilerParams(dimension_semantics=tuple(sem), vmem_limit_bytes=VMEM_LIMIT)


def _norm_mod(x, g, shift, scale):
    y = x * lax.rsqrt(jnp.mean(x * x, axis=-1, keepdims=True) + EPS)
    return (y * g) * (1 + scale) + shift


def _mod_kernel(c_ref, w_ref, b_ref, o_ref):
    c = c_ref[...]
    s = c * jax.nn.sigmoid(c)
    o_ref[0] = jnp.dot(s, w_ref[0], preferred_element_type=F32, precision=HIGHEST) + b_ref[0]


def _modulations(c, c_ctx, w_mod, b_mod):
    depth, d, nmd = w_mod.shape
    b = c.shape[0]
    rows = -(-(b + 1) // 8) * 8
    c_all = jnp.zeros((rows, d), F32).at[:b].set(c).at[b].set(c_ctx)
    tn = nmd // 4
    out = pl.pallas_call(
        _mod_kernel,
        grid=(depth, nmd // tn),
        in_specs=[pl.BlockSpec((rows, d), lambda l, j: (0, 0)),
                  pl.BlockSpec((1, d, tn), lambda l, j: (l, 0, j)),
                  pl.BlockSpec((1, 1, tn), lambda l, j: (l, 0, j))],
        out_specs=pl.BlockSpec((1, rows, tn), lambda l, j: (l, 0, j)),
        out_shape=jax.ShapeDtypeStruct((depth, rows, nmd), F32),
        compiler_params=_params("arbitrary", "arbitrary"),
        name="modulation",
    )(c_all, w_mod, b_mod.reshape(depth, 1, nmd))
    return out.reshape(depth, rows, N_MOD, d)


def _inproj_kernel(x_ref, mod_ref, g_ref, wt_ref, qg_ref, kg_ref, rope_ref,
                   q_ref, k_ref, v_ref, u_ref):
    x = x_ref[0]
    tm = x.shape[0]
    h = _norm_mod(x, g_ref[...], mod_ref[0, 0:1, :], mod_ref[0, 1:2, :])
    pt = lax.dot_general(wt_ref[...], h.astype(BF16), (((1,), (1,)), ((), ())),
                         preferred_element_type=F32)
    rope = rope_ref[...]
    qd = ROPE_QUARTER
    cr, sr = rope[0:qd][None], rope[qd:2 * qd][None]
    cc, sc = rope[2 * qd:3 * qd][None], rope[3 * qd:4 * qd][None]

    def norm_rope(t, gain, nh):
        t = t.reshape(nh, HEAD_DIM, tm)
        t = t * lax.rsqrt(jnp.mean(t * t, axis=1, keepdims=True) + EPS) * gain[None]
        a, b = t[:, 0:qd], t[:, qd:2 * qd]
        c, d = t[:, 2 * qd:3 * qd], t[:, 3 * qd:4 * qd]
        return jnp.concatenate([a * cr - b * sr, b * cr + a * sr,
                                c * cc - d * sc, d * cc + c * sc], axis=1)

    q = norm_rope(pt[0:ATTN_WIDTH], qg_ref[...], N_Q_HEADS) * (ATTN_SCALE * LOG2_E)
    q_ref[0] = q.astype(BF16)
    k = norm_rope(pt[ATTN_WIDTH:ATTN_WIDTH + KV_WIDTH], kg_ref[...], N_KV_HEADS)
    kt = k.reshape(KV_WIDTH, tm).T
    for g in range(N_KV_HEADS):
        k_ref[0, g, 0, :, 0:HEAD_DIM] = kt[:, g * HEAD_DIM:(g + 1) * HEAD_DIM].astype(BF16)
    pad_col = lax.broadcasted_iota(I32, (N_KV_HEADS, tm, K_COLS - HEAD_DIM), 2)
    k_ref[0, :, 0, :, HEAD_DIM:K_COLS] = jnp.where(pad_col == 0, 1.0, 0.0).astype(BF16)
    v = pt[ATTN_WIDTH + KV_WIDTH:ATTN_WIDTH + 2 * KV_WIDTH]
    v_ref[0, :, 0, 0:HEAD_DIM] = v.reshape(N_KV_HEADS, HEAD_DIM, tm).astype(BF16)
    pad_row = lax.broadcasted_iota(I32, (N_KV_HEADS, V_ROWS - HEAD_DIM, tm), 1)
    v_ref[0, :, 0, HEAD_DIM:V_ROWS] = jnp.where(pad_row == 0, 1.0, 0.0).astype(BF16)
    u_ref[0, 0] = pt[ATTN_WIDTH + 2 * KV_WIDTH:].astype(BF16)


def _inproj(x, mods, mod_row, g, wt, qg, kg, rope):
    b, l, d = x.shape
    nt = l // TM
    p = wt.shape[0]
    uw = p - ATTN_WIDTH - 2 * KV_WIDTH
    if mod_row is None:
        mod_map = lambda bi, i: (bi, 0, 0)
    else:
        mod_map = lambda bi, i: (mod_row, 0, 0)
    return pl.pallas_call(
        _inproj_kernel,
        grid=(b, nt),
        in_specs=[pl.BlockSpec((1, TM, d), lambda bi, i: (bi, i, 0)),
                  pl.BlockSpec((1, N_MOD, d), mod_map),
                  pl.BlockSpec((1, d), lambda bi, i: (0, 0)),
                  pl.BlockSpec((p, d), lambda bi, i: (0, 0)),
                  pl.BlockSpec((HEAD_DIM, 1), lambda bi, i: (0, 0)),
                  pl.BlockSpec((HEAD_DIM, 1), lambda bi, i: (0, 0)),
                  pl.BlockSpec((HEAD_DIM, TM), lambda bi, i: (0, i))],
        out_specs=[pl.BlockSpec((1, N_Q_HEADS, HEAD_DIM, TM), lambda bi, i: (bi, 0, 0, i)),
                   pl.BlockSpec((1, N_KV_HEADS, 1, TM, K_COLS), lambda bi, i: (bi, 0, i, 0, 0)),
                   pl.BlockSpec((1, N_KV_HEADS, 1, V_ROWS, TM), lambda bi, i: (bi, 0, i, 0, 0)),
                   pl.BlockSpec((1, 1, uw, TM), lambda bi, i: (bi, i, 0, 0))],
        out_shape=[jax.ShapeDtypeStruct((b, N_Q_HEADS, HEAD_DIM, l), BF16),
                   jax.ShapeDtypeStruct((b, N_KV_HEADS, nt, TM, K_COLS), BF16),
                   jax.ShapeDtypeStruct((b, N_KV_HEADS, nt, V_ROWS, TM), BF16),
                   jax.ShapeDtypeStruct((b, nt, uw, TM), BF16)],
        compiler_params=_params("arbitrary", "arbitrary"),
        name="inproj",
    )(x, mods, g, wt, qg, kg, rope)


def _attn_kernel(q_ref, k_ref, v_ref, kmax_ref, o_ref, acc_ref, m_ref):
    nk = k_ref.shape[2]
    tq = q_ref.shape[-1]
    nq = KV_REP * tq
    q = jnp.concatenate([q_ref[0, r] for r in range(KV_REP)], axis=1)
    pad = jnp.zeros((K_COLS - HEAD_DIM - 1, nq), F32)

    def write_out():
        acc = acc_ref[...]
        out = acc[0:HEAD_DIM] / acc[HEAD_DIM:HEAD_DIM + 1]
        for r in range(KV_REP):
            for t in range(tq // TM):
                o_ref[0, t, r * HEAD_DIM:(r + 1) * HEAD_DIM, :] = (
                    out[:, r * tq + t * TM:r * tq + (t + 1) * TM].astype(o_ref.dtype))

    def run(chunk):
        def group(i, carry):
            for t in range(ATTN_UNROLL):
                chunk(ATTN_UNROLL * i + t)
            return carry

        lax.fori_loop(0, nk // ATTN_UNROLL, group, 0)
        for j in range(nk - nk % ATTN_UNROLL, nk):
            chunk(j)

    qf = q.astype(F32)
    qnorm = jnp.sqrt(jnp.sum(qf * qf, axis=0, keepdims=True))
    bound = qnorm * (kmax_ref[0, 0][:, 0:1] * SCORE_BOUND_SLACK)
    q_shift = jnp.concatenate([q, jnp.concatenate([-bound, pad], axis=0).astype(BF16)], axis=0)
    acc_ref[...] = jnp.zeros_like(acc_ref)

    def chunk_shifted(j):
        s = jnp.dot(k_ref[0, 0, j], q_shift, preferred_element_type=F32)
        p = jnp.exp2(s).astype(BF16)
        acc_ref[...] += jnp.dot(v_ref[0, 0, j], p, preferred_element_type=F32)

    run(chunk_shifted)
    denom_ok = jnp.min(acc_ref[HEAD_DIM:HEAD_DIM + 1, :]) >= UNDERFLOW_GUARD

    @pl.when(denom_ok)
    def _():
        write_out()

    @pl.when(jnp.logical_not(denom_ok))
    def _():
        q_plain = jnp.concatenate([q, jnp.zeros((K_COLS - HEAD_DIM, nq), BF16)], axis=0)
        acc_ref[...] = jnp.zeros_like(acc_ref)
        m_ref[...] = jnp.full_like(m_ref, -jnp.inf)

        def chunk_online(j):
            s = jnp.dot(k_ref[0, 0, j], q_plain, preferred_element_type=F32)
            m = m_ref[...]
            m_new = jnp.maximum(m, jnp.max(s, axis=0, keepdims=True))
            alpha = jnp.exp2(m - m_new)
            p = jnp.exp2(s - m_new).astype(BF16)
            acc_ref[...] = alpha * acc_ref[...] + jnp.dot(v_ref[0, 0, j], p, preferred_element_type=F32)
            m_ref[...] = m_new

        run(chunk_online)
        write_out()


def _attention(q, k, v):
    b, _, _, l = q.shape
    nk = k.shape[2]
    nt = l // TM
    tq = min(ATTN_TQ, l)
    kf = k[..., :HEAD_DIM].astype(F32)
    kmax = jnp.sqrt(jnp.max(jnp.sum(kf * kf, axis=-1), axis=(2, 3)))
    kmax = jnp.broadcast_to(kmax[:, :, None, None], (b, N_KV_HEADS, 1, 128))
    return pl.pallas_call(
        _attn_kernel,
        grid=(b, N_KV_HEADS, l // tq),
        in_specs=[pl.BlockSpec((1, KV_REP, HEAD_DIM, tq), lambda bi, g, i: (bi, g, 0, i)),
                  pl.BlockSpec((1, 1, nk, TM, K_COLS), lambda bi, g, i: (bi, g, 0, 0, 0)),
                  pl.BlockSpec((1, 1, nk, V_ROWS, TM), lambda bi, g, i: (bi, g, 0, 0, 0)),
                  pl.BlockSpec((1, 1, 1, 128), lambda bi, g, i: (bi, g, 0, 0))],
        out_specs=pl.BlockSpec((1, tq // TM, KV_REP * HEAD_DIM, TM), lambda bi, g, i: (bi, i, g, 0)),
        out_shape=jax.ShapeDtypeStruct((b, nt, ATTN_WIDTH, TM), BF16),
        scratch_shapes=[pltpu.VMEM((V_ROWS, KV_REP * tq), F32), pltpu.VMEM((1, KV_REP * tq), F32)],
        compiler_params=_params("arbitrary", "arbitrary", "arbitrary"),
        name="attention",
    )(q, k, v, kmax)


def _short_conv_tiles(u_ref, w, bias):
    nt, c, tm = u_ref.shape[1:]
    lane = lax.broadcasted_iota(I32, (c, tm), 1)
    tiles = [u_ref[0, i].astype(F32) for i in range(nt)]
    prev = [pltpu.roll(t, 1, 1) for t in tiles]
    nxt = [pltpu.roll(t, tm - 1, 1) for t in tiles]
    zero = jnp.zeros((c, tm), F32)
    out = []
    for i in range(nt):
        up = jnp.where(lane == 0, prev[i - 1] if i > 0 else zero, prev[i])
        un = jnp.where(lane == tm - 1, nxt[i + 1] if i < nt - 1 else zero, nxt[i])
        out.append(bias + up * w[:, 0:1] + tiles[i] * w[:, 1:2] + un * w[:, 2:3])
    return out


def _sconv_kernel(u0_ref, u1_ref, u2_ref, w_ref, b_ref, g_ref, x0_ref):
    x0 = _short_conv_tiles(u0_ref, w_ref[0], b_ref[0])
    x1 = _short_conv_tiles(u1_ref, w_ref[1], b_ref[1])
    v = _short_conv_tiles(u2_ref, w_ref[2], b_ref[2])
    for i in range(len(x0)):
        g_ref[0, i] = (v[i] * x1[i]).astype(g_ref.dtype)
        x0_ref[0, i] = x0[i].astype(x0_ref.dtype)


def _short_conv_gate(u, cw, cb):
    b, nt, _, _ = u.shape
    w = HYENA_WIDTH
    nc = w // CONV_CB
    uspec = lambda grp: pl.BlockSpec((1, nt, CONV_CB, TM), lambda bi, c: (bi, 0, grp * nc + c, 0))
    ospec = pl.BlockSpec((1, nt, CONV_CB, TM), lambda bi, c: (bi, 0, c, 0))
    return pl.pallas_call(
        _sconv_kernel,
        grid=(b, nc),
        in_specs=[uspec(0), uspec(1), uspec(2),
                  pl.BlockSpec((3, CONV_CB, SHORT_CONV), lambda bi, c: (0, c, 0)),
                  pl.BlockSpec((3, CONV_CB, 1), lambda bi, c: (0, c, 0))],
        out_specs=[ospec, ospec],
        out_shape=[jax.ShapeDtypeStruct((b, nt, w, TM), BF16)] * 2,
        compiler_params=_params("arbitrary", "arbitrary"),
        name="short_conv",
    )(u, u, u, cw, cb)


def _dft_consts(nt):
    n1 = 2 * nt
    n = n1 * TM
    two_pi = 2.0 * math.pi

    def cs(prod, mod):
        ang = (prod % mod).astype(F32) * (two_pi / mod)
        return jnp.cos(ang), jnp.sin(ang)

    f1 = jnp.arange(n1, dtype=I32)
    t1 = jnp.arange(nt, dtype=I32)
    t2 = jnp.arange(TM, dtype=I32)
    c, s = cs(f1[:, None] * t1[None, :], n1)
    fwd1 = jnp.concatenate([c, -s], axis=0).astype(BF16)
    fwd1c = jnp.concatenate([jnp.concatenate([c, s], axis=1),
                             jnp.concatenate([-s, c], axis=1)], axis=0).astype(BF16)
    c, s = cs(f1[:, None] * t2[None, :], n)
    tw = jnp.stack([c, -s])
    c, s = cs(t2[:, None] * t2[None, :], TM)
    fwd2 = jnp.stack([jnp.concatenate([c, -s], axis=1),
                      jnp.concatenate([s, c], axis=1)]).astype(BF16)
    inv2 = jnp.stack([jnp.concatenate([c, s], axis=1),
                      jnp.concatenate([-s, c], axis=1)]).astype(BF16)
    c, s = cs(t1[:, None] * f1[None, :], n1)
    inv1c = (jnp.stack([jnp.concatenate([c, s], axis=0),
                        jnp.concatenate([-s, c], axis=0)]) * (1.0 / n)).astype(BF16)
    return fwd1, tw, fwd2, inv2, fwd1c, inv1c


def _fft_fwd(g, fwd1, tw, fwd2, cb):
    n1 = fwd1.shape[0] // 2
    a = jnp.dot(fwd1, g, preferred_element_type=F32)
    are = jnp.concatenate([a[:n1, c * TM:(c + 1) * TM] for c in range(cb)], axis=0)
    aim = jnp.concatenate([a[n1:, c * TM:(c + 1) * TM] for c in range(cb)], axis=0)
    are = are.reshape(cb, n1, TM)
    aim = aim.reshape(cb, n1, TM)
    twr, twi = tw[0][None], tw[1][None]
    pr = (are * twr - aim * twi).reshape(cb * n1, TM).astype(BF16)
    pi = (are * twi + aim * twr).reshape(cb * n1, TM).astype(BF16)
    return (jnp.dot(pr, fwd2[0], preferred_element_type=F32)
            + jnp.dot(pi, fwd2[1], preferred_element_type=F32))


def _fftconv_kernel(g_ref, x0_ref, k_ref, fb_ref, fwd1_ref, tw_ref, fwd2_ref, inv2_ref, inv1_ref, o_ref):
    cb = g_ref.shape[-1] // TM
    n1 = fwd1_ref.shape[0] // 2
    nt = g_ref.shape[1]
    g = jnp.concatenate([g_ref[0], g_ref[1]], axis=0)
    tw = tw_ref[...]
    x = _fft_fwd(g, fwd1_ref[...], tw, fwd2_ref[...], cb)
    kf = k_ref[...]
    xr, xi = x[:, :TM], x[:, TM:]
    kr, ki = kf[:, :TM], kf[:, TM:]
    yr = (xr * kr - xi * ki).astype(BF16)
    yi = (xr * ki + xi * kr).astype(BF16)
    bc = (jnp.dot(yr, inv2_ref[0], preferred_element_type=F32)
          + jnp.dot(yi, inv2_ref[1], preferred_element_type=F32))
    br = bc[:, :TM].reshape(cb, n1, TM)
    bi = bc[:, TM:].reshape(cb, n1, TM)
    twr, twi = tw[0][None], tw[1][None]
    pr = br * twr + bi * twi
    pi = bi * twr - br * twi
    prl = jnp.concatenate([pr[c] for c in range(cb)], axis=1).astype(BF16)
    pil = jnp.concatenate([pi[c] for c in range(cb)], axis=1).astype(BF16)
    y = (jnp.dot(inv1_ref[0], prl, preferred_element_type=F32)
         + jnp.dot(inv1_ref[1], pil, preferred_element_type=F32))
    gf = g.astype(F32)
    for part in range(2):
        rows = slice(part * nt, (part + 1) * nt)
        o_ref[part] = (x0_ref[part].astype(F32) * (y[rows] + fb_ref[...] * gf[rows])).astype(o_ref.dtype)


def _fft_conv(g, x0, kf, fb, consts):
    b, nt, w, _ = g.shape
    n1 = 2 * nt
    cb = FFT_CB
    g2 = g.reshape(b, nt, w * TM)
    x2 = x0.reshape(b, nt, w * TM)
    _, tw, fwd2, inv2, fwd1, inv1 = consts
    assert b % 2 == 0
    dspec = pl.BlockSpec((2, nt, cb * TM), lambda c, bi: (bi, 0, c))
    full = lambda a: pl.BlockSpec(a.shape, lambda c, bi: (0,) * a.ndim)
    out = pl.pallas_call(
        _fftconv_kernel,
        grid=(w // cb, b // 2),
        in_specs=[dspec, dspec,
                  pl.BlockSpec((cb * n1, 2 * TM), lambda c, bi: (c, 0)),
                  pl.BlockSpec((1, cb * TM), lambda c, bi: (0, c)),
                  full(fwd1), full(tw), full(fwd2), full(inv2), full(inv1)],
        out_specs=dspec,
        out_shape=jax.ShapeDtypeStruct((b, nt, w * TM), BF16),
        compiler_params=_params("arbitrary", "arbitrary"),
        name="long_conv",
    )(g2, x2, kf, fb, fwd1, tw, fwd2, inv2, inv1)
    return out.reshape(b, nt, w, TM)


def _spectrum_kernel(hf_ref, hb_ref, fwd1_ref, tw_ref, fwd2_ref, k_ref):
    cb = hf_ref.shape[-1] // TM
    tw = tw_ref[...]
    xf = _fft_fwd(hf_ref[...].astype(BF16), fwd1_ref[...], tw, fwd2_ref[...], cb)
    xb = _fft_fwd(hb_ref[...].astype(BF16), fwd1_ref[...], tw, fwd2_ref[...], cb)
    k_ref[...] = jnp.concatenate([xf[:, :TM] + xb[:, :TM], xf[:, TM:] - xb[:, TM:]], axis=1)


def _filter_spectrum(hf, hb, consts):
    nt, w, _ = hf.shape
    n1 = 2 * nt
    cb = FFT_CB
    fwd1, tw, fwd2 = consts[:3]
    hspec = pl.BlockSpec((nt, cb * TM), lambda c: (0, c))
    full = lambda a: pl.BlockSpec(a.shape, lambda c: (0,) * a.ndim)
    return pl.pallas_call(
        _spectrum_kernel,
        grid=(w // cb,),
        in_specs=[hspec, hspec, full(fwd1), full(tw), full(fwd2)],
        out_specs=pl.BlockSpec((cb * n1, 2 * TM), lambda c: (c, 0)),
        out_shape=jax.ShapeDtypeStruct((w * n1, 2 * TM), F32),
        compiler_params=_params("arbitrary"),
        name="filter_spectrum",
    )(hf.reshape(nt, w * TM), hb.reshape(nt, w * TM), fwd1, tw, fwd2)


def _filter_kernel(z_ref, t_ref, w1_ref, b1_ref, w2_ref, b2_ref, fr_ref, w3f_ref, w3b_ref, dl_ref,
                   hf_ref, hb_ref):
    nt = hf_ref.shape[0]
    dot = functools.partial(jnp.dot, preferred_element_type=F32, precision=HIGHEST)
    fr = fr_ref[...]
    h = jnp.sin(fr * (dot(w1_ref[...], z_ref[...]) + b1_ref[...]))
    h = jnp.sin(fr * (dot(w2_ref[...], h) + b2_ref[...]))
    decay = jnp.exp(-t_ref[...] * jnp.abs(dl_ref[...]))
    hf = dot(w3f_ref[...], h) * decay
    hb = dot(w3b_ref[...], h) * decay
    norm = (jnp.sum(jnp.abs(hf), axis=1, keepdims=True)
            + jnp.sum(jnp.abs(hb), axis=1, keepdims=True))
    hf = hf / norm
    hb = hb / norm
    lane = lax.broadcasted_iota(I32, hb.shape, 1)
    hb = jnp.where(lane == 0, 0.0, hb)
    for i in range(nt):
        hf_ref[i] = hf[:, i * TM:(i + 1) * TM]
        hb_ref[i] = hb[:, i * TM:(i + 1) * TM]


def _implicit_filters(l, fw1, fb1, fw2, fb2, fw3, ffreq):
    nt = l // TM
    w = HYENA_WIDTH
    t = jnp.linspace(0.0, 1.0, l, dtype=F32)[None, :]
    bands = (FILTER_EMB_DIM - 1) // 2
    wv = 2.0 * math.pi * jnp.arange(l, dtype=F32)[None, :] / l
    f = jnp.linspace(1e-4, bands - 1, bands, dtype=F32)[:, None]
    z = jnp.concatenate([t, jnp.cos(f * wv), -jnp.sin(f * wv)], axis=0)
    min_decay = math.log(DECAY_TARGET) / SLOW_DECAY_PCT
    max_decay = math.log(DECAY_TARGET) / FAST_DECAY_PCT
    deltas = jnp.linspace(min_decay, max_decay, w, dtype=F32)[:, None]
    w3t = fw3.T
    nc = w // FILT_CB
    col = lambda a: a.reshape(-1, 1)
    full = lambda a: pl.BlockSpec(a.shape, lambda c: (0,) * a.ndim)
    args = (z, t, fw1.T, col(fb1), fw2.T, col(fb2), col(ffreq))
    ospec = pl.BlockSpec((nt, FILT_CB, TM), lambda c: (0, c, 0))
    return pl.pallas_call(
        _filter_kernel,
        grid=(nc,),
        in_specs=[full(a) for a in args] + [
            pl.BlockSpec((FILT_CB, FILTER_HIDDEN), lambda c: (c, 0)),
            pl.BlockSpec((FILT_CB, FILTER_HIDDEN), lambda c: (nc + c, 0)),
            pl.BlockSpec((FILT_CB, 1), lambda c: (c, 0))],
        out_specs=[ospec, ospec],
        out_shape=[jax.ShapeDtypeStruct((nt, w, TM), F32)] * 2,
        compiler_params=_params("arbitrary"),
        name="implicit_filter",
    )(*args, w3t, w3t, deltas)


def _dense_dft_consts():
    n = 2 * TM
    two_pi = 2.0 * math.pi
    t = jnp.arange(TM, dtype=I32)
    f = jnp.arange(n, dtype=I32)
    ang = ((t[:, None] * f[None, :]) % n).astype(F32) * (two_pi / n)
    fwd = jnp.concatenate([jnp.cos(ang), -jnp.sin(ang)], axis=1).astype(BF16)
    inv = (jnp.stack([jnp.cos(ang).T, -jnp.sin(ang).T]) * (1.0 / n)).astype(BF16)
    return fwd, inv


def _hyena_tile_kernel(u_ref, w_ref, b_ref, hf_ref, hb_ref, fb_ref, fwd_ref, inv_ref, o_ref):
    w = HYENA_WIDTH
    n = 2 * TM
    x0 = _short_conv_tiles(u_ref.at[:, :, 0:w], w_ref[0], b_ref[0])[0]
    x1 = _short_conv_tiles(u_ref.at[:, :, w:2 * w], w_ref[1], b_ref[1])[0]
    v = _short_conv_tiles(u_ref.at[:, :, 2 * w:3 * w], w_ref[2], b_ref[2])[0]
    g = (v * x1).astype(BF16)
    fwd = fwd_ref[...]
    dot = functools.partial(jnp.dot, preferred_element_type=F32)
    kf = dot(hf_ref[0].astype(BF16), fwd)
    kb = dot(hb_ref[0].astype(BF16), fwd)
    kr = kf[:, :n] + kb[:, :n]
    ki = kf[:, n:] - kb[:, n:]
    x = dot(g, fwd)
    xr, xi = x[:, :n], x[:, n:]
    yr = (xr * kr - xi * ki).astype(BF16)
    yi = (xr * ki + xi * kr).astype(BF16)
    y = dot(yr, inv_ref[0]) + dot(yi, inv_ref[1])
    o_ref[0, 0] = (x0 * (y + fb_ref[...] * g.astype(F32))).astype(o_ref.dtype)


def _hyena_one_tile(u, cw, cb, hf, hb, fbias, consts):
    b = u.shape[0]
    w = HYENA_WIDTH
    fwd, inv = consts
    full = lambda a: pl.BlockSpec(a.shape, lambda bi: (0,) * a.ndim)
    return pl.pallas_call(
        _hyena_tile_kernel,
        grid=(b,),
        in_specs=[pl.BlockSpec((1, 1, 3 * w, TM), lambda bi: (bi, 0, 0, 0)),
                  full(cw), full(cb), full(hf), full(hb), full(fbias), full(fwd), full(inv)],
        out_specs=pl.BlockSpec((1, 1, w, TM), lambda bi: (bi, 0, 0, 0)),
        out_shape=jax.ShapeDtypeStruct((b, 1, w, TM), BF16),
        compiler_params=_params("arbitrary"),
        name="hyena_context",
    )(u, cw, cb, hf, hb, fbias, fwd, inv)


def _top2_sum(a, b, c, d):
    hi1, lo1 = jnp.maximum(a, b), jnp.minimum(a, b)
    hi2, lo2 = jnp.maximum(c, d), jnp.minimum(c, d)
    return jnp.maximum(hi1, hi2) + jnp.maximum(jnp.minimum(hi1, hi2), jnp.maximum(lo1, lo2))


def _outproj_kernel(a_ref, y_ref, x_ref, mod_ref, ga_ref, gy_ref, wo_ref, n2_ref, rw_ref, rb_ref,
                    cnt_ref, tri_ref, xo_ref, row_ref, route_ref, cnto_ref, cnt_sc):
    first = jnp.logical_and(pl.program_id(0) == 0, pl.program_id(1) == 0)

    @pl.when(first)
    def _():
        cnt_sc[...] = cnt_ref[...]

    def group_norm(t, gain):
        t = t.astype(F32)
        return t * lax.rsqrt(jnp.mean(t * t, axis=0, keepdims=True) + EPS) * gain

    mix = jnp.concatenate([group_norm(a_ref[0, 0], ga_ref[...]),
                           group_norm(y_ref[0, 0], gy_ref[...])], axis=0).astype(BF16)
    o = lax.dot_general(mix, wo_ref[...], (((0,), (0,)), ((), ())), preferred_element_type=F32)
    xn = x_ref[0] + mod_ref[0, 2:3, :] * o
    xo_ref[0] = xn
    h2 = _norm_mod(xn, n2_ref[...], mod_ref[0, 3:4, :], mod_ref[0, 4:5, :])

    tm = xn.shape[0]
    nt_dims = (((1,), (1,)), ((), ()))
    h_hi = h2.astype(BF16)
    h_lo = (h2 - h_hi.astype(F32)).astype(BF16)
    rw = rw_ref[...]
    part = lax.dot_general(rw, h_hi, nt_dims, preferred_element_type=F32)
    logits = (part[:N_EXPERTS] + part[N_EXPERTS:]
              + lax.dot_general(rw[:N_EXPERTS], h_lo, nt_dims, preferred_element_type=F32))
    score = jax.nn.sigmoid(logits)
    sel = score + rb_ref[...]
    srow = [sel[e:e + 1] for e in range(N_EXPERTS)]
    prow = [score[e:e + 1] for e in range(N_EXPERTS)]
    epg = EXPERTS_PER_GROUP
    gs = [_top2_sum(*srow[g * epg:(g + 1) * epg]) for g in range(N_EXPERT_GROUPS)]
    bg = jnp.zeros((1, tm), I32)
    best = gs[0]
    for g in range(1, N_EXPERT_GROUPS):
        upd = gs[g] > best
        bg = jnp.where(upd, g, bg)
        best = jnp.where(upd, gs[g], best)

    def pick_group(rows, j):
        out = rows[j]
        for g in range(1, N_EXPERT_GROUPS):
            out = jnp.where(bg == g, rows[g * epg + j], out)
        return out

    cand = [pick_group(srow, j) for j in range(epg)]
    cprob = [pick_group(prow, j) for j in range(epg)]
    i1 = jnp.zeros((1, tm), I32)
    v1, w1 = cand[0], cprob[0]
    for j in range(1, epg):
        upd = cand[j] > v1
        i1 = jnp.where(upd, j, i1)
        v1 = jnp.where(upd, cand[j], v1)
        w1 = jnp.where(upd, cprob[j], w1)
    i2 = jnp.zeros((1, tm), I32)
    v2 = jnp.full((1, tm), -jnp.inf, F32)
    w2 = jnp.zeros((1, tm), F32)
    for j in range(epg):
        upd = jnp.logical_and(i1 != j, cand[j] > v2)
        i2 = jnp.where(upd, j, i2)
        v2 = jnp.where(upd, cand[j], v2)
        w2 = jnp.where(upd, cprob[j], w2)
    den = w1 + w2
    w1, w2 = w1 / den, w2 / den
    first_lower = i1 < i2
    lo = jnp.where(first_lower, i1, i2)
    hi = jnp.where(first_lower, i2, i1)
    w_lo = jnp.where(first_lower, w1, w2)
    w_hi = jnp.where(first_lower, w2, w1)
    pair = jnp.where(lo == 0, hi - 1, jnp.where(lo == 1, hi + 1, PAIRS_PER_GROUP - 1))
    cls = bg * PAIRS_PER_GROUP + pair

    cio = lax.broadcasted_iota(I32, (N_CLASS_ROWS, tm), 0)
    onehot = jnp.where(cio == cls, 1.0, 0.0)
    cum = jnp.dot(onehot.astype(BF16), tri_ref[...], preferred_element_type=F32)
    base = cnt_sc[...]
    rank = jnp.sum(onehot * (base + cum), axis=0, keepdims=True)
    route_ref[0, 0] = jnp.concatenate([cls, rank.astype(I32)], axis=0)
    new = base + jnp.sum(onehot, axis=1, keepdims=True)
    cnt_sc[...] = new
    cnto_ref[...] = new

    half = h2.shape[1] // 2
    wrows = jnp.concatenate([w_lo, w_hi, jnp.zeros((ROW_EXTRA - 2, tm), F32)], axis=0)
    row_ref[0] = jnp.concatenate([_pack_bf16_pair(h2[:, :half], h2[:, half:]),
                                  lax.bitcast_convert_type(wrows.T, U32)], axis=1)


def _outproj_route(a, y, x, mods, mod_row, ga, gy, wo, n2g, rwt, rb, cnt, tri):
    b, l, d = x.shape
    nt = l // TM
    if mod_row is None:
        mod_map = lambda bi, i: (bi, 0, 0)
    else:
        mod_map = lambda bi, i: (mod_row, 0, 0)
    full = lambda arr: pl.BlockSpec(arr.shape, lambda bi, i: (0,) * arr.ndim)
    tile = pl.BlockSpec((1, 1, ATTN_WIDTH, TM), lambda bi, i: (bi, i, 0, 0))
    xspec = pl.BlockSpec((1, TM, d), lambda bi, i: (bi, i, 0))
    rspec = pl.BlockSpec((1, 1, 2, TM), lambda bi, i: (bi, i, 0, 0))
    row_w = d // 2 + ROW_EXTRA
    return pl.pallas_call(
        _outproj_kernel,
        grid=(b, nt),
        in_specs=[tile, tile, xspec, pl.BlockSpec((1, N_MOD, d), mod_map),
                  full(ga), full(gy), full(wo), full(n2g), full(rwt), full(rb), full(cnt), full(tri)],
        out_specs=[xspec, pl.BlockSpec((1, TM, row_w), lambda bi, i: (bi, i, 0)), rspec, full(cnt)],
        out_shape=[jax.ShapeDtypeStruct((b, l, d), F32),
                   jax.ShapeDtypeStruct((b, l, row_w), U32),
                   jax.ShapeDtypeStruct((b, nt, 2, TM), I32),
                   jax.ShapeDtypeStruct(cnt.shape, F32)],
        scratch_shapes=[pltpu.VMEM(cnt.shape, F32)],
        compiler_params=_params("arbitrary", "arbitrary"),
        name="outproj_route",
    )(a, y, x, mods, ga, gy, wo, n2g, rwt, rb, cnt, tri)


def _dispatch_kernel(slot_ref, h_ref, xs_in_ref, xs_ref, stage, sem, *, nstep):
    del xs_in_ref
    tm = h_ref.shape[0]
    step = pl.program_id(0)
    cur = lax.rem(step, 2)

    def copy(side, r, s):
        return pltpu.make_async_copy(stage.at[side, pl.ds(r, 1)], xs_ref.at[pl.ds(s, 1)], sem.at[side])

    def drain(side):
        def wait_one(r, carry):
            copy(side, 0, 0).wait()
            return carry

        lax.fori_loop(0, tm, wait_one, 0, unroll=8)

    @pl.when(step >= 2)
    def _():
        drain(cur)

    stage[cur] = h_ref[...]

    def issue(r, carry):
        copy(cur, r, slot_ref[0, 0, r]).start()
        return carry

    lax.fori_loop(0, tm, issue, 0, unroll=8)

    @pl.when(step == nstep - 1)
    def _():
        @pl.when(step >= 1)
        def _():
            drain(1 - cur)

        drain(cur)


def _dispatch(slots, h2, xs):
    t, d = h2.shape
    return pl.pallas_call(
        functools.partial(_dispatch_kernel, nstep=t // TM),
        grid=(t // TM,),
        in_specs=[pl.BlockSpec((1, 1, TM), lambda i: (i, 0, 0), memory_space=pltpu.SMEM),
                  pl.BlockSpec((TM, d), lambda i: (i, 0)),
                  pl.BlockSpec(memory_space=pl.ANY)],
        out_specs=pl.BlockSpec(memory_space=pl.ANY),
        out_shape=jax.ShapeDtypeStruct(xs.shape, xs.dtype),
        scratch_shapes=[pltpu.VMEM((2, TM, d), xs.dtype), pltpu.SemaphoreType.DMA((2,))],
        input_output_aliases={2: 0},
        compiler_params=_params("arbitrary"),
        name="moe_dispatch",
    )(slots, h2, xs)


def _ffn_kernel(ea_ref, eb_ref, nt_ref, xs_ref, wga_ref, wua_ref, wda_ref, wgb_ref, wub_ref, wdb_ref, ys_ref):
    del ea_ref, eb_ref
    live = pl.program_id(0) < nt_ref[0]
    half = ys_ref.shape[1]

    @pl.when(live)
    def _():
        words = xs_ref[...]
        lo, hi = _unpack_bf16_pair(words[:, :half])
        lo, hi = lo.astype(BF16), hi.astype(BF16)
        wts = lax.bitcast_convert_type(words[:, half:half + 2], F32)

        def expert(wg_ref, wu_ref, wd_ref):
            dot = functools.partial(jnp.dot, preferred_element_type=F32)
            g = dot(lo, wg_ref[0, :half, :]) + dot(hi, wg_ref[0, half:, :])
            u = dot(lo, wu_ref[0, :half, :]) + dot(hi, wu_ref[0, half:, :])
            a = (g * jax.nn.sigmoid(g)) * u
            return dot(a.astype(BF16), wd_ref[0])

        y = (wts[:, 0:1] * expert(wga_ref, wua_ref, wda_ref)
             + wts[:, 1:2] * expert(wgb_ref, wub_ref, wdb_ref))
        ys_ref[...] = _pack_bf16_pair(y[:, :half], y[:, half:])

    @pl.when(jnp.logical_not(live))
    def _():
        ys_ref[...] = jnp.zeros_like(ys_ref)


def _expert_ffn(tile_ea, tile_eb, n_tiles, xs, wg, wu, wd):
    nslot, row_w = xs.shape
    d, f = wg.shape[1:]
    ntile = nslot // FFN_TILE
    row = lambda i, ea, eb, nt: (jnp.minimum(i, nt[0] - 1), 0)
    amap = lambda i, ea, eb, nt: (ea[jnp.minimum(i, nt[0] - 1)], 0, 0)
    bmap = lambda i, ea, eb, nt: (eb[jnp.minimum(i, nt[0] - 1)], 0, 0)
    return pl.pallas_call(
        _ffn_kernel,
        grid_spec=pltpu.PrefetchScalarGridSpec(
            num_scalar_prefetch=3,
            grid=(ntile,),
            in_specs=[pl.BlockSpec((FFN_TILE, row_w), row),
                      pl.BlockSpec((1, d, f), amap), pl.BlockSpec((1, d, f), amap), pl.BlockSpec((1, f, d), amap),
                      pl.BlockSpec((1, d, f), bmap), pl.BlockSpec((1, d, f), bmap), pl.BlockSpec((1, f, d), bmap)],
            out_specs=pl.BlockSpec((FFN_TILE, d // 2), lambda i, ea, eb, nt: (i, 0))),
        out_shape=jax.ShapeDtypeStruct((nslot, d // 2), U32),
        compiler_params=_params("arbitrary"),
        name="moe_experts",
    )(tile_ea, tile_eb, n_tiles, xs, wg, wu, wd, wg, wu, wd)


def _combine_kernel(slot_ref, next_slot_ref, x_ref, mod_ref, ys_ref, o_ref, buf, sem, *, nt, nstep):
    tm = x_ref.shape[1]
    step = pl.program_id(0) * nt + pl.program_id(1)
    cur = lax.rem(step, 2)

    def copy(side, r, s):
        return pltpu.make_async_copy(ys_ref.at[pl.ds(s, 1)], buf.at[side, pl.ds(r, 1)], sem.at[side])

    def gather(side, slots):
        def issue(r, carry):
            copy(side, r, slots[0, 0, r]).start()
            return carry

        lax.fori_loop(0, tm, issue, 0, unroll=8)

    @pl.when(step == 0)
    def _():
        gather(cur, slot_ref)

    @pl.when(step + 1 < nstep)
    def _():
        gather(1 - cur, next_slot_ref)

    def wait_one(r, carry):
        copy(cur, 0, 0).wait()
        return carry

    lax.fori_loop(0, tm, wait_one, 0, unroll=8)
    lo, hi = _unpack_bf16_pair(buf[cur])
    y = jnp.concatenate([lo, hi], axis=1)
    o_ref[0] = x_ref[0] + mod_ref[0, 5:6, :] * y


def _combine(slots, x, mods, mod_row, ys):
    b, l, d = x.shape
    nt = l // TM
    if mod_row is None:
        mod_map = lambda bi, i: (bi, 0, 0)
    else:
        mod_map = lambda bi, i: (mod_row, 0, 0)
    xspec = pl.BlockSpec((1, TM, d), lambda bi, i: (bi, i, 0))
    last = b * nt - 1
    return pl.pallas_call(
        functools.partial(_combine_kernel, nt=nt, nstep=b * nt),
        grid=(b, nt),
        in_specs=[pl.BlockSpec((1, 1, TM), lambda bi, i: (bi * nt + i, 0, 0), memory_space=pltpu.SMEM),
                  pl.BlockSpec((1, 1, TM), lambda bi, i: (jnp.minimum(bi * nt + i + 1, last), 0, 0),
                               memory_space=pltpu.SMEM),
                  xspec,
                  pl.BlockSpec((1, N_MOD, d), mod_map),
                  pl.BlockSpec(memory_space=pl.ANY)],
        out_specs=xspec,
        out_shape=jax.ShapeDtypeStruct((b, l, d), F32),
        scratch_shapes=[pltpu.VMEM((2, TM, d // 2), U32), pltpu.SemaphoreType.DMA((2,))],
        compiler_params=_params("arbitrary", "arbitrary"),
        name="moe_combine",
    )(slots, slots, x, mods, ys)


def _moe(streams, wg, wu, wd):
    counts = streams[-1]["counts"].reshape(-1).astype(I32)
    padded = ((counts + FFN_TILE - 1) // FFN_TILE) * FFN_TILE
    ends = jnp.cumsum(padded)
    offs = ends - padded
    total = sum(s["x"].shape[0] * s["x"].shape[1] for s in streams)
    ntile = total // FFN_TILE + N_CLASSES
    nslot = ntile * FFN_TILE
    tile_start = jnp.arange(ntile, dtype=I32) * FFN_TILE
    tile_class = jnp.minimum(jnp.sum((tile_start[:, None] >= ends[None, :N_CLASSES]).astype(I32), axis=1),
                             N_CLASSES - 1)
    group_base = (tile_class // PAIRS_PER_GROUP) * EXPERTS_PER_GROUP
    tile_ea = (group_base + jnp.asarray(PAIR_LO, I32)[tile_class % PAIRS_PER_GROUP]).astype(I32)
    tile_eb = (group_base + jnp.asarray(PAIR_HI, I32)[tile_class % PAIRS_PER_GROUP]).astype(I32)
    n_tiles = (ends[-1] // FFN_TILE).astype(I32).reshape(1)

    row_w = streams[0]["rows"].shape[-1]
    xs = jnp.zeros((nslot, row_w), U32)
    for s in streams:
        b, l, _ = s["x"].shape
        cls, rank = s["route"][:, :, 0], s["route"][:, :, 1]
        slot = rank
        for c in range(N_CLASSES):
            slot = slot + jnp.where(cls == c, offs[c], 0)
        s["slots"] = slot.reshape(b * (l // TM), 1, TM)
        xs = _dispatch(s["slots"], s["rows"].reshape(b * l, row_w), xs)
    ys = _expert_ffn(tile_ea, tile_eb, n_tiles, xs, wg, wu, wd)
    return [_combine(s["slots"], s["x"], s["mods"], s["mod_row"], ys) for s in streams]


def _rope_table(l):
    t = jnp.arange(l)
    row = (t // GRID_W).astype(F32)
    col = (t % GRID_W).astype(F32)
    inv_freq = ROPE_THETA ** (-jnp.arange(ROPE_QUARTER, dtype=F32) / ROPE_QUARTER)
    ar = inv_freq[:, None] * row[None, :]
    ac = inv_freq[:, None] * col[None, :]
    return jnp.concatenate([jnp.cos(ar), jnp.sin(ar), jnp.cos(ac), jnp.sin(ac)], axis=0)


def _identity_rope_table(l):
    one = jnp.ones((ROPE_QUARTER, l), F32)
    zero = jnp.zeros((ROPE_QUARTER, l), F32)
    return jnp.concatenate([one, zero, one, zero], axis=0)


def kernel(x, c, ctx, c_ctx, w_mod, b_mod, norm1_g, w_in, q_norm_g, k_norm_g, conv_w, conv_b, filt_w1, filt_b1, filt_w2, filt_b2, filt_w3, filt_freq, filt_bias, attn_out_g, hyena_out_g, w_out, norm2_g, router_w, router_bias, expert_w_gate, expert_w_up, expert_w_down):
    depth = w_mod.shape[0]
    b, l, d = x.shape
    lc = ctx.shape[1]
    assert l % TM == 0 and lc == TM and d == D_MODEL
    nt = l // TM
    w = HYENA_WIDTH

    mods = _modulations(c, c_ctx, w_mod, b_mod)
    rope_lat = _rope_table(l)
    rope_ctx = _identity_rope_table(lc)
    dft = _dft_consts(nt)
    dft_ctx = _dense_dft_consts()
    tri = (jnp.arange(TM)[:, None] < jnp.arange(TM)[None, :]).astype(BF16)
    rwt = router_w.T
    rw_hi = rwt.astype(BF16)
    rwt = jnp.concatenate([rw_hi, (rwt - rw_hi.astype(F32)).astype(BF16)], axis=0)
    rb = router_bias.reshape(-1, 1)
    zero_cnt = jnp.zeros((N_CLASS_ROWS, 1), F32)
    col = lambda a: a.reshape(-1, 1)

    for li in range(depth):
        last = li == depth - 1
        m = mods[li]
        wt = w_in[li].T.astype(BF16)
        wo = w_out[li].astype(BF16)
        g1n = norm1_g[li].reshape(1, d)
        g2n = norm2_g[li].reshape(1, d)
        qg, kg = col(q_norm_g[li]), col(k_norm_g[li])
        cw = conv_w[li].T.reshape(3, w, SHORT_CONV)
        cb = conv_b[li].reshape(3, w, 1)
        ga, gy = col(attn_out_g[li]), col(hyena_out_g[li])
        fargs = (filt_w1[li], filt_b1[li], filt_w2[li], filt_b2[li], filt_w3[li], filt_freq[li])

        q, k, v, u = _inproj(x, m, None, g1n, wt, qg, kg, rope_lat)
        qc, kc, vc, uc = _inproj(ctx, m, b, g1n, wt, qg, kg, rope_ctx)
        a = _attention(q, jnp.concatenate([kc, k], axis=2), jnp.concatenate([vc, v], axis=2))
        hf, hb = _implicit_filters(l, *fargs)
        kf = _filter_spectrum(hf, hb, dft)
        g, x0 = _short_conv_gate(u, cw, cb)
        fb_lanes = jnp.repeat(filt_bias[li], TM).reshape(1, w * TM)
        y = _fft_conv(g, x0, kf, fb_lanes, dft)
        lat = dict(zip(("x", "rows", "route", "counts"),
                       _outproj_route(a, y, x, m, None, ga, gy, wo, g2n, rwt, rb, zero_cnt, tri)))
        lat.update(mods=m, mod_row=None)
        streams = [lat]

        if not last:
            ac = _attention(qc, kc, vc)
            hfc, hbc = _implicit_filters(lc, *fargs)
            yc = _hyena_one_tile(uc, cw, cb, hfc, hbc, col(filt_bias[li]), dft_ctx)
            cs = dict(zip(("x", "rows", "route", "counts"),
                          _outproj_route(ac, yc, ctx, m, b, ga, gy, wo, g2n, rwt, rb, lat["counts"], tri)))
            cs.update(mods=m, mod_row=b)
            streams.append(cs)

        outs = _moe(streams, expert_w_gate[li].astype(BF16), expert_w_up[li].astype(BF16),
                    expert_w_down[li].astype(BF16))
        x = outs[0]
        if not last:
            ctx = outs[1]
    return x
```

```python
import functools
import math

import jax
import jax.numpy as jnp
from jax import lax
from jax.experimental import pallas as pl
from jax.experimental.pallas import tpu as pltpu

F32 = jnp.float32
BF16 = jnp.bfloat16
I32 = jnp.int32
HIGHEST = lax.Precision.HIGHEST

D_MODEL = 1024
N_MOD = 6
EPS = 1e-6
N_Q_HEADS = 8
N_KV_HEADS = 2
HEAD_DIM = 64
KV_REP = N_Q_HEADS // N_KV_HEADS
ATTN_WIDTH = N_Q_HEADS * HEAD_DIM
KV_WIDTH = N_KV_HEADS * HEAD_DIM
ATTN_SCALE = HEAD_DIM ** -0.5
LOG2_E = math.log2(math.e)
V_ROWS = HEAD_DIM + 16
K_COLS = HEAD_DIM + 16
SCORE_BOUND_SLACK = 1.0 + 2.0 ** -6
UNDERFLOW_GUARD = 2.0 ** -80
GRID_W = 64
ROPE_THETA = 10000.0
ROPE_QUARTER = HEAD_DIM // 4
HYENA_WIDTH = D_MODEL - ATTN_WIDTH
SHORT_CONV = 3
FILTER_EMB_DIM = 33
FILTER_HIDDEN = 64
DECAY_TARGET = 1e-2
FAST_DECAY_PCT = 0.3
SLOW_DECAY_PCT = 1.5
N_EXPERTS = 16
N_EXPERT_GROUPS = 4
EXPERTS_PER_GROUP = N_EXPERTS // N_EXPERT_GROUPS
D_FF_EXPERT = 512

TM = 256
ATTN_UNROLL = 8
ATTN_TQ = 1024
FFN_TILE = 256
OUTPROJ_TILES = 2
CONV_CB = 128
FFT_CB = 32
FILT_CB = 128
VMEM_LIMIT = 56 * 1024 * 1024


U32 = jnp.uint32
PAIRS_PER_GROUP = EXPERTS_PER_GROUP * (EXPERTS_PER_GROUP - 1) // 2
N_CLASSES = N_EXPERT_GROUPS * PAIRS_PER_GROUP
N_CLASS_ROWS = -(-N_CLASSES // 8) * 8
PAIR_LO = tuple(a for a in range(EXPERTS_PER_GROUP) for b in range(a + 1, EXPERTS_PER_GROUP))
PAIR_HI = tuple(b for a in range(EXPERTS_PER_GROUP) for b in range(a + 1, EXPERTS_PER_GROUP))
ROW_EXTRA = 128


def _pack_bf16_pair(lo, hi):
    lo_bits = lax.bitcast_convert_type(lo.astype(BF16).astype(F32), U32)
    hi_bits = lax.bitcast_convert_type(hi.astype(BF16).astype(F32), U32)
    return lax.shift_right_logical(lo_bits, U32(16)) | (hi_bits & U32(0xFFFF0000))


def _unpack_bf16_pair(words):
    lo = lax.bitcast_convert_type(lax.shift_left(words, U32(16)), F32)
    hi = lax.bitcast_convert_type(words & U32(0xFFFF0000), F32)
    return lo, hi


def _params(*sem):
    return pltpu.CompilerParams(dimension_semantics=tuple(sem), vmem_limit_bytes=VMEM_LIMIT)


def _norm_mod(x, g, shift, scale):
    y = x * lax.rsqrt(jnp.mean(x * x, axis=-1, keepdims=True) + EPS)
    return (y * g) * (1 + scale) + shift


def _mod_kernel(c_ref, w_ref, b_ref, o_ref):
    c = c_ref[...]
    s = c * jax.nn.sigmoid(c)
    o_ref[0] = jnp.dot(s, w_ref[0], preferred_element_type=F32, precision=HIGHEST) + b_ref[0]


def _modulations(c, c_ctx, w_mod, b_mod):
    depth, d, nmd = w_mod.shape
    b = c.shape[0]
    rows = -(-(b + 1) // 8) * 8
    c_all = jnp.zeros((rows, d), F32).at[:b].set(c).at[b].set(c_ctx)
    tn = nmd // 4
    out = pl.pallas_call(
        _mod_kernel,
        grid=(depth, nmd // tn),
        in_specs=[pl.BlockSpec((rows, d), lambda l, j: (0, 0)),
                  pl.BlockSpec((1, d, tn), lambda l, j: (l, 0, j)),
                  pl.BlockSpec((1, 1, tn), lambda l, j: (l, 0, j))],
        out_specs=pl.BlockSpec((1, rows, tn), lambda l, j: (l, 0, j)),
        out_shape=jax.ShapeDtypeStruct((depth, rows, nmd), F32),
        compiler_params=_params("arbitrary", "arbitrary"),
        name="modulation",
    )(c_all, w_mod, b_mod.reshape(depth, 1, nmd))
    return out.reshape(depth, rows, N_MOD, d)


def _inproj_kernel(x_ref, mod_ref, g_ref, wt_ref, qg_ref, kg_ref, rope_ref,
                   q_ref, k_ref, v_ref, u_ref):
    x = x_ref[0]
    tm = x.shape[0]
    h = _norm_mod(x, g_ref[...], mod_ref[0, 0:1, :], mod_ref[0, 1:2, :])
    pt = lax.dot_general(wt_ref[...], h.astype(BF16), (((1,), (1,)), ((), ())),
                         preferred_element_type=F32)
    rope = rope_ref[...]
    qd = ROPE_QUARTER
    cr, sr = rope[0:qd][None], rope[qd:2 * qd][None]
    cc, sc = rope[2 * qd:3 * qd][None], rope[3 * qd:4 * qd][None]

    def norm_rope(t, gain, nh):
        t = t.reshape(nh, HEAD_DIM, tm)
        t = t * lax.rsqrt(jnp.mean(t * t, axis=1, keepdims=True) + EPS) * gain[None]
        a, b = t[:, 0:qd], t[:, qd:2 * qd]
        c, d = t[:, 2 * qd:3 * qd], t[:, 3 * qd:4 * qd]
        return jnp.concatenate([a * cr - b * sr, b * cr + a * sr,
                                c * cc - d * sc, d * cc + c * sc], axis=1)

    q = norm_rope(pt[0:ATTN_WIDTH], qg_ref[...], N_Q_HEADS) * (ATTN_SCALE * LOG2_E)
    q_ref[0] = q.astype(BF16)
    k = norm_rope(pt[ATTN_WIDTH:ATTN_WIDTH + KV_WIDTH], kg_ref[...], N_KV_HEADS)
    kt = k.reshape(KV_WIDTH, tm).T
    for g in range(N_KV_HEADS):
        k_ref[0, g, 0, :, 0:HEAD_DIM] = kt[:, g * HEAD_DIM:(g + 1) * HEAD_DIM].astype(BF16)
    pad_col = lax.broadcasted_iota(I32, (N_KV_HEADS, tm, K_COLS - HEAD_DIM), 2)
    k_ref[0, :, 0, :, HEAD_DIM:K_COLS] = jnp.where(pad_col == 0, 1.0, 0.0).astype(BF16)
    v = pt[ATTN_WIDTH + KV_WIDTH:ATTN_WIDTH + 2 * KV_WIDTH]
    v_ref[0, :, 0, 0:HEAD_DIM] = v.reshape(N_KV_HEADS, HEAD_DIM, tm).astype(BF16)
    pad_row = lax.broadcasted_iota(I32, (N_KV_HEADS, V_ROWS - HEAD_DIM, tm), 1)
    v_ref[0, :, 0, HEAD_DIM:V_ROWS] = jnp.where(pad_row == 0, 1.0, 0.0).astype(BF16)
    u_ref[0, 0] = pt[ATTN_WIDTH + 2 * KV_WIDTH:].astype(BF16)


def _inproj(x, mods, mod_row, g, wt, qg, kg, rope):
    b, l, d = x.shape
    nt = l // TM
    p = wt.shape[0]
    uw = p - ATTN_WIDTH - 2 * KV_WIDTH
    if mod_row is None:
        mod_map = lambda bi, i: (bi, 0, 0)
    else:
        mod_map = lambda bi, i: (mod_row, 0, 0)
    return pl.pallas_call(
        _inproj_kernel,
        grid=(b, nt),
        in_specs=[pl.BlockSpec((1, TM, d), lambda bi, i: (bi, i, 0)),
                  pl.BlockSpec((1, N_MOD, d), mod_map),
                  pl.BlockSpec((1, d), lambda bi, i: (0, 0)),
                  pl.BlockSpec((p, d), lambda bi, i: (0, 0)),
                  pl.BlockSpec((HEAD_DIM, 1), lambda bi, i: (0, 0)),
                  pl.BlockSpec((HEAD_DIM, 1), lambda bi, i: (0, 0)),
                  pl.BlockSpec((HEAD_DIM, TM), lambda bi, i: (0, i))],
        out_specs=[pl.BlockSpec((1, N_Q_HEADS, HEAD_DIM, TM), lambda bi, i: (bi, 0, 0, i)),
                   pl.BlockSpec((1, N_KV_HEADS, 1, TM, K_COLS), lambda bi, i: (bi, 0, i, 0, 0)),
                   pl.BlockSpec((1, N_KV_HEADS, 1, V_ROWS, TM), lambda bi, i: (bi, 0, i, 0, 0)),
                   pl.BlockSpec((1, 1, uw, TM), lambda bi, i: (bi, i, 0, 0))],
        out_shape=[jax.ShapeDtypeStruct((b, N_Q_HEADS, HEAD_DIM, l), BF16),
                   jax.ShapeDtypeStruct((b, N_KV_HEADS, nt, TM, K_COLS), BF16),
                   jax.ShapeDtypeStruct((b, N_KV_HEADS, nt, V_ROWS, TM), BF16),
                   jax.ShapeDtypeStruct((b, nt, uw, TM), BF16)],
        compiler_params=_params("arbitrary", "arbitrary"),
        name="inproj",
    )(x, mods, g, wt, qg, kg, rope)


def _attn_kernel(q_ref, k_ref, v_ref, kmax_ref, o_ref, acc_ref, m_ref):
    nk = k_ref.shape[2]
    tq = q_ref.shape[-1]
    nq = KV_REP * tq
    q = jnp.concatenate([q_ref[0, r] for r in range(KV_REP)], axis=1)
    pad = jnp.zeros((K_COLS - HEAD_DIM - 1, nq), F32)

    def write_out():
        acc = acc_ref[...]
        out = acc[0:HEAD_DIM] / acc[HEAD_DIM:HEAD_DIM + 1]
        for r in range(KV_REP):
            for t in range(tq // TM):
                o_ref[0, t, r * HEAD_DIM:(r + 1) * HEAD_DIM, :] = (
                    out[:, r * tq + t * TM:r * tq + (t + 1) * TM].astype(o_ref.dtype))

    def run(chunk):
        def group(i, carry):
            for t in range(ATTN_UNROLL):
                chunk(ATTN_UNROLL * i + t)
            return carry

        lax.fori_loop(0, nk // ATTN_UNROLL, group, 0)
        for j in range(nk - nk % ATTN_UNROLL, nk):
            chunk(j)

    qf = q.astype(F32)
    qnorm = jnp.sqrt(jnp.sum(qf * qf, axis=0, keepdims=True))
    bound = qnorm * (kmax_ref[0, 0][:, 0:1] * SCORE_BOUND_SLACK)
    q_shift = jnp.concatenate([q, jnp.concatenate([-bound, pad], axis=0).astype(BF16)], axis=0)
    acc_ref[...] = jnp.zeros_like(acc_ref)

    def chunk_shifted(j):
        s = jnp.dot(k_ref[0, 0, j], q_shift, preferred_element_type=F32)
        p = jnp.exp2(s).astype(BF16)
        acc_ref[...] += jnp.dot(v_ref[0, 0, j], p, preferred_element_type=F32)

    run(chunk_shifted)
    denom_ok = jnp.min(acc_ref[HEAD_DIM:HEAD_DIM + 1, :]) >= UNDERFLOW_GUARD

    @pl.when(denom_ok)
    def _():
        write_out()

    @pl.when(jnp.logical_not(denom_ok))
    def _():
        q_plain = jnp.concatenate([q, jnp.zeros((K_COLS - HEAD_DIM, nq), BF16)], axis=0)
        acc_ref[...] = jnp.zeros_like(acc_ref)
        m_ref[...] = jnp.full_like(m_ref, -jnp.inf)

        def chunk_online(j):
            s = jnp.dot(k_ref[0, 0, j], q_plain, preferred_element_type=F32)
            m = m_ref[...]
            m_new = jnp.maximum(m, jnp.max(s, axis=0, keepdims=True))
            alpha = jnp.exp2(m - m_new)
            p = jnp.exp2(s - m_new).astype(BF16)
            acc_ref[...] = alpha * acc_ref[...] + jnp.dot(v_ref[0, 0, j], p, preferred_element_type=F32)
            m_ref[...] = m_new

        run(chunk_online)
        write_out()


def _attention(q, k, v):
    b, _, _, l = q.shape
    nk = k.shape[2]
    nt = l // TM
    tq = min(ATTN_TQ, l)
    kf = k[..., :HEAD_DIM].astype(F32)
    kmax = jnp.sqrt(jnp.max(jnp.sum(kf * kf, axis=-1), axis=(2, 3)))
    kmax = jnp.broadcast_to(kmax[:, :, None, None], (b, N_KV_HEADS, 1, 128))
    return pl.pallas_call(
        _attn_kernel,
        grid=(b, N_KV_HEADS, l // tq),
        in_specs=[pl.BlockSpec((1, KV_REP, HEAD_DIM, tq), lambda bi, g, i: (bi, g, 0, i)),
                  pl.BlockSpec((1, 1, nk, TM, K_COLS), lambda bi, g, i: (bi, g, 0, 0, 0)),
                  pl.BlockSpec((1, 1, nk, V_ROWS, TM), lambda bi, g, i: (bi, g, 0, 0, 0)),
                  pl.BlockSpec((1, 1, 1, 128), lambda bi, g, i: (bi, g, 0, 0))],
        out_specs=pl.BlockSpec((1, tq // TM, KV_REP * HEAD_DIM, TM), lambda bi, g, i: (bi, i, g, 0)),
        out_shape=jax.ShapeDtypeStruct((b, nt, ATTN_WIDTH, TM), BF16),
        scratch_shapes=[pltpu.VMEM((V_ROWS, KV_REP * tq), F32), pltpu.VMEM((1, KV_REP * tq), F32)],
        compiler_params=_params("arbitrary", "arbitrary", "arbitrary"),
        name="attention",
    )(q, k, v, kmax)


def _short_conv_tiles(u_ref, w, bias):
    nt, c, tm = u_ref.shape[1:]
    lane = lax.broadcasted_iota(I32, (c, tm), 1)
    tiles = [u_ref[0, i].astype(F32) for i in range(nt)]
    prev = [pltpu.roll(t, 1, 1) for t in tiles]
    nxt = [pltpu.roll(t, tm - 1, 1) for t in tiles]
    zero = jnp.zeros((c, tm), F32)
    out = []
    for i in range(nt):
        up = jnp.where(lane == 0, prev[i - 1] if i > 0 else zero, prev[i])
        un = jnp.where(lane == tm - 1, nxt[i + 1] if i < nt - 1 else zero, nxt[i])
        out.append(bias + up * w[:, 0:1] + tiles[i] * w[:, 1:2] + un * w[:, 2:3])
    return out


def _sconv_kernel(u0_ref, u1_ref, u2_ref, w_ref, b_ref, g_ref, x0_ref):
    x0 = _short_conv_tiles(u0_ref, w_ref[0], b_ref[0])
    x1 = _short_conv_tiles(u1_ref, w_ref[1], b_ref[1])
    v = _short_conv_tiles(u2_ref, w_ref[2], b_ref[2])
    for i in range(len(x0)):
        g_ref[0, i] = (v[i] * x1[i]).astype(g_ref.dtype)
        x0_ref[0, i] = x0[i].astype(x0_ref.dtype)


def _short_conv_gate(u, cw, cb):
    b, nt, _, _ = u.shape
    w = HYENA_WIDTH
    nc = w // CONV_CB
    uspec = lambda grp: pl.BlockSpec((1, nt, CONV_CB, TM), lambda bi, c: (bi, 0, grp * nc + c, 0))
    ospec = pl.BlockSpec((1, nt, CONV_CB, TM), lambda bi, c: (bi, 0, c, 0))
    return pl.pallas_call(
        _sconv_kernel,
        grid=(b, nc),
        in_specs=[uspec(0), uspec(1), uspec(2),
                  pl.BlockSpec((3, CONV_CB, SHORT_CONV), lambda bi, c: (0, c, 0)),
                  pl.BlockSpec((3, CONV_CB, 1), lambda bi, c: (0, c, 0))],
        out_specs=[ospec, ospec],
        out_shape=[jax.ShapeDtypeStruct((b, nt, w, TM), BF16)] * 2,
        compiler_params=_params("arbitrary", "arbitrary"),
        name="short_conv",
    )(u, u, u, cw, cb)


def _dft_consts(nt):
    n1 = 2 * nt
    n = n1 * TM
    two_pi = 2.0 * math.pi

    def cs(prod, mod):
        ang = (prod % mod).astype(F32) * (two_pi / mod)
        return jnp.cos(ang), jnp.sin(ang)

    f1 = jnp.arange(n1, dtype=I32)
    t1 = jnp.arange(nt, dtype=I32)
    t2 = jnp.arange(TM, dtype=I32)
    c, s = cs(f1[:, None] * t1[None, :], n1)
    fwd1 = jnp.concatenate([c, -s], axis=0).astype(BF16)
    fwd1c = jnp.concatenate([jnp.concatenate([c, s], axis=1),
                             jnp.concatenate([-s, c], axis=1)], axis=0).astype(BF16)
    c, s = cs(f1[:, None] * t2[None, :], n)
    tw = jnp.stack([c, -s])
    c, s = cs(t2[:, None] * t2[None, :], TM)
    fwd2 = jnp.stack([jnp.concatenate([c, -s], axis=1),
                      jnp.concatenate([s, c], axis=1)]).astype(BF16)
    inv2 = jnp.stack([jnp.concatenate([c, s], axis=1),
                      jnp.concatenate([-s, c], axis=1)]).astype(BF16)
    c, s = cs(t1[:, None] * f1[None, :], n1)
    inv1c = (jnp.stack([jnp.concatenate([c, s], axis=0),
                        jnp.concatenate([-s, c], axis=0)]) * (1.0 / n)).astype(BF16)
    return fwd1, tw, fwd2, inv2, fwd1c, inv1c


def _fft_fwd(g, fwd1, tw, fwd2, cb):
    n1 = fwd1.shape[0] // 2
    a = jnp.dot(fwd1, g, preferred_element_type=F32)
    are = jnp.concatenate([a[:n1, c * TM:(c + 1) * TM] for c in range(cb)], axis=0)
    aim = jnp.concatenate([a[n1:, c * TM:(c + 1) * TM] for c in range(cb)], axis=0)
    are = are.reshape(cb, n1, TM)
    aim = aim.reshape(cb, n1, TM)
    twr, twi = tw[0][None], tw[1][None]
    pr = (are * twr - aim * twi).reshape(cb * n1, TM).astype(BF16)
    pi = (are * twi + aim * twr).reshape(cb * n1, TM).astype(BF16)
    return (jnp.dot(pr, fwd2[0], preferred_element_type=F32)
            + jnp.dot(pi, fwd2[1], preferred_element_type=F32))


def _fftconv_kernel(g_ref, x0_ref, k_ref, fb_ref, fwd1_ref, tw_ref, fwd2_ref, inv2_ref, inv1_ref, o_ref):
    cb = g_ref.shape[-1] // TM
    n1 = fwd1_ref.shape[0] // 2
    nt = g_ref.shape[1]
    g = jnp.concatenate([g_ref[0], g_ref[1]], axis=0)
    tw = tw_ref[...]
    x = _fft_fwd(g, fwd1_ref[...], tw, fwd2_ref[...], cb)
    kf = k_ref[...]
    xr, xi = x[:, :TM], x[:, TM:]
    kr, ki = kf[:, :TM], kf[:, TM:]
    yr = (xr * kr - xi * ki).astype(BF16)
    yi = (xr * ki + xi * kr).astype(BF16)
    bc = (jnp.dot(yr, inv2_ref[0], preferred_element_type=F32)
          + jnp.dot(yi, inv2_ref[1], preferred_element_type=F32))
    br = bc[:, :TM].reshape(cb, n1, TM)
    bi = bc[:, TM:].reshape(cb, n1, TM)
    twr, twi = tw[0][None], tw[1][None]
    pr = br * twr + bi * twi
    pi = bi * twr - br * twi
    prl = jnp.concatenate([pr[c] for c in range(cb)], axis=1).astype(BF16)
    pil = jnp.concatenate([pi[c] for c in range(cb)], axis=1).astype(BF16)
    y = (jnp.dot(inv1_ref[0], prl, preferred_element_type=F32)
         + jnp.dot(inv1_ref[1], pil, preferred_element_type=F32))
    gf = g.astype(F32)
    for part in range(2):
        rows = slice(part * nt, (part + 1) * nt)
        o_ref[part] = (x0_ref[part].astype(F32) * (y[rows] + fb_ref[...] * gf[rows])).astype(o_ref.dtype)


def _fft_conv(g, x0, kf, fb, consts):
    b, nt, w, _ = g.shape
    n1 = 2 * nt
    cb = FFT_CB
    g2 = g.reshape(b, nt, w * TM)
    x2 = x0.reshape(b, nt, w * TM)
    _, tw, fwd2, inv2, fwd1, inv1 = consts
    assert b % 2 == 0
    dspec = pl.BlockSpec((2, nt, cb * TM), lambda c, bi: (bi, 0, c))
    full = lambda a: pl.BlockSpec(a.shape, lambda c, bi: (0,) * a.ndim)
    out = pl.pallas_call(
        _fftconv_kernel,
        grid=(w // cb, b // 2),
        in_specs=[dspec, dspec,
                  pl.BlockSpec((cb * n1, 2 * TM), lambda c, bi: (c, 0)),
                  pl.BlockSpec((1, cb * TM), lambda c, bi: (0, c)),
                  full(fwd1), full(tw), full(fwd2), full(inv2), full(inv1)],
        out_specs=dspec,
        out_shape=jax.ShapeDtypeStruct((b, nt, w * TM), BF16),
        compiler_params=_params("arbitrary", "arbitrary"),
        name="long_conv",
    )(g2, x2, kf, fb, fwd1, tw, fwd2, inv2, inv1)
    return out.reshape(b, nt, w, TM)


def _spectrum_kernel(hf_ref, hb_ref, fwd1_ref, tw_ref, fwd2_ref, k_ref):
    cb = hf_ref.shape[-1] // TM
    tw = tw_ref[...]
    xf = _fft_fwd(hf_ref[...].astype(BF16), fwd1_ref[...], tw, fwd2_ref[...], cb)
    xb = _fft_fwd(hb_ref[...].astype(BF16), fwd1_ref[...], tw, fwd2_ref[...], cb)
    k_ref[...] = jnp.concatenate([xf[:, :TM] + xb[:, :TM], xf[:, TM:] - xb[:, TM:]], axis=1)


def _filter_spectrum(hf, hb, consts):
    nt, w, _ = hf.shape
    n1 = 2 * nt
    cb = FFT_CB
    fwd1, tw, fwd2 = consts[:3]
    hspec = pl.BlockSpec((nt, cb * TM), lambda c: (0, c))
    full = lambda a: pl.BlockSpec(a.shape, lambda c: (0,) * a.ndim)
    return pl.pallas_call(
        _spectrum_kernel,
        grid=(w // cb,),
        in_specs=[hspec, hspec, full(fwd1), full(tw), full(fwd2)],
        out_specs=pl.BlockSpec((cb * n1, 2 * TM), lambda c: (c, 0)),
        out_shape=jax.ShapeDtypeStruct((w * n1, 2 * TM), F32),
        compiler_params=_params("arbitrary"),
        name="filter_spectrum",
    )(hf.reshape(nt, w * TM), hb.reshape(nt, w * TM), fwd1, tw, fwd2)


def _filter_kernel(z_ref, t_ref, w1_ref, b1_ref, w2_ref, b2_ref, fr_ref, w3f_ref, w3b_ref, dl_ref,
                   hf_ref, hb_ref):
    nt = hf_ref.shape[0]
    dot = functools.partial(jnp.dot, preferred_element_type=F32, precision=HIGHEST)
    fr = fr_ref[...]
    h = jnp.sin(fr * (dot(w1_ref[...], z_ref[...]) + b1_ref[...]))
    h = jnp.sin(fr * (dot(w2_ref[...], h) + b2_ref[...]))
    decay = jnp.exp(-t_ref[...] * jnp.abs(dl_ref[...]))
    hf = dot(w3f_ref[...], h) * decay
    hb = dot(w3b_ref[...], h) * decay
    norm = (jnp.sum(jnp.abs(hf), axis=1, keepdims=True)
            + jnp.sum(jnp.abs(hb), axis=1, keepdims=True))
    hf = hf / norm
    hb = hb / norm
    lane = lax.broadcasted_iota(I32, hb.shape, 1)
    hb = jnp.where(lane == 0, 0.0, hb)
    for i in range(nt):
        hf_ref[i] = hf[:, i * TM:(i + 1) * TM]
        hb_ref[i] = hb[:, i * TM:(i + 1) * TM]


def _implicit_filters(l, fw1, fb1, fw2, fb2, fw3, ffreq):
    nt = l // TM
    w = HYENA_WIDTH
    t = jnp.linspace(0.0, 1.0, l, dtype=F32)[None, :]
    bands = (FILTER_EMB_DIM - 1) // 2
    wv = 2.0 * math.pi * jnp.arange(l, dtype=F32)[None, :] / l
    f = jnp.linspace(1e-4, bands - 1, bands, dtype=F32)[:, None]
    z = jnp.concatenate([t, jnp.cos(f * wv), -jnp.sin(f * wv)], axis=0)
    min_decay = math.log(DECAY_TARGET) / SLOW_DECAY_PCT
    max_decay = math.log(DECAY_TARGET) / FAST_DECAY_PCT
    deltas = jnp.linspace(min_decay, max_decay, w, dtype=F32)[:, None]
    w3t = fw3.T
    nc = w // FILT_CB
    col = lambda a: a.reshape(-1, 1)
    full = lambda a: pl.BlockSpec(a.shape, lambda c: (0,) * a.ndim)
    args = (z, t, fw1.T, col(fb1), fw2.T, col(fb2), col(ffreq))
    ospec = pl.BlockSpec((nt, FILT_CB, TM), lambda c: (0, c, 0))
    return pl.pallas_call(
        _filter_kernel,
        grid=(nc,),
        in_specs=[full(a) for a in args] + [
            pl.BlockSpec((FILT_CB, FILTER_HIDDEN), lambda c: (c, 0)),
            pl.BlockSpec((FILT_CB, FILTER_HIDDEN), lambda c: (nc + c, 0)),
            pl.BlockSpec((FILT_CB, 1), lambda c: (c, 0))],
        out_specs=[ospec, ospec],
        out_shape=[jax.ShapeDtypeStruct((nt, w, TM), F32)] * 2,
        compiler_params=_params("arbitrary"),
        name="implicit_filter",
    )(*args, w3t, w3t, deltas)


def _dense_dft_consts():
    n = 2 * TM
    two_pi = 2.0 * math.pi
    t = jnp.arange(TM, dtype=I32)
    f = jnp.arange(n, dtype=I32)
    ang = ((t[:, None] * f[None, :]) % n).astype(F32) * (two_pi / n)
    fwd = jnp.concatenate([jnp.cos(ang), -jnp.sin(ang)], axis=1).astype(BF16)
    inv = (jnp.stack([jnp.cos(ang).T, -jnp.sin(ang).T]) * (1.0 / n)).astype(BF16)
    return fwd, inv


def _hyena_tile_kernel(u_ref, w_ref, b_ref, hf_ref, hb_ref, fb_ref, fwd_ref, inv_ref, o_ref):
    w = HYENA_WIDTH
    n = 2 * TM
    x0 = _short_conv_tiles(u_ref.at[:, :, 0:w], w_ref[0], b_ref[0])[0]
    x1 = _short_conv_tiles(u_ref.at[:, :, w:2 * w], w_ref[1], b_ref[1])[0]
    v = _short_conv_tiles(u_ref.at[:, :, 2 * w:3 * w], w_ref[2], b_ref[2])[0]
    g = (v * x1).astype(BF16)
    fwd = fwd_ref[...]
    dot = functools.partial(jnp.dot, preferred_element_type=F32)
    kf = dot(hf_ref[0].astype(BF16), fwd)
    kb = dot(hb_ref[0].astype(BF16), fwd)
    kr = kf[:, :n] + kb[:, :n]
    ki = kf[:, n:] - kb[:, n:]
    x = dot(g, fwd)
    xr, xi = x[:, :n], x[:, n:]
    yr = (xr * kr - xi * ki).astype(BF16)
    yi = (xr * ki + xi * kr).astype(BF16)
    y = dot(yr, inv_ref[0]) + dot(yi, inv_ref[1])
    o_ref[0, 0] = (x0 * (y + fb_ref[...] * g.astype(F32))).astype(o_ref.dtype)


def _hyena_one_tile(u, cw, cb, hf, hb, fbias, consts):
    b = u.shape[0]
    w = HYENA_WIDTH
    fwd, inv = consts
    full = lambda a: pl.BlockSpec(a.shape, lambda bi: (0,) * a.ndim)
    return pl.pallas_call(
        _hyena_tile_kernel,
        grid=(b,),
        in_specs=[pl.BlockSpec((1, 1, 3 * w, TM), lambda bi: (bi, 0, 0, 0)),
                  full(cw), full(cb), full(hf), full(hb), full(fbias), full(fwd), full(inv)],
        out_specs=pl.BlockSpec((1, 1, w, TM), lambda bi: (bi, 0, 0, 0)),
        out_shape=jax.ShapeDtypeStruct((b, 1, w, TM), BF16),
        compiler_params=_params("arbitrary"),
        name="hyena_context",
    )(u, cw, cb, hf, hb, fbias, fwd, inv)


def _top2_sum(a, b, c, d):
    hi1, lo1 = jnp.maximum(a, b), jnp.minimum(a, b)
    hi2, lo2 = jnp.maximum(c, d), jnp.minimum(c, d)
    return jnp.maximum(hi1, hi2) + jnp.maximum(jnp.minimum(hi1, hi2), jnp.maximum(lo1, lo2))


def _outproj_kernel(a_ref, y_ref, x_ref, mod_ref, ga_ref, gy_ref, wo_ref, n2_ref, rw_ref, rb_ref,
                    cnt_ref, tri_ref, xo_ref, row_ref, route_ref, cnto_ref, cnt_sc):
    first = jnp.logical_and(pl.program_id(0) == 0, pl.program_id(1) == 0)

    @pl.when(first)
    def _():
        cnt_sc[...] = cnt_ref[...]

    counts = cnt_sc[...]
    for s in range(a_ref.shape[1]):
        counts = _outproj_tile(s, counts, a_ref, y_ref, x_ref, mod_ref, ga_ref, gy_ref, wo_ref, n2_ref,
                               rw_ref, rb_ref, tri_ref, xo_ref, row_ref, route_ref)
    cnt_sc[...] = counts
    cnto_ref[...] = counts


def _outproj_tile(s, base, a_ref, y_ref, x_ref, mod_ref, ga_ref, gy_ref, wo_ref, n2_ref, rw_ref, rb_ref,
                  tri_ref, xo_ref, row_ref, route_ref):
    rows = slice(s * TM, (s + 1) * TM)

    def group_norm(t, gain):
        t = t.astype(F32)
        return t * lax.rsqrt(jnp.mean(t * t, axis=0, keepdims=True) + EPS) * gain

    mix = jnp.concatenate([group_norm(a_ref[0, s], ga_ref[...]),
                           group_norm(y_ref[0, s], gy_ref[...])], axis=0).astype(BF16)
    o = lax.dot_general(mix, wo_ref[...], (((0,), (0,)), ((), ())), preferred_element_type=F32)
    xn = x_ref[0, rows] + mod_ref[0, 2:3, :] * o
    xo_ref[0, rows] = xn
    h2 = _norm_mod(xn, n2_ref[...], mod_ref[0, 3:4, :], mod_ref[0, 4:5, :])

    tm = xn.shape[0]
    nt_dims = (((1,), (1,)), ((), ()))
    h_hi = h2.astype(BF16)
    h_lo = (h2 - h_hi.astype(F32)).astype(BF16)
    rw = rw_ref[...]
    part = lax.dot_general(rw, h_hi, nt_dims, preferred_element_type=F32)
    logits = (part[:N_EXPERTS] + part[N_EXPERTS:]
              + lax.dot_general(rw[:N_EXPERTS], h_lo, nt_dims, preferred_element_type=F32))
    score = jax.nn.sigmoid(logits)
    sel = score + rb_ref[...]
    srow = [sel[e:e + 1] for e in range(N_EXPERTS)]
    prow = [score[e:e + 1] for e in range(N_EXPERTS)]
    epg = EXPERTS_PER_GROUP
    gs = [_top2_sum(*srow[g * epg:(g + 1) * epg]) for g in range(N_EXPERT_GROUPS)]
    bg = jnp.zeros((1, tm), I32)
    best = gs[0]
    for g in range(1, N_EXPERT_GROUPS):
        upd = gs[g] > best
        bg = jnp.where(upd, g, bg)
        best = jnp.where(upd, gs[g], best)

    def pick_group(rows, j):
        out = rows[j]
        for g in range(1, N_EXPERT_GROUPS):
            out = jnp.where(bg == g, rows[g * epg + j], out)
        return out

    cand = [pick_group(srow, j) for j in range(epg)]
    cprob = [pick_group(prow, j) for j in range(epg)]
    i1 = jnp.zeros((1, tm), I32)
    v1, w1 = cand[0], cprob[0]
    for j in range(1, epg):
        upd = cand[j] > v1
        i1 = jnp.where(upd, j, i1)
        v1 = jnp.where(upd, cand[j], v1)
        w1 = jnp.where(upd, cprob[j], w1)
    i2 = jnp.zeros((1, tm), I32)
    v2 = jnp.full((1, tm), -jnp.inf, F32)
    w2 = jnp.zeros((1, tm), F32)
    for j in range(epg):
        upd = jnp.logical_and(i1 != j, cand[j] > v2)
        i2 = jnp.where(upd, j, i2)
        v2 = jnp.where(upd, cand[j], v2)
        w2 = jnp.where(upd, cprob[j], w2)
    den = w1 + w2
    w1, w2 = w1 / den, w2 / den
    first_lower = i1 < i2
    lo = jnp.where(first_lower, i1, i2)
    hi = jnp.where(first_lower, i2, i1)
    w_lo = jnp.where(first_lower, w1, w2)
    w_hi = jnp.where(first_lower, w2, w1)
    pair = jnp.where(lo == 0, hi - 1, jnp.where(lo == 1, hi + 1, PAIRS_PER_GROUP - 1))
    cls = bg * PAIRS_PER_GROUP + pair

    cio = lax.broadcasted_iota(I32, (N_CLASS_ROWS, tm), 0)
    onehot = jnp.where(cio == cls, 1.0, 0.0)
    cum = jnp.dot(onehot.astype(BF16), tri_ref[...], preferred_element_type=F32)
    rank = jnp.sum(onehot * (base + cum), axis=0, keepdims=True)
    route_ref[0, s] = jnp.concatenate([cls, rank.astype(I32)], axis=0)

    half = h2.shape[1] // 2
    wrows = jnp.concatenate([w_lo, w_hi, jnp.zeros((ROW_EXTRA - 2, tm), F32)], axis=0)
    row_ref[0, rows] = jnp.concatenate([_pack_bf16_pair(h2[:, :half], h2[:, half:]),
                                        lax.bitcast_convert_type(wrows.T, U32)], axis=1)
    return base + jnp.sum(onehot, axis=1, keepdims=True)


def _outproj_route(a, y, x, mods, mod_row, ga, gy, wo, n2g, rwt, rb, cnt, tri):
    b, l, d = x.shape
    nt = l // TM
    if mod_row is None:
        mod_map = lambda bi, i: (bi, 0, 0)
    else:
        mod_map = lambda bi, i: (mod_row, 0, 0)
    full = lambda arr: pl.BlockSpec(arr.shape, lambda bi, i: (0,) * arr.ndim)
    sub = OUTPROJ_TILES if nt % OUTPROJ_TILES == 0 else 1
    tile = pl.BlockSpec((1, sub, ATTN_WIDTH, TM), lambda bi, i: (bi, i, 0, 0))
    xspec = pl.BlockSpec((1, sub * TM, d), lambda bi, i: (bi, i, 0))
    rspec = pl.BlockSpec((1, sub, 2, TM), lambda bi, i: (bi, i, 0, 0))
    row_w = d // 2 + ROW_EXTRA
    return pl.pallas_call(
        _outproj_kernel,
        grid=(b, nt // sub),
        in_specs=[tile, tile, xspec, pl.BlockSpec((1, N_MOD, d), mod_map),
                  full(ga), full(gy), full(wo), full(n2g), full(rwt), full(rb), full(cnt), full(tri)],
        out_specs=[xspec, pl.BlockSpec((1, sub * TM, row_w), lambda bi, i: (bi, i, 0)), rspec, full(cnt)],
        out_shape=[jax.ShapeDtypeStruct((b, l, d), F32),
                   jax.ShapeDtypeStruct((b, l, row_w), U32),
                   jax.ShapeDtypeStruct((b, nt, 2, TM), I32),
                   jax.ShapeDtypeStruct(cnt.shape, F32)],
        scratch_shapes=[pltpu.VMEM(cnt.shape, F32)],
        compiler_params=_params("arbitrary", "arbitrary"),
        name="outproj_route",
    )(a, y, x, mods, ga, gy, wo, n2g, rwt, rb, cnt, tri)


def _dispatch_kernel(slot_ref, h_ref, xs_in_ref, xs_ref, stage, sem, *, nstep):
    del xs_in_ref
    tm = h_ref.shape[0]
    step = pl.program_id(0)
    cur = lax.rem(step, 2)

    def copy(side, r, s):
        return pltpu.make_async_copy(stage.at[side, pl.ds(r, 1)], xs_ref.at[pl.ds(s, 1)], sem.at[side])

    def drain(side):
        def wait_one(r, carry):
            copy(side, 0, 0).wait()
            return carry

        lax.fori_loop(0, tm, wait_one, 0, unroll=8)

    @pl.when(step >= 2)
    def _():
        drain(cur)

    stage[cur] = h_ref[...]

    def issue(r, carry):
        copy(cur, r, slot_ref[0, 0, r]).start()
        return carry

    lax.fori_loop(0, tm, issue, 0, unroll=8)

    @pl.when(step == nstep - 1)
    def _():
        @pl.when(step >= 1)
        def _():
            drain(1 - cur)

        drain(cur)


def _dispatch(slots, h2, xs):
    t, d = h2.shape
    return pl.pallas_call(
        functools.partial(_dispatch_kernel, nstep=t // TM),
        grid=(t // TM,),
        in_specs=[pl.BlockSpec((1, 1, TM), lambda i: (i, 0, 0), memory_space=pltpu.SMEM),
                  pl.BlockSpec((TM, d), lambda i: (i, 0)),
                  pl.BlockSpec(memory_space=pl.ANY)],
        out_specs=pl.BlockSpec(memory_space=pl.ANY),
        out_shape=jax.ShapeDtypeStruct(xs.shape, xs.dtype),
        scratch_shapes=[pltpu.VMEM((2, TM, d), xs.dtype), pltpu.SemaphoreType.DMA((2,))],
        input_output_aliases={2: 0},
        compiler_params=_params("arbitrary"),
        name="moe_dispatch",
    )(slots, h2, xs)


def _ffn_kernel(ea_ref, eb_ref, nt_ref, xs_ref, wga_ref, wua_ref, wda_ref, wgb_ref, wub_ref, wdb_ref, ys_ref):
    del ea_ref, eb_ref
    live = pl.program_id(0) < nt_ref[0]
    half = ys_ref.shape[1]

    @pl.when(live)
    def _():
        words = xs_ref[...]
        lo, hi = _unpack_bf16_pair(words[:, :half])
        lo, hi = lo.astype(BF16), hi.astype(BF16)
        wts = lax.bitcast_convert_type(words[:, half:half + 2], F32)

        def expert(wg_ref, wu_ref, wd_ref):
            dot = functools.partial(jnp.dot, preferred_element_type=F32)
            g = dot(lo, wg_ref[0, :half, :]) + dot(hi, wg_ref[0, half:, :])
            u = dot(lo, wu_ref[0, :half, :]) + dot(hi, wu_ref[0, half:, :])
            a = (g * jax.nn.sigmoid(g)) * u
            return dot(a.astype(BF16), wd_ref[0])

        y = (wts[:, 0:1] * expert(wga_ref, wua_ref, wda_ref)
             + wts[:, 1:2] * expert(wgb_ref, wub_ref, wdb_ref))
        ys_ref[...] = _pack_bf16_pair(y[:, :half], y[:, half:])

    @pl.when(jnp.logical_not(live))
    def _():
        ys_ref[...] = jnp.zeros_like(ys_ref)


def _expert_ffn(tile_ea, tile_eb, n_tiles, xs, wg, wu, wd):
    nslot, row_w = xs.shape
    d, f = wg.shape[1:]
    ntile = nslot // FFN_TILE
    row = lambda i, ea, eb, nt: (jnp.minimum(i, nt[0] - 1), 0)
    amap = lambda i, ea, eb, nt: (ea[jnp.minimum(i, nt[0] - 1)], 0, 0)
    bmap = lambda i, ea, eb, nt: (eb[jnp.minimum(i, nt[0] - 1)], 0, 0)
    return pl.pallas_call(
        _ffn_kernel,
        grid_spec=pltpu.PrefetchScalarGridSpec(
            num_scalar_prefetch=3,
            grid=(ntile,),
            in_specs=[pl.BlockSpec((FFN_TILE, row_w), row),
                      pl.BlockSpec((1, d, f), amap), pl.BlockSpec((1, d, f), amap), pl.BlockSpec((1, f, d), amap),
                      pl.BlockSpec((1, d, f), bmap), pl.BlockSpec((1, d, f), bmap), pl.BlockSpec((1, f, d), bmap)],
            out_specs=pl.BlockSpec((FFN_TILE, d // 2), lambda i, ea, eb, nt: (i, 0))),
        out_shape=jax.ShapeDtypeStruct((nslot, d // 2), U32),
        compiler_params=_params("arbitrary"),
        name="moe_experts",
    )(tile_ea, tile_eb, n_tiles, xs, wg, wu, wd, wg, wu, wd)


def _combine_kernel(slot_ref, next_slot_ref, x_ref, mod_ref, ys_ref, o_ref, buf, sem, *, nt, nstep):
    tm = x_ref.shape[1]
    step = pl.program_id(0) * nt + pl.program_id(1)
    cur = lax.rem(step, 2)

    def copy(side, r, s):
        return pltpu.make_async_copy(ys_ref.at[pl.ds(s, 1)], buf.at[side, pl.ds(r, 1)], sem.at[side])

    def gather(side, slots):
        def issue(r, carry):
            copy(side, r, slots[0, 0, r]).start()
            return carry

        lax.fori_loop(0, tm, issue, 0, unroll=8)

    @pl.when(step == 0)
    def _():
        gather(cur, slot_ref)

    @pl.when(step + 1 < nstep)
    def _():
        gather(1 - cur, next_slot_ref)

    def wait_one(r, carry):
        copy(cur, 0, 0).wait()
        return carry

    lax.fori_loop(0, tm, wait_one, 0, unroll=8)
    lo, hi = _unpack_bf16_pair(buf[cur])
    y = jnp.concatenate([lo, hi], axis=1)
    o_ref[0] = x_ref[0] + mod_ref[0, 5:6, :] * y


def _combine(slots, x, mods, mod_row, ys):
    b, l, d = x.shape
    nt = l // TM
    if mod_row is None:
        mod_map = lambda bi, i: (bi, 0, 0)
    else:
        mod_map = lambda bi, i: (mod_row, 0, 0)
    xspec = pl.BlockSpec((1, TM, d), lambda bi, i: (bi, i, 0))
    last = b * nt - 1
    return pl.pallas_call(
        functools.partial(_combine_kernel, nt=nt, nstep=b * nt),
        grid=(b, nt),
        in_specs=[pl.BlockSpec((1, 1, TM), lambda bi, i: (bi * nt + i, 0, 0), memory_space=pltpu.SMEM),
                  pl.BlockSpec((1, 1, TM), lambda bi, i: (jnp.minimum(bi * nt + i + 1, last), 0, 0),
                               memory_space=pltpu.SMEM),
                  xspec,
                  pl.BlockSpec((1, N_MOD, d), mod_map),
                  pl.BlockSpec(memory_space=pl.ANY)],
        out_specs=xspec,
        out_shape=jax.ShapeDtypeStruct((b, l, d), F32),
        scratch_shapes=[pltpu.VMEM((2, TM, d // 2), U32), pltpu.SemaphoreType.DMA((2,))],
        compiler_params=_params("arbitrary", "arbitrary"),
        name="moe_combine",
    )(slots, slots, x, mods, ys)


def _moe(streams, wg, wu, wd):
    counts = streams[-1]["counts"].reshape(-1).astype(I32)
    padded = ((counts + FFN_TILE - 1) // FFN_TILE) * FFN_TILE
    ends = jnp.cumsum(padded)
    offs = ends - padded
    total = sum(s["x"].shape[0] * s["x"].shape[1] for s in streams)
    ntile = total // FFN_TILE + N_CLASSES
    nslot = ntile * FFN_TILE
    tile_start = jnp.arange(ntile, dtype=I32) * FFN_TILE
    tile_class = jnp.minimum(jnp.sum((tile_start[:, None] >= ends[None, :N_CLASSES]).astype(I32), axis=1),
                             N_CLASSES - 1)
    group_base = (tile_class // PAIRS_PER_GROUP) * EXPERTS_PER_GROUP
    tile_ea = (group_base + jnp.asarray(PAIR_LO, I32)[tile_class % PAIRS_PER_GROUP]).astype(I32)
    tile_eb = (group_base + jnp.asarray(PAIR_HI, I32)[tile_class % PAIRS_PER_GROUP]).astype(I32)
    n_tiles = (ends[-1] // FFN_TILE).astype(I32).reshape(1)

    row_w = streams[0]["rows"].shape[-1]
    xs = jnp.zeros((nslot, row_w), U32)
    for s in streams:
        b, l, _ = s["x"].shape
        cls, rank = s["route"][:, :, 0], s["route"][:, :, 1]
        slot = rank
        for c in range(N_CLASSES):
            slot = slot + jnp.where(cls == c, offs[c], 0)
        s["slots"] = slot.reshape(b * (l // TM), 1, TM)
        xs = _dispatch(s["slots"], s["rows"].reshape(b * l, row_w), xs)
    ys = _expert_ffn(tile_ea, tile_eb, n_tiles, xs, wg, wu, wd)
    return [_combine(s["slots"], s["x"], s["mods"], s["mod_row"], ys) for s in streams]


def _rope_table(l):
    t = jnp.arange(l)
    row = (t // GRID_W).astype(F32)
    col = (t % GRID_W).astype(F32)
    inv_freq = ROPE_THETA ** (-jnp.arange(ROPE_QUARTER, dtype=F32) / ROPE_QUARTER)
    ar = inv_freq[:, None] * row[None, :]
    ac = inv_freq[:, None] * col[None, :]
    return jnp.concatenate([jnp.cos(ar), jnp.sin(ar), jnp.cos(ac), jnp.sin(ac)], axis=0)


def _identity_rope_table(l):
    one = jnp.ones((ROPE_QUARTER, l), F32)
    zero = jnp.zeros((ROPE_QUARTER, l), F32)
    return jnp.concatenate([one, zero, one, zero], axis=0)


def kernel(x, c, ctx, c_ctx, w_mod, b_mod, norm1_g, w_in, q_norm_g, k_norm_g, conv_w, conv_b, filt_w1, filt_b1, filt_w2, filt_b2, filt_w3, filt_freq, filt_bias, attn_out_g, hyena_out_g, w_out, norm2_g, router_w, router_bias, expert_w_gate, expert_w_up, expert_w_down):
    depth = w_mod.shape[0]
    b, l, d = x.shape
    lc = ctx.shape[1]
    assert l % TM == 0 and lc == TM and d == D_MODEL
    nt = l // TM
    w = HYENA_WIDTH

    mods = _modulations(c, c_ctx, w_mod, b_mod)
    rope_lat = _rope_table(l)
    rope_ctx = _identity_rope_table(lc)
    dft = _dft_consts(nt)
    dft_ctx = _dense_dft_consts()
    tri = (jnp.arange(TM)[:, None] < jnp.arange(TM)[None, :]).astype(BF16)
    rwt = router_w.T
    rw_hi = rwt.astype(BF16)
    rwt = jnp.concatenate([rw_hi, (rwt - rw_hi.astype(F32)).astype(BF16)], axis=0)
    rb = router_bias.reshape(-1, 1)
    zero_cnt = jnp.zeros((N_CLASS_ROWS, 1), F32)
    col = lambda a: a.reshape(-1, 1)

    for li in range(depth):
        last = li == depth - 1
        m = mods[li]
        wt = w_in[li].T.astype(BF16)
        wo = w_out[li].astype(BF16)
        g1n = norm1_g[li].reshape(1, d)
        g2n = norm2_g[li].reshape(1, d)
        qg, kg = col(q_norm_g[li]), col(k_norm_g[li])
        cw = conv_w[li].T.reshape(3, w, SHORT_CONV)
        cb = conv_b[li].reshape(3, w, 1)
        ga, gy = col(attn_out_g[li]), col(hyena_out_g[li])
        fargs = (filt_w1[li], filt_b1[li], filt_w2[li], filt_b2[li], filt_w3[li], filt_freq[li])

        q, k, v, u = _inproj(x, m, None, g1n, wt, qg, kg, rope_lat)
        qc, kc, vc, uc = _inproj(ctx, m, b, g1n, wt, qg, kg, rope_ctx)
        a = _attention(q, jnp.concatenate([kc, k], axis=2), jnp.concatenate([vc, v], axis=2))
        hf, hb = _implicit_filters(l, *fargs)
        kf = _filter_spectrum(hf, hb, dft)
        g, x0 = _short_conv_gate(u, cw, cb)
        fb_lanes = jnp.repeat(filt_bias[li], TM).reshape(1, w * TM)
        y = _fft_conv(g, x0, kf, fb_lanes, dft)
        lat = dict(zip(("x", "rows", "route", "counts"),
                       _outproj_route(a, y, x, m, None, ga, gy, wo, g2n, rwt, rb, zero_cnt, tri)))
        lat.update(mods=m, mod_row=None)
        streams = [lat]

        if not last:
            ac = _attention(qc, kc, vc)
            hfc, hbc = _implicit_filters(lc, *fargs)
            yc = _hyena_one_tile(uc, cw, cb, hfc, hbc, col(filt_bias[li]), dft_ctx)
            cs = dict(zip(("x", "rows", "route", "counts"),
                          _outproj_route(ac, yc, ctx, m, b, ga, gy, wo, g2n, rwt, rb, lat["counts"], tri)))
            cs.update(mods=m, mod_row=b)
            streams.append(cs)

        outs = _moe(streams, expert_w_gate[li].astype(BF16), expert_w_up[li].astype(BF16),
                    expert_w_down[li].astype(BF16))
        x = outs[0]
        if not last:
            ctx = outs[1]
    return x
```

```python
import functools
import math

import jax
import jax.numpy as jnp
from jax import lax
from jax.experimental import pallas as pl
from jax.experimental.pallas import tpu as pltpu

F32 = jnp.float32
BF16 = jnp.bfloat16
I32 = jnp.int32
HIGHEST = lax.Precision.HIGHEST

D_MODEL = 1024
N_MOD = 6
EPS = 1e-6
N_Q_HEADS = 8
N_KV_HEADS = 2
HEAD_DIM = 64
KV_REP = N_Q_HEADS // N_KV_HEADS
ATTN_WIDTH = N_Q_HEADS * HEAD_DIM
KV_WIDTH = N_KV_HEADS * HEAD_DIM
ATTN_SCALE = HEAD_DIM ** -0.5
LOG2_E = math.log2(math.e)
V_ROWS = HEAD_DIM + 16
K_COLS = HEAD_DIM + 16
SCORE_BOUND_SLACK = 1.0 + 2.0 ** -6
UNDERFLOW_GUARD = 2.0 ** -80
GRID_W = 64
ROPE_THETA = 10000.0
ROPE_QUARTER = HEAD_DIM // 4
HYENA_WIDTH = D_MODEL - ATTN_WIDTH
SHORT_CONV = 3
FILTER_EMB_DIM = 33
FILTER_HIDDEN = 64
DECAY_TARGET = 1e-2
FAST_DECAY_PCT = 0.3
SLOW_DECAY_PCT = 1.5
N_EXPERTS = 16
N_EXPERT_GROUPS = 4
EXPERTS_PER_GROUP = N_EXPERTS // N_EXPERT_GROUPS
D_FF_EXPERT = 512

TM = 256
ATTN_UNROLL = 8
ATTN_TQ = 1024
FFN_TILE = 256
OUTPROJ_TILES = 2
CONV_CB = 128
FFT_CB = 32
FILT_CB = 128
VMEM_LIMIT = 56 * 1024 * 1024


U32 = jnp.uint32
PAIRS_PER_GROUP = EXPERTS_PER_GROUP * (EXPERTS_PER_GROUP - 1) // 2
N_CLASSES = N_EXPERT_GROUPS * PAIRS_PER_GROUP
N_CLASS_ROWS = -(-N_CLASSES // 8) * 8
PAIR_LO = tuple(a for a in range(EXPERTS_PER_GROUP) for b in range(a + 1, EXPERTS_PER_GROUP))
PAIR_HI = tuple(b for a in range(EXPERTS_PER_GROUP) for b in range(a + 1, EXPERTS_PER_GROUP))
ROW_EXTRA = 128


def _pack_bf16_pair(lo, hi):
    lo_bits = lax.bitcast_convert_type(lo.astype(BF16).astype(F32), U32)
    hi_bits = lax.bitcast_convert_type(hi.astype(BF16).astype(F32), U32)
    return lax.shift_right_logical(lo_bits, U32(16)) | (hi_bits & U32(0xFFFF0000))


def _unpack_bf16_pair(words):
    lo = lax.bitcast_convert_type(lax.shift_left(words, U32(16)), F32)
    hi = lax.bitcast_convert_type(words & U32(0xFFFF0000), F32)
    return lo, hi


def _params(*sem):
    return pltpu.CompilerParams(dimension_semantics=tuple(sem), vmem_limit_bytes=VMEM_LIMIT)


def _norm_mod(x, g, shift, scale):
    y = x * lax.rsqrt(jnp.mean(x * x, axis=-1, keepdims=True) + EPS)
    return (y * g) * (1 + scale) + shift


def _mod_kernel(c_ref, w_ref, b_ref, o_ref):
    c = c_ref[...]
    s = c * jax.nn.sigmoid(c)
    o_ref[0] = jnp.dot(s, w_ref[0], preferred_element_type=F32, precision=HIGHEST) + b_ref[0]


def _modulations(c, c_ctx, w_mod, b_mod):
    depth, d, nmd = w_mod.shape
    b = c.shape[0]
    rows = -(-(b + 1) // 8) * 8
    c_all = jnp.zeros((rows, d), F32).at[:b].set(c).at[b].set(c_ctx)
    tn = nmd // 4
    out = pl.pallas_call(
        _mod_kernel,
        grid=(depth, nmd // tn),
        in_specs=[pl.BlockSpec((rows, d), lambda l, j: (0, 0)),
                  pl.BlockSpec((1, d, tn), lambda l, j: (l, 0, j)),
                  pl.BlockSpec((1, 1, tn), lambda l, j: (l, 0, j))],
        out_specs=pl.BlockSpec((1, rows, tn), lambda l, j: (l, 0, j)),
        out_shape=jax.ShapeDtypeStruct((depth, rows, nmd), F32),
        compiler_params=_params("arbitrary", "arbitrary"),
        name="modulation",
    )(c_all, w_mod, b_mod.reshape(depth, 1, nmd))
    return out.reshape(depth, rows, N_MOD, d)


def _inproj_kernel(x_ref, mod_ref, g_ref, wt_ref, qg_ref, kg_ref, rope_ref,
                   q_ref, k_ref, v_ref, u_ref):
    x = x_ref[0]
    tm = x.shape[0]
    h = _norm_mod(x, g_ref[...], mod_ref[0, 0:1, :], mod_ref[0, 1:2, :])
    pt = lax.dot_general(wt_ref[...], h.astype(BF16), (((1,), (1,)), ((), ())),
                         preferred_element_type=F32)
    rope = rope_ref[...]
    qd = ROPE_QUARTER
    cr, sr = rope[0:qd][None], rope[qd:2 * qd][None]
    cc, sc = rope[2 * qd:3 * qd][None], rope[3 * qd:4 * qd][None]

    def norm_rope(t, gain, nh):
        t = t.reshape(nh, HEAD_DIM, tm)
        t = t * lax.rsqrt(jnp.mean(t * t, axis=1, keepdims=True) + EPS) * gain[None]
        a, b = t[:, 0:qd], t[:, qd:2 * qd]
        c, d = t[:, 2 * qd:3 * qd], t[:, 3 * qd:4 * qd]
        return jnp.concatenate([a * cr - b * sr, b * cr + a * sr,
                                c * cc - d * sc, d * cc + c * sc], axis=1)

    q = norm_rope(pt[0:ATTN_WIDTH], qg_ref[...], N_Q_HEADS) * (ATTN_SCALE * LOG2_E)
    q_ref[0] = q.astype(BF16)
    k = norm_rope(pt[ATTN_WIDTH:ATTN_WIDTH + KV_WIDTH], kg_ref[...], N_KV_HEADS)
    kt = k.reshape(KV_WIDTH, tm).T
    for g in range(N_KV_HEADS):
        k_ref[0, g, 0, :, 0:HEAD_DIM] = kt[:, g * HEAD_DIM:(g + 1) * HEAD_DIM].astype(BF16)
    pad_col = lax.broadcasted_iota(I32, (N_KV_HEADS, tm, K_COLS - HEAD_DIM), 2)
    k_ref[0, :, 0, :, HEAD_DIM:K_COLS] = jnp.where(pad_col == 0, 1.0, 0.0).astype(BF16)
    v = pt[ATTN_WIDTH + KV_WIDTH:ATTN_WIDTH + 2 * KV_WIDTH]
    v_ref[0, :, 0, 0:HEAD_DIM] = v.reshape(N_KV_HEADS, HEAD_DIM, tm).astype(BF16)
    pad_row = lax.broadcasted_iota(I32, (N_KV_HEADS, V_ROWS - HEAD_DIM, tm), 1)
    v_ref[0, :, 0, HEAD_DIM:V_ROWS] = jnp.where(pad_row == 0, 1.0, 0.0).astype(BF16)
    u_ref[0, 0] = pt[ATTN_WIDTH + 2 * KV_WIDTH:].astype(BF16)


def _inproj(x, mods, mod_row, g, wt, qg, kg, rope):
    b, l, d = x.shape
    nt = l // TM
    p = wt.shape[0]
    uw = p - ATTN_WIDTH - 2 * KV_WIDTH
    if mod_row is None:
        mod_map = lambda bi, i: (bi, 0, 0)
    else:
        mod_map = lambda bi, i: (mod_row, 0, 0)
    return pl.pallas_call(
        _inproj_kernel,
        grid=(b, nt),
        in_specs=[pl.BlockSpec((1, TM, d), lambda bi, i: (bi, i, 0)),
                  pl.BlockSpec((1, N_MOD, d), mod_map),
                  pl.BlockSpec((1, d), lambda bi, i: (0, 0)),
                  pl.BlockSpec((p, d), lambda bi, i: (0, 0)),
                  pl.BlockSpec((HEAD_DIM, 1), lambda bi, i: (0, 0)),
                  pl.BlockSpec((HEAD_DIM, 1), lambda bi, i: (0, 0)),
                  pl.BlockSpec((HEAD_DIM, TM), lambda bi, i: (0, i))],
        out_specs=[pl.BlockSpec((1, N_Q_HEADS, HEAD_DIM, TM), lambda bi, i: (bi, 0, 0, i)),
                   pl.BlockSpec((1, N_KV_HEADS, 1, TM, K_COLS), lambda bi, i: (bi, 0, i, 0, 0)),
                   pl.BlockSpec((1, N_KV_HEADS, 1, V_ROWS, TM), lambda bi, i: (bi, 0, i, 0, 0)),
                   pl.BlockSpec((1, 1, uw, TM), lambda bi, i: (bi, i, 0, 0))],
        out_shape=[jax.ShapeDtypeStruct((b, N_Q_HEADS, HEAD_DIM, l), BF16),
                   jax.ShapeDtypeStruct((b, N_KV_HEADS, nt, TM, K_COLS), BF16),
                   jax.ShapeDtypeStruct((b, N_KV_HEADS, nt, V_ROWS, TM), BF16),
                   jax.ShapeDtypeStruct((b, nt, uw, TM), BF16)],
        compiler_params=_params("arbitrary", "arbitrary"),
        name="inproj",
    )(x, mods, g, wt, qg, kg, rope)


def _attn_kernel(q_ref, *refs, nparts):
    kv_refs = refs[:2 * nparts]
    kmax_ref, o_ref, acc_ref, m_ref = refs[2 * nparts:]
    tq = q_ref.shape[-1]
    nq = KV_REP * tq
    q = jnp.concatenate([q_ref[0, r] for r in range(KV_REP)], axis=1)
    pad = jnp.zeros((K_COLS - HEAD_DIM - 1, nq), F32)

    def write_out():
        acc = acc_ref[...]
        out = acc[0:HEAD_DIM] / acc[HEAD_DIM:HEAD_DIM + 1]
        for r in range(KV_REP):
            for t in range(tq // TM):
                o_ref[0, t, r * HEAD_DIM:(r + 1) * HEAD_DIM, :] = (
                    out[:, r * tq + t * TM:r * tq + (t + 1) * TM].astype(o_ref.dtype))

    def run(chunk):
        for part in range(nparts):
            k_ref, v_ref = kv_refs[2 * part], kv_refs[2 * part + 1]
            nk = k_ref.shape[2]

            def group(i, carry, k_ref=k_ref, v_ref=v_ref):
                for t in range(ATTN_UNROLL):
                    chunk(k_ref, v_ref, ATTN_UNROLL * i + t)
                return carry

            if nk >= ATTN_UNROLL:
                lax.fori_loop(0, nk // ATTN_UNROLL, group, 0)
            for j in range(nk - nk % ATTN_UNROLL, nk):
                chunk(k_ref, v_ref, j)

    qf = q.astype(F32)
    qnorm = jnp.sqrt(jnp.sum(qf * qf, axis=0, keepdims=True))
    bound = qnorm * (kmax_ref[0, 0][:, 0:1] * SCORE_BOUND_SLACK)
    q_shift = jnp.concatenate([q, jnp.concatenate([-bound, pad], axis=0).astype(BF16)], axis=0)
    acc_ref[...] = jnp.zeros_like(acc_ref)

    def chunk_shifted(k_ref, v_ref, j):
        s = jnp.dot(k_ref[0, 0, j], q_shift, preferred_element_type=F32)
        p = jnp.exp2(s).astype(BF16)
        acc_ref[...] += jnp.dot(v_ref[0, 0, j], p, preferred_element_type=F32)

    run(chunk_shifted)
    denom_ok = jnp.min(acc_ref[HEAD_DIM:HEAD_DIM + 1, :]) >= UNDERFLOW_GUARD

    @pl.when(denom_ok)
    def _():
        write_out()

    @pl.when(jnp.logical_not(denom_ok))
    def _():
        q_plain = jnp.concatenate([q, jnp.zeros((K_COLS - HEAD_DIM, nq), BF16)], axis=0)
        acc_ref[...] = jnp.zeros_like(acc_ref)
        m_ref[...] = jnp.full_like(m_ref, -jnp.inf)

        def chunk_online(k_ref, v_ref, j):
            s = jnp.dot(k_ref[0, 0, j], q_plain, preferred_element_type=F32)
            m = m_ref[...]
            m_new = jnp.maximum(m, jnp.max(s, axis=0, keepdims=True))
            alpha = jnp.exp2(m - m_new)
            p = jnp.exp2(s - m_new).astype(BF16)
            acc_ref[...] = alpha * acc_ref[...] + jnp.dot(v_ref[0, 0, j], p, preferred_element_type=F32)
            m_ref[...] = m_new

        run(chunk_online)
        write_out()


def _attention(q, kv_parts):
    b, _, _, l = q.shape
    nt = l // TM
    tq = min(ATTN_TQ, l)
    kmax = None
    for k, _ in kv_parts:
        kf = k[..., :HEAD_DIM].astype(F32)
        part_max = jnp.max(jnp.sum(kf * kf, axis=-1), axis=(2, 3))
        kmax = part_max if kmax is None else jnp.maximum(kmax, part_max)
    kmax = jnp.broadcast_to(jnp.sqrt(kmax)[:, :, None, None], (b, N_KV_HEADS, 1, 128))
    kv_specs, kv_args = [], []
    for k, v in kv_parts:
        nk = k.shape[2]
        kv_specs += [pl.BlockSpec((1, 1, nk, TM, K_COLS), lambda bi, g, i: (bi, g, 0, 0, 0)),
                     pl.BlockSpec((1, 1, nk, V_ROWS, TM), lambda bi, g, i: (bi, g, 0, 0, 0))]
        kv_args += [k, v]
    return pl.pallas_call(
        functools.partial(_attn_kernel, nparts=len(kv_parts)),
        grid=(b, N_KV_HEADS, l // tq),
        in_specs=[pl.BlockSpec((1, KV_REP, HEAD_DIM, tq), lambda bi, g, i: (bi, g, 0, i))] + kv_specs
                 + [pl.BlockSpec((1, 1, 1, 128), lambda bi, g, i: (bi, g, 0, 0))],
        out_specs=pl.BlockSpec((1, tq // TM, KV_REP * HEAD_DIM, TM), lambda bi, g, i: (bi, i, g, 0)),
        out_shape=jax.ShapeDtypeStruct((b, nt, ATTN_WIDTH, TM), BF16),
        scratch_shapes=[pltpu.VMEM((V_ROWS, KV_REP * tq), F32), pltpu.VMEM((1, KV_REP * tq), F32)],
        compiler_params=_params("arbitrary", "arbitrary", "arbitrary"),
        name="attention",
    )(q, *kv_args, kmax)


def _short_conv_tiles(u_ref, w, bias):
    nt, c, tm = u_ref.shape[1:]
    lane = lax.broadcasted_iota(I32, (c, tm), 1)
    tiles = [u_ref[0, i].astype(F32) for i in range(nt)]
    prev = [pltpu.roll(t, 1, 1) for t in tiles]
    nxt = [pltpu.roll(t, tm - 1, 1) for t in tiles]
    zero = jnp.zeros((c, tm), F32)
    out = []
    for i in range(nt):
        up = jnp.where(lane == 0, prev[i - 1] if i > 0 else zero, prev[i])
        un = jnp.where(lane == tm - 1, nxt[i + 1] if i < nt - 1 else zero, nxt[i])
        out.append(bias + up * w[:, 0:1] + tiles[i] * w[:, 1:2] + un * w[:, 2:3])
    return out


def _sconv_kernel(u0_ref, u1_ref, u2_ref, w_ref, b_ref, g_ref, x0_ref):
    x0 = _short_conv_tiles(u0_ref, w_ref[0], b_ref[0])
    x1 = _short_conv_tiles(u1_ref, w_ref[1], b_ref[1])
    v = _short_conv_tiles(u2_ref, w_ref[2], b_ref[2])
    for i in range(len(x0)):
        g_ref[0, i] = (v[i] * x1[i]).astype(g_ref.dtype)
        x0_ref[0, i] = x0[i].astype(x0_ref.dtype)


def _short_conv_gate(u, cw, cb):
    b, nt, _, _ = u.shape
    w = HYENA_WIDTH
    nc = w // CONV_CB
    uspec = lambda grp: pl.BlockSpec((1, nt, CONV_CB, TM), lambda bi, c: (bi, 0, grp * nc + c, 0))
    ospec = pl.BlockSpec((1, nt, CONV_CB, TM), lambda bi, c: (bi, 0, c, 0))
    return pl.pallas_call(
        _sconv_kernel,
        grid=(b, nc),
        in_specs=[uspec(0), uspec(1), uspec(2),
                  pl.BlockSpec((3, CONV_CB, SHORT_CONV), lambda bi, c: (0, c, 0)),
                  pl.BlockSpec((3, CONV_CB, 1), lambda bi, c: (0, c, 0))],
        out_specs=[ospec, ospec],
        out_shape=[jax.ShapeDtypeStruct((b, nt, w, TM), BF16)] * 2,
        compiler_params=_params("arbitrary", "arbitrary"),
        name="short_conv",
    )(u, u, u, cw, cb)


def _dft_consts(nt):
    n1 = 2 * nt
    n = n1 * TM
    two_pi = 2.0 * math.pi

    def cs(prod, mod):
        ang = (prod % mod).astype(F32) * (two_pi / mod)
        return jnp.cos(ang), jnp.sin(ang)

    f1 = jnp.arange(n1, dtype=I32)
    t1 = jnp.arange(nt, dtype=I32)
    t2 = jnp.arange(TM, dtype=I32)
    c, s = cs(f1[:, None] * t1[None, :], n1)
    fwd1 = jnp.concatenate([c, -s], axis=0).astype(BF16)
    fwd1c = jnp.concatenate([jnp.concatenate([c, s], axis=1),
                             jnp.concatenate([-s, c], axis=1)], axis=0).astype(BF16)
    c, s = cs(f1[:, None] * t2[None, :], n)
    tw = jnp.stack([c, -s])
    c, s = cs(t2[:, None] * t2[None, :], TM)
    fwd2 = jnp.stack([jnp.concatenate([c, -s], axis=1),
                      jnp.concatenate([s, c], axis=1)]).astype(BF16)
    inv2 = jnp.stack([jnp.concatenate([c, s], axis=1),
                      jnp.concatenate([-s, c], axis=1)]).astype(BF16)
    c, s = cs(t1[:, None] * f1[None, :], n1)
    inv1c = (jnp.stack([jnp.concatenate([c, s], axis=0),
                        jnp.concatenate([-s, c], axis=0)]) * (1.0 / n)).astype(BF16)
    return fwd1, tw, fwd2, inv2, fwd1c, inv1c


def _fft_fwd(g, fwd1, tw, fwd2, cb):
    n1 = fwd1.shape[0] // 2
    a = jnp.dot(fwd1, g, preferred_element_type=F32)
    are = jnp.concatenate([a[:n1, c * TM:(c + 1) * TM] for c in range(cb)], axis=0)
    aim = jnp.concatenate([a[n1:, c * TM:(c + 1) * TM] for c in range(cb)], axis=0)
    are = are.reshape(cb, n1, TM)
    aim = aim.reshape(cb, n1, TM)
    twr, twi = tw[0][None], tw[1][None]
    pr = (are * twr - aim * twi).reshape(cb * n1, TM).astype(BF16)
    pi = (are * twi + aim * twr).reshape(cb * n1, TM).astype(BF16)
    return (jnp.dot(pr, fwd2[0], preferred_element_type=F32)
            + jnp.dot(pi, fwd2[1], preferred_element_type=F32))


def _fftconv_kernel(g_ref, x0_ref, k_ref, fb_ref, fwd1_ref, tw_ref, fwd2_ref, inv2_ref, inv1_ref, o_ref):
    cb = g_ref.shape[-1] // TM
    n1 = fwd1_ref.shape[0] // 2
    nt = g_ref.shape[1]
    g = jnp.concatenate([g_ref[0], g_ref[1]], axis=0)
    tw = tw_ref[...]
    x = _fft_fwd(g, fwd1_ref[...], tw, fwd2_ref[...], cb)
    kf = k_ref[...]
    xr, xi = x[:, :TM], x[:, TM:]
    kr, ki = kf[:, :TM], kf[:, TM:]
    yr = (xr * kr - xi * ki).astype(BF16)
    yi = (xr * ki + xi * kr).astype(BF16)
    bc = (jnp.dot(yr, inv2_ref[0], preferred_element_type=F32)
          + jnp.dot(yi, inv2_ref[1], preferred_element_type=F32))
    br = bc[:, :TM].reshape(cb, n1, TM)
    bi = bc[:, TM:].reshape(cb, n1, TM)
    twr, twi = tw[0][None], tw[1][None]
    pr = br * twr + bi * twi
    pi = bi * twr - br * twi
    prl = jnp.concatenate([pr[c] for c in range(cb)], axis=1).astype(BF16)
    pil = jnp.concatenate([pi[c] for c in range(cb)], axis=1).astype(BF16)
    y = (jnp.dot(inv1_ref[0], prl, preferred_element_type=F32)
         + jnp.dot(inv1_ref[1], pil, preferred_element_type=F32))
    gf = g.astype(F32)
    for part in range(2):
        rows = slice(part * nt, (part + 1) * nt)
        o_ref[part] = (x0_ref[part].astype(F32) * (y[rows] + fb_ref[...] * gf[rows])).astype(o_ref.dtype)


def _fft_conv(g, x0, kf, fb, consts):
    b, nt, w, _ = g.shape
    n1 = 2 * nt
    cb = FFT_CB
    g2 = g.reshape(b, nt, w * TM)
    x2 = x0.reshape(b, nt, w * TM)
    _, tw, fwd2, inv2, fwd1, inv1 = consts
    assert b % 2 == 0
    dspec = pl.BlockSpec((2, nt, cb * TM), lambda c, bi: (bi, 0, c))
    full = lambda a: pl.BlockSpec(a.shape, lambda c, bi: (0,) * a.ndim)
    out = pl.pallas_call(
        _fftconv_kernel,
        grid=(w // cb, b // 2),
        in_specs=[dspec, dspec,
                  pl.BlockSpec((cb * n1, 2 * TM), lambda c, bi: (c, 0)),
                  pl.BlockSpec((1, cb * TM), lambda c, bi: (0, c)),
                  full(fwd1), full(tw), full(fwd2), full(inv2), full(inv1)],
        out_specs=dspec,
        out_shape=jax.ShapeDtypeStruct((b, nt, w * TM), BF16),
        compiler_params=_params("arbitrary", "arbitrary"),
        name="long_conv",
    )(g2, x2, kf, fb, fwd1, tw, fwd2, inv2, inv1)
    return out.reshape(b, nt, w, TM)


def _spectrum_kernel(hf_ref, hb_ref, fwd1_ref, tw_ref, fwd2_ref, k_ref):
    cb = hf_ref.shape[-1] // TM
    tw = tw_ref[...]
    xf = _fft_fwd(hf_ref[...].astype(BF16), fwd1_ref[...], tw, fwd2_ref[...], cb)
    xb = _fft_fwd(hb_ref[...].astype(BF16), fwd1_ref[...], tw, fwd2_ref[...], cb)
    k_ref[...] = jnp.concatenate([xf[:, :TM] + xb[:, :TM], xf[:, TM:] - xb[:, TM:]], axis=1)


def _filter_spectrum(hf, hb, consts):
    nt, w, _ = hf.shape
    n1 = 2 * nt
    cb = FFT_CB
    fwd1, tw, fwd2 = consts[:3]
    hspec = pl.BlockSpec((nt, cb * TM), lambda c: (0, c))
    full = lambda a: pl.BlockSpec(a.shape, lambda c: (0,) * a.ndim)
    return pl.pallas_call(
        _spectrum_kernel,
        grid=(w // cb,),
        in_specs=[hspec, hspec, full(fwd1), full(tw), full(fwd2)],
        out_specs=pl.BlockSpec((cb * n1, 2 * TM), lambda c: (c, 0)),
        out_shape=jax.ShapeDtypeStruct((w * n1, 2 * TM), F32),
        compiler_params=_params("arbitrary"),
        name="filter_spectrum",
    )(hf.reshape(nt, w * TM), hb.reshape(nt, w * TM), fwd1, tw, fwd2)


def _filter_kernel(z_ref, t_ref, w1_ref, b1_ref, w2_ref, b2_ref, fr_ref, w3f_ref, w3b_ref, dl_ref,
                   hf_ref, hb_ref):
    nt = hf_ref.shape[0]
    dot = functools.partial(jnp.dot, preferred_element_type=F32, precision=HIGHEST)
    fr = fr_ref[...]
    h = jnp.sin(fr * (dot(w1_ref[...], z_ref[...]) + b1_ref[...]))
    h = jnp.sin(fr * (dot(w2_ref[...], h) + b2_ref[...]))
    decay = jnp.exp(-t_ref[...] * jnp.abs(dl_ref[...]))
    hf = dot(w3f_ref[...], h) * decay
    hb = dot(w3b_ref[...], h) * decay
    norm = (jnp.sum(jnp.abs(hf), axis=1, keepdims=True)
            + jnp.sum(jnp.abs(hb), axis=1, keepdims=True))
    hf = hf / norm
    hb = hb / norm
    lane = lax.broadcasted_iota(I32, hb.shape, 1)
    hb = jnp.where(lane == 0, 0.0, hb)
    for i in range(nt):
        hf_ref[i] = hf[:, i * TM:(i + 1) * TM]
        hb_ref[i] = hb[:, i * TM:(i + 1) * TM]


def _implicit_filters(l, fw1, fb1, fw2, fb2, fw3, ffreq):
    nt = l // TM
    w = HYENA_WIDTH
    t = jnp.linspace(0.0, 1.0, l, dtype=F32)[None, :]
    bands = (FILTER_EMB_DIM - 1) // 2
    wv = 2.0 * math.pi * jnp.arange(l, dtype=F32)[None, :] / l
    f = jnp.linspace(1e-4, bands - 1, bands, dtype=F32)[:, None]
    z = jnp.concatenate([t, jnp.cos(f * wv), -jnp.sin(f * wv)], axis=0)
    min_decay = math.log(DECAY_TARGET) / SLOW_DECAY_PCT
    max_decay = math.log(DECAY_TARGET) / FAST_DECAY_PCT
    deltas = jnp.linspace(min_decay, max_decay, w, dtype=F32)[:, None]
    w3t = fw3.T
    nc = w // FILT_CB
    col = lambda a: a.reshape(-1, 1)
    full = lambda a: pl.BlockSpec(a.shape, lambda c: (0,) * a.ndim)
    args = (z, t, fw1.T, col(fb1), fw2.T, col(fb2), col(ffreq))
    ospec = pl.BlockSpec((nt, FILT_CB, TM), lambda c: (0, c, 0))
    return pl.pallas_call(
        _filter_kernel,
        grid=(nc,),
        in_specs=[full(a) for a in args] + [
            pl.BlockSpec((FILT_CB, FILTER_HIDDEN), lambda c: (c, 0)),
            pl.BlockSpec((FILT_CB, FILTER_HIDDEN), lambda c: (nc + c, 0)),
            pl.BlockSpec((FILT_CB, 1), lambda c: (c, 0))],
        out_specs=[ospec, ospec],
        out_shape=[jax.ShapeDtypeStruct((nt, w, TM), F32)] * 2,
        compiler_params=_params("arbitrary"),
        name="implicit_filter",
    )(*args, w3t, w3t, deltas)


def _dense_dft_consts():
    n = 2 * TM
    two_pi = 2.0 * math.pi
    t = jnp.arange(TM, dtype=I32)
    f = jnp.arange(n, dtype=I32)
    ang = ((t[:, None] * f[None, :]) % n).astype(F32) * (two_pi / n)
    fwd = jnp.concatenate([jnp.cos(ang), -jnp.sin(ang)], axis=1).astype(BF16)
    inv = (jnp.stack([jnp.cos(ang).T, -jnp.sin(ang).T]) * (1.0 / n)).astype(BF16)
    return fwd, inv


def _hyena_tile_kernel(u_ref, w_ref, b_ref, hf_ref, hb_ref, fb_ref, fwd_ref, inv_ref, o_ref):
    w = HYENA_WIDTH
    n = 2 * TM
    x0 = _short_conv_tiles(u_ref.at[:, :, 0:w], w_ref[0], b_ref[0])[0]
    x1 = _short_conv_tiles(u_ref.at[:, :, w:2 * w], w_ref[1], b_ref[1])[0]
    v = _short_conv_tiles(u_ref.at[:, :, 2 * w:3 * w], w_ref[2], b_ref[2])[0]
    g = (v * x1).astype(BF16)
    fwd = fwd_ref[...]
    dot = functools.partial(jnp.dot, preferred_element_type=F32)
    kf = dot(hf_ref[0].astype(BF16), fwd)
    kb = dot(hb_ref[0].astype(BF16), fwd)
    kr = kf[:, :n] + kb[:, :n]
    ki = kf[:, n:] - kb[:, n:]
    x = dot(g, fwd)
    xr, xi = x[:, :n], x[:, n:]
    yr = (xr * kr - xi * ki).astype(BF16)
    yi = (xr * ki + xi * kr).astype(BF16)
    y = dot(yr, inv_ref[0]) + dot(yi, inv_ref[1])
    o_ref[0, 0] = (x0 * (y + fb_ref[...] * g.astype(F32))).astype(o_ref.dtype)


def _hyena_one_tile(u, cw, cb, hf, hb, fbias, consts):
    b = u.shape[0]
    w = HYENA_WIDTH
    fwd, inv = consts
    full = lambda a: pl.BlockSpec(a.shape, lambda bi: (0,) * a.ndim)
    return pl.pallas_call(
        _hyena_tile_kernel,
        grid=(b,),
        in_specs=[pl.BlockSpec((1, 1, 3 * w, TM), lambda bi: (bi, 0, 0, 0)),
                  full(cw), full(cb), full(hf), full(hb), full(fbias), full(fwd), full(inv)],
        out_specs=pl.BlockSpec((1, 1, w, TM), lambda bi: (bi, 0, 0, 0)),
        out_shape=jax.ShapeDtypeStruct((b, 1, w, TM), BF16),
        compiler_params=_params("arbitrary"),
        name="hyena_context",
    )(u, cw, cb, hf, hb, fbias, fwd, inv)


def _top2_sum(a, b, c, d):
    hi1, lo1 = jnp.maximum(a, b), jnp.minimum(a, b)
    hi2, lo2 = jnp.maximum(c, d), jnp.minimum(c, d)
    return jnp.maximum(hi1, hi2) + jnp.maximum(jnp.minimum(hi1, hi2), jnp.maximum(lo1, lo2))


def _outproj_kernel(a_ref, y_ref, x_ref, mod_ref, ga_ref, gy_ref, wo_ref, n2_ref, rw_ref, rb_ref,
                    cnt_ref, tri_ref, xo_ref, row_ref, route_ref, cnto_ref, cnt_sc):
    first = jnp.logical_and(pl.program_id(0) == 0, pl.program_id(1) == 0)

    @pl.when(first)
    def _():
        cnt_sc[...] = cnt_ref[...]

    counts = cnt_sc[...]
    for s in range(a_ref.shape[1]):
        counts = _outproj_tile(s, counts, a_ref, y_ref, x_ref, mod_ref, ga_ref, gy_ref, wo_ref, n2_ref,
                               rw_ref, rb_ref, tri_ref, xo_ref, row_ref, route_ref)
    cnt_sc[...] = counts
    cnto_ref[...] = counts


def _outproj_tile(s, base, a_ref, y_ref, x_ref, mod_ref, ga_ref, gy_ref, wo_ref, n2_ref, rw_ref, rb_ref,
                  tri_ref, xo_ref, row_ref, route_ref):
    rows = slice(s * TM, (s + 1) * TM)

    def group_norm(t, gain):
        t = t.astype(F32)
        return t * lax.rsqrt(jnp.mean(t * t, axis=0, keepdims=True) + EPS) * gain

    mix = jnp.concatenate([group_norm(a_ref[0, s], ga_ref[...]),
                           group_norm(y_ref[0, s], gy_ref[...])], axis=0).astype(BF16)
    o = lax.dot_general(mix, wo_ref[...], (((0,), (0,)), ((), ())), preferred_element_type=F32)
    xn = x_ref[0, rows] + mod_ref[0, 2:3, :] * o
    xo_ref[0, rows] = xn
    h2 = _norm_mod(xn, n2_ref[...], mod_ref[0, 3:4, :], mod_ref[0, 4:5, :])

    tm = xn.shape[0]
    nt_dims = (((1,), (1,)), ((), ()))
    h_hi = h2.astype(BF16)
    h_lo = (h2 - h_hi.astype(F32)).astype(BF16)
    rw = rw_ref[...]
    part = lax.dot_general(rw, h_hi, nt_dims, preferred_element_type=F32)
    logits = (part[:N_EXPERTS] + part[N_EXPERTS:]
              + lax.dot_general(rw[:N_EXPERTS], h_lo, nt_dims, preferred_element_type=F32))
    score = jax.nn.sigmoid(logits)
    sel = score + rb_ref[...]
    srow = [sel[e:e + 1] for e in range(N_EXPERTS)]
    prow = [score[e:e + 1] for e in range(N_EXPERTS)]
    epg = EXPERTS_PER_GROUP
    gs = [_top2_sum(*srow[g * epg:(g + 1) * epg]) for g in range(N_EXPERT_GROUPS)]
    bg = jnp.zeros((1, tm), I32)
    best = gs[0]
    for g in range(1, N_EXPERT_GROUPS):
        upd = gs[g] > best
        bg = jnp.where(upd, g, bg)
        best = jnp.where(upd, gs[g], best)

    def pick_group(rows, j):
        out = rows[j]
        for g in range(1, N_EXPERT_GROUPS):
            out = jnp.where(bg == g, rows[g * epg + j], out)
        return out

    cand = [pick_group(srow, j) for j in range(epg)]
    cprob = [pick_group(prow, j) for j in range(epg)]
    i1 = jnp.zeros((1, tm), I32)
    v1, w1 = cand[0], cprob[0]
    for j in range(1, epg):
        upd = cand[j] > v1
        i1 = jnp.where(upd, j, i1)
        v1 = jnp.where(upd, cand[j], v1)
        w1 = jnp.where(upd, cprob[j], w1)
    i2 = jnp.zeros((1, tm), I32)
    v2 = jnp.full((1, tm), -jnp.inf, F32)
    w2 = jnp.zeros((1, tm), F32)
    for j in range(epg):
        upd = jnp.logical_and(i1 != j, cand[j] > v2)
        i2 = jnp.where(upd, j, i2)
        v2 = jnp.where(upd, cand[j], v2)
        w2 = jnp.where(upd, cprob[j], w2)
    den = w1 + w2
    w1, w2 = w1 / den, w2 / den
    first_lower = i1 < i2
    lo = jnp.where(first_lower, i1, i2)
    hi = jnp.where(first_lower, i2, i1)
    w_lo = jnp.where(first_lower, w1, w2)
    w_hi = jnp.where(first_lower, w2, w1)
    pair = jnp.where(lo == 0, hi - 1, jnp.where(lo == 1, hi + 1, PAIRS_PER_GROUP - 1))
    cls = bg * PAIRS_PER_GROUP + pair

    cio = lax.broadcasted_iota(I32, (N_CLASS_ROWS, tm), 0)
    onehot = jnp.where(cio == cls, 1.0, 0.0)
    cum = jnp.dot(onehot.astype(BF16), tri_ref[...], preferred_element_type=F32)
    rank = jnp.sum(onehot * (base + cum), axis=0, keepdims=True)
    route_ref[0, s] = jnp.concatenate([cls, rank.astype(I32)], axis=0)

    half = h2.shape[1] // 2
    wrows = jnp.concatenate([w_lo, w_hi, jnp.zeros((ROW_EXTRA - 2, tm), F32)], axis=0)
    row_ref[0, rows] = jnp.concatenate([_pack_bf16_pair(h2[:, :half], h2[:, half:]),
                                        lax.bitcast_convert_type(wrows.T, U32)], axis=1)
    return base + jnp.sum(onehot, axis=1, keepdims=True)


def _outproj_route(a, y, x, mods, mod_row, ga, gy, wo, n2g, rwt, rb, cnt, tri):
    b, l, d = x.shape
    nt = l // TM
    if mod_row is None:
        mod_map = lambda bi, i: (bi, 0, 0)
    else:
        mod_map = lambda bi, i: (mod_row, 0, 0)
    full = lambda arr: pl.BlockSpec(arr.shape, lambda bi, i: (0,) * arr.ndim)
    sub = OUTPROJ_TILES if nt % OUTPROJ_TILES == 0 else 1
    tile = pl.BlockSpec((1, sub, ATTN_WIDTH, TM), lambda bi, i: (bi, i, 0, 0))
    xspec = pl.BlockSpec((1, sub * TM, d), lambda bi, i: (bi, i, 0))
    rspec = pl.BlockSpec((1, sub, 2, TM), lambda bi, i: (bi, i, 0, 0))
    row_w = d // 2 + ROW_EXTRA
    return pl.pallas_call(
        _outproj_kernel,
        grid=(b, nt // sub),
        in_specs=[tile, tile, xspec, pl.BlockSpec((1, N_MOD, d), mod_map),
                  full(ga), full(gy), full(wo), full(n2g), full(rwt), full(rb), full(cnt), full(tri)],
        out_specs=[xspec, pl.BlockSpec((1, sub * TM, row_w), lambda bi, i: (bi, i, 0)), rspec, full(cnt)],
        out_shape=[jax.ShapeDtypeStruct((b, l, d), F32),
                   jax.ShapeDtypeStruct((b, l, row_w), U32),
                   jax.ShapeDtypeStruct((b, nt, 2, TM), I32),
                   jax.ShapeDtypeStruct(cnt.shape, F32)],
        scratch_shapes=[pltpu.VMEM(cnt.shape, F32)],
        compiler_params=_params("arbitrary", "arbitrary"),
        name="outproj_route",
    )(a, y, x, mods, ga, gy, wo, n2g, rwt, rb, cnt, tri)


def _dispatch_kernel(slot_ref, h_ref, xs_in_ref, xs_ref, stage, sem, *, nstep):
    del xs_in_ref
    tm = h_ref.shape[0]
    step = pl.program_id(0)
    cur = lax.rem(step, 2)

    def copy(side, r, s):
        return pltpu.make_async_copy(stage.at[side, pl.ds(r, 1)], xs_ref.at[pl.ds(s, 1)], sem.at[side])

    def drain(side):
        def wait_one(r, carry):
            copy(side, 0, 0).wait()
            return carry

        lax.fori_loop(0, tm, wait_one, 0, unroll=8)

    @pl.when(step >= 2)
    def _():
        drain(cur)

    stage[cur] = h_ref[...]

    def issue(r, carry):
        copy(cur, r, slot_ref[0, 0, r]).start()
        return carry

    lax.fori_loop(0, tm, issue, 0, unroll=8)

    @pl.when(step == nstep - 1)
    def _():
        @pl.when(step >= 1)
        def _():
            drain(1 - cur)

        drain(cur)


def _dispatch(slots, h2, xs):
    t, d = h2.shape
    return pl.pallas_call(
        functools.partial(_dispatch_kernel, nstep=t // TM),
        grid=(t // TM,),
        in_specs=[pl.BlockSpec((1, 1, TM), lambda i: (i, 0, 0), memory_space=pltpu.SMEM),
                  pl.BlockSpec((TM, d), lambda i: (i, 0)),
                  pl.BlockSpec(memory_space=pl.ANY)],
        out_specs=pl.BlockSpec(memory_space=pl.ANY),
        out_shape=jax.ShapeDtypeStruct(xs.shape, xs.dtype),
        scratch_shapes=[pltpu.VMEM((2, TM, d), xs.dtype), pltpu.SemaphoreType.DMA((2,))],
        input_output_aliases={2: 0},
        compiler_params=_params("arbitrary"),
        name="moe_dispatch",
    )(slots, h2, xs)


def _ffn_kernel(ea_ref, eb_ref, nt_ref, xs_ref, wga_ref, wua_ref, wda_ref, wgb_ref, wub_ref, wdb_ref, ys_ref):
    del ea_ref, eb_ref
    live = pl.program_id(0) < nt_ref[0]
    half = ys_ref.shape[1]

    @pl.when(live)
    def _():
        words = xs_ref[...]
        lo, hi = _unpack_bf16_pair(words[:, :half])
        lo, hi = lo.astype(BF16), hi.astype(BF16)
        wts = lax.bitcast_convert_type(words[:, half:half + 2], F32)

        def expert(wg_ref, wu_ref, wd_ref):
            dot = functools.partial(jnp.dot, preferred_element_type=F32)
            g = dot(lo, wg_ref[0, :half, :]) + dot(hi, wg_ref[0, half:, :])
            u = dot(lo, wu_ref[0, :half, :]) + dot(hi, wu_ref[0, half:, :])
            a = (g * jax.nn.sigmoid(g)) * u
            return dot(a.astype(BF16), wd_ref[0])

        y = (wts[:, 0:1] * expert(wga_ref, wua_ref, wda_ref)
             + wts[:, 1:2] * expert(wgb_ref, wub_ref, wdb_ref))
        ys_ref[...] = _pack_bf16_pair(y[:, :half], y[:, half:])

    @pl.when(jnp.logical_not(live))
    def _():
        ys_ref[...] = jnp.zeros_like(ys_ref)


def _expert_ffn(tile_ea, tile_eb, n_tiles, xs, wg, wu, wd):
    nslot, row_w = xs.shape
    d, f = wg.shape[1:]
    ntile = nslot // FFN_TILE
    row = lambda i, ea, eb, nt: (jnp.minimum(i, nt[0] - 1), 0)
    amap = lambda i, ea, eb, nt: (ea[jnp.minimum(i, nt[0] - 1)], 0, 0)
    bmap = lambda i, ea, eb, nt: (eb[jnp.minimum(i, nt[0] - 1)], 0, 0)
    return pl.pallas_call(
        _ffn_kernel,
        grid_spec=pltpu.PrefetchScalarGridSpec(
            num_scalar_prefetch=3,
            grid=(ntile,),
            in_specs=[pl.BlockSpec((FFN_TILE, row_w), row),
                      pl.BlockSpec((1, d, f), amap), pl.BlockSpec((1, d, f), amap), pl.BlockSpec((1, f, d), amap),
                      pl.BlockSpec((1, d, f), bmap), pl.BlockSpec((1, d, f), bmap), pl.BlockSpec((1, f, d), bmap)],
            out_specs=pl.BlockSpec((FFN_TILE, d // 2), lambda i, ea, eb, nt: (i, 0))),
        out_shape=jax.ShapeDtypeStruct((nslot, d // 2), U32),
        compiler_params=_params("arbitrary"),
        name="moe_experts",
    )(tile_ea, tile_eb, n_tiles, xs, wg, wu, wd, wg, wu, wd)


def _combine_kernel(slot_ref, next_slot_ref, x_ref, mod_ref, ys_ref, o_ref, buf, sem, *, nt, nstep):
    tm = x_ref.shape[1]
    step = pl.program_id(0) * nt + pl.program_id(1)
    cur = lax.rem(step, 2)

    def copy(side, r, s):
        return pltpu.make_async_copy(ys_ref.at[pl.ds(s, 1)], buf.at[side, pl.ds(r, 1)], sem.at[side])

    def gather(side, slots):
        def issue(r, carry):
            copy(side, r, slots[0, 0, r]).start()
            return carry

        lax.fori_loop(0, tm, issue, 0, unroll=8)

    @pl.when(step == 0)
    def _():
        gather(cur, slot_ref)

    @pl.when(step + 1 < nstep)
    def _():
        gather(1 - cur, next_slot_ref)

    def wait_one(r, carry):
        copy(cur, 0, 0).wait()
        return carry

    lax.fori_loop(0, tm, wait_one, 0, unroll=8)
    lo, hi = _unpack_bf16_pair(buf[cur])
    y = jnp.concatenate([lo, hi], axis=1)
    o_ref[0] = x_ref[0] + mod_ref[0, 5:6, :] * y


def _combine(slots, x, mods, mod_row, ys):
    b, l, d = x.shape
    nt = l // TM
    if mod_row is None:
        mod_map = lambda bi, i: (bi, 0, 0)
    else:
        mod_map = lambda bi, i: (mod_row, 0, 0)
    xspec = pl.BlockSpec((1, TM, d), lambda bi, i: (bi, i, 0))
    last = b * nt - 1
    return pl.pallas_call(
        functools.partial(_combine_kernel, nt=nt, nstep=b * nt),
        grid=(b, nt),
        in_specs=[pl.BlockSpec((1, 1, TM), lambda bi, i: (bi * nt + i, 0, 0), memory_space=pltpu.SMEM),
                  pl.BlockSpec((1, 1, TM), lambda bi, i: (jnp.minimum(bi * nt + i + 1, last), 0, 0),
                               memory_space=pltpu.SMEM),
                  xspec,
                  pl.BlockSpec((1, N_MOD, d), mod_map),
                  pl.BlockSpec(memory_space=pl.ANY)],
        out_specs=xspec,
        out_shape=jax.ShapeDtypeStruct((b, l, d), F32),
        scratch_shapes=[pltpu.VMEM((2, TM, d // 2), U32), pltpu.SemaphoreType.DMA((2,))],
        compiler_params=_params("arbitrary", "arbitrary"),
        name="moe_combine",
    )(slots, slots, x, mods, ys)


def _moe(streams, wg, wu, wd, xs, token_cap):
    counts = streams[-1]["counts"].reshape(-1).astype(I32)
    padded = ((counts + FFN_TILE - 1) // FFN_TILE) * FFN_TILE
    ends = jnp.cumsum(padded)
    offs = ends - padded
    ntile = token_cap // FFN_TILE + N_CLASSES
    nslot = ntile * FFN_TILE
    tile_start = jnp.arange(ntile, dtype=I32) * FFN_TILE
    tile_class = jnp.minimum(jnp.sum((tile_start[:, None] >= ends[None, :N_CLASSES]).astype(I32), axis=1),
                             N_CLASSES - 1)
    group_base = (tile_class // PAIRS_PER_GROUP) * EXPERTS_PER_GROUP
    tile_ea = (group_base + jnp.asarray(PAIR_LO, I32)[tile_class % PAIRS_PER_GROUP]).astype(I32)
    tile_eb = (group_base + jnp.asarray(PAIR_HI, I32)[tile_class % PAIRS_PER_GROUP]).astype(I32)
    n_tiles = (ends[-1] // FFN_TILE).astype(I32).reshape(1)

    row_w = streams[0]["rows"].shape[-1]
    if xs is None:
        xs = jnp.zeros((nslot, row_w), U32)
    for s in streams:
        b, l, _ = s["x"].shape
        cls, rank = s["route"][:, :, 0], s["route"][:, :, 1]
        slot = rank
        for c in range(N_CLASSES):
            slot = slot + jnp.where(cls == c, offs[c], 0)
        s["slots"] = slot.reshape(b * (l // TM), 1, TM)
        xs = _dispatch(s["slots"], s["rows"].reshape(b * l, row_w), xs)
    ys = _expert_ffn(tile_ea, tile_eb, n_tiles, xs, wg, wu, wd)
    return [_combine(s["slots"], s["x"], s["mods"], s["mod_row"], ys) for s in streams], xs


def _rope_table(l):
    t = jnp.arange(l)
    row = (t // GRID_W).astype(F32)
    col = (t % GRID_W).astype(F32)
    inv_freq = ROPE_THETA ** (-jnp.arange(ROPE_QUARTER, dtype=F32) / ROPE_QUARTER)
    ar = inv_freq[:, None] * row[None, :]
    ac = inv_freq[:, None] * col[None, :]
    return jnp.concatenate([jnp.cos(ar), jnp.sin(ar), jnp.cos(ac), jnp.sin(ac)], axis=0)


def _identity_rope_table(l):
    one = jnp.ones((ROPE_QUARTER, l), F32)
    zero = jnp.zeros((ROPE_QUARTER, l), F32)
    return jnp.concatenate([one, zero, one, zero], axis=0)


def kernel(x, c, ctx, c_ctx, w_mod, b_mod, norm1_g, w_in, q_norm_g, k_norm_g, conv_w, conv_b, filt_w1, filt_b1, filt_w2, filt_b2, filt_w3, filt_freq, filt_bias, attn_out_g, hyena_out_g, w_out, norm2_g, router_w, router_bias, expert_w_gate, expert_w_up, expert_w_down):
    depth = w_mod.shape[0]
    b, l, d = x.shape
    lc = ctx.shape[1]
    assert l % TM == 0 and lc == TM and d == D_MODEL
    nt = l // TM
    w = HYENA_WIDTH

    mods = _modulations(c, c_ctx, w_mod, b_mod)
    rope_lat = _rope_table(l)
    rope_ctx = _identity_rope_table(lc)
    dft = _dft_consts(nt)
    dft_ctx = _dense_dft_consts()
    tri = (jnp.arange(TM)[:, None] < jnp.arange(TM)[None, :]).astype(BF16)
    rwt = router_w.T
    rw_hi = rwt.astype(BF16)
    rwt = jnp.concatenate([rw_hi, (rwt - rw_hi.astype(F32)).astype(BF16)], axis=0)
    rb = router_bias.reshape(-1, 1)
    zero_cnt = jnp.zeros((N_CLASS_ROWS, 1), F32)
    xs = None
    col = lambda a: a.reshape(-1, 1)

    for li in range(depth):
        last = li == depth - 1
        m = mods[li]
        wt = w_in[li].T.astype(BF16)
        wo = w_out[li].astype(BF16)
        g1n = norm1_g[li].reshape(1, d)
        g2n = norm2_g[li].reshape(1, d)
        qg, kg = col(q_norm_g[li]), col(k_norm_g[li])
        cw = conv_w[li].T.reshape(3, w, SHORT_CONV)
        cb = conv_b[li].reshape(3, w, 1)
        ga, gy = col(attn_out_g[li]), col(hyena_out_g[li])
        fargs = (filt_w1[li], filt_b1[li], filt_w2[li], filt_b2[li], filt_w3[li], filt_freq[li])

        q, k, v, u = _inproj(x, m, None, g1n, wt, qg, kg, rope_lat)
        qc, kc, vc, uc = _inproj(ctx, m, b, g1n, wt, qg, kg, rope_ctx)
        a = _attention(q, [(kc, vc), (k, v)])
        hf, hb = _implicit_filters(l, *fargs)
        kf = _filter_spectrum(hf, hb, dft)
        g, x0 = _short_conv_gate(u, cw, cb)
        fb_lanes = jnp.repeat(filt_bias[li], TM).reshape(1, w * TM)
        y = _fft_conv(g, x0, kf, fb_lanes, dft)
        lat = dict(zip(("x", "rows", "route", "counts"),
                       _outproj_route(a, y, x, m, None, ga, gy, wo, g2n, rwt, rb, zero_cnt, tri)))
        lat.update(mods=m, mod_row=None)
        streams = [lat]

        if not last:
            ac = _attention(qc, [(kc, vc)])
            hfc, hbc = _implicit_filters(lc, *fargs)
            yc = _hyena_one_tile(uc, cw, cb, hfc, hbc, col(filt_bias[li]), dft_ctx)
            cs = dict(zip(("x", "rows", "route", "counts"),
                          _outproj_route(ac, yc, ctx, m, b, ga, gy, wo, g2n, rwt, rb, lat["counts"], tri)))
            cs.update(mods=m, mod_row=b)
            streams.append(cs)

        outs, xs = _moe(streams, expert_w_gate[li].astype(BF16), expert_w_up[li].astype(BF16),
                        expert_w_down[li].astype(BF16), xs, b * (l + lc))
        x = outs[0]
        if not last:
            ctx = outs[1]
    return x
```
